```python
import math
import jax, jax.numpy as jnp
from jax import lax
import numpy as np


D_MODEL = 1024
BATCH = 4
SEQ = 8192
DEPTH = 2

CHUNK = 64
N_MIXERS = 2
N_S5_LAYERS = (DEPTH + 1) // 2
N_RET_LAYERS = DEPTH // 2

S5_GROUP = 16
S5_GROUPS = D_MODEL // S5_GROUP
S5_STATE = 64
S5_DT_MIN = 1e-3
S5_DT_MAX = 1e-1

RET_QK_DIM = 256
RET_HEADS = D_MODEL // RET_QK_DIM
RET_V_DIM = 2 * RET_QK_DIM
RET_QK_WIDTH = RET_HEADS * RET_QK_DIM
RET_V_WIDTH = RET_HEADS * RET_V_DIM
RET_PROJ_WIDTH = 2 * RET_QK_WIDTH + 2 * RET_V_WIDTH
ROPE_BASE = 10000.0

FFN_HIDDEN = -(-8 * D_MODEL // (3 * 256)) * 256
NORM_EPS = 1e-6

kernel_name = 'chunk_causal_s5_retention_hybrid'


def rmsnorm(x, g):
    xf = x.astype(jnp.float32)
    y = xf * lax.rsqrt(jnp.mean(jnp.square(xf), axis=-1, keepdims=True) + NORM_EPS)
    return (y * g.astype(jnp.float32)).astype(x.dtype)


def swiglu(u, w_gate, w_up, w_down):
    h = jax.nn.silu(u @ w_gate.astype(u.dtype)) * (u @ w_up.astype(u.dtype))
    return h @ w_down.astype(u.dtype)


def _linear_recurrence_combine(e1, e2):
    a1, b1 = e1
    a2, b2 = e2
    return a1 * a2, a2 * b1 + b2


def s5_mixer(u, lam_re, lam_im, log_step, b_re, b_im, c_re, c_im, d, glu_w, glu_b):
    f32 = jnp.float32
    bsz, seq, _ = u.shape
    uf = u.astype(f32).reshape(bsz, seq, S5_GROUPS, S5_GROUP)
    lam = lax.complex(lam_re.astype(f32), lam_im.astype(f32))
    step = jnp.exp(log_step.astype(f32))[:, None]
    lam_bar = jnp.exp(lam * step)
    b = lax.complex(b_re.astype(f32), b_im.astype(f32))
    b_bar = ((lam_bar - 1.0) / lam)[..., None] * b
    bu = jnp.einsum('blgn,gpn->blgp', uf.astype(jnp.complex64), b_bar)
    a = jnp.broadcast_to(lam_bar, (1, seq, S5_GROUPS, S5_STATE))
    _, states = lax.associative_scan(_linear_recurrence_combine, (a, bu), axis=1)
    c = lax.complex(c_re.astype(f32), c_im.astype(f32))
    y = jnp.einsum('gnp,blgp->blgn', c, states).real + d.astype(f32) * uf
    y = jax.nn.gelu(y.reshape(bsz, seq, D_MODEL))
    y = y * jax.nn.sigmoid(y @ glu_w.astype(f32) + glu_b.astype(f32))
    return y.astype(u.dtype)


def _rotary_tables(seq):
    inv_freq = 1.0 / (ROPE_BASE ** jnp.linspace(0.0, 1.0, RET_QK_DIM // 2, dtype=jnp.float32))
    ang = jnp.arange(seq, dtype=jnp.float32)[:, None] * inv_freq[None, :]
    return jnp.cos(ang)[None, :, None, :], jnp.sin(ang)[None, :, None, :]


def _apply_rotary(t, cos, sin):
    t1, t2 = jnp.split(t, 2, axis=-1)
    return jnp.concatenate([t1 * cos - t2 * sin, t1 * sin + t2 * cos], axis=-1)


def retention_mixer(u, w_qkvg, gn_w, w_o):
    f32 = jnp.float32
    bsz, seq, _ = u.shape
    n_chunks = seq // CHUNK
    proj = jnp.matmul(u, w_qkvg.astype(u.dtype)).astype(f32)
    q, k, v, g = jnp.split(proj, [RET_QK_WIDTH, 2 * RET_QK_WIDTH, 2 * RET_QK_WIDTH + RET_V_WIDTH], axis=-1)
    q = q.reshape(bsz, seq, RET_HEADS, RET_QK_DIM)
    k = k.reshape(bsz, seq, RET_HEADS, RET_QK_DIM) * (RET_QK_DIM ** -0.5)
    v = v.reshape(bsz, seq, RET_HEADS, RET_V_DIM)
    cos, sin = _rotary_tables(seq)
    q = _apply_rotary(q, cos, sin)
    k = _apply_rotary(k, cos, sin)

    log_gamma = jnp.log1p(-jnp.exp2(-5.0 - jnp.arange(RET_HEADS, dtype=f32)))
    pos = jnp.arange(CHUNK, dtype=f32)
    intra_decay = jnp.exp(log_gamma[:, None, None] * jnp.abs(pos[:, None] - pos[None, :]))
    q_decay = jnp.exp((pos[:, None] + 1.0) * log_gamma[None, :])
    k_decay = jnp.exp((CHUNK - 1.0 - pos)[:, None] * log_gamma[None, :])
    chunk_decay = jnp.exp(CHUNK * log_gamma)

    def to_chunks(t):
        return t.reshape(bsz, n_chunks, CHUNK, RET_HEADS, t.shape[-1]).swapaxes(0, 1)

    def step(state, qkv):
        qc, kc, vc = qkv
        scores = jnp.einsum('bihd,bjhd->bhij', qc, kc) * intra_decay
        o_intra = jnp.einsum('bhij,bjhe->bihe', scores, vc)
        o_cross = jnp.einsum('bihd,bhde->bihe', qc * q_decay[None, :, :, None], state)
        new_state = state * chunk_decay[None, :, None, None] + jnp.einsum(
            'bjhd,bjhe->bhde', kc * k_decay[None, :, :, None], vc)
        return new_state, o_intra + o_cross

    state0 = jnp.zeros((bsz, RET_HEADS, RET_QK_DIM, RET_V_DIM), f32)
    _, o = lax.scan(step, state0, (to_chunks(q), to_chunks(k), to_chunks(v)))
    o = o.swapaxes(0, 1).reshape(bsz, seq, RET_HEADS, RET_V_DIM)
    mean = jnp.mean(o, axis=-1, keepdims=True)
    var = jnp.mean(jnp.square(o - mean), axis=-1, keepdims=True)
    o = ((o - mean) * lax.rsqrt(var + NORM_EPS)).reshape(bsz, seq, RET_V_WIDTH) * gn_w.astype(f32)
    y = jax.nn.silu(g) * o
    return jnp.matmul(y.astype(u.dtype), w_o.astype(u.dtype))


def setup_inputs(seed: int = 0) -> dict:
    key = jax.random.key(seed)
    ks = jax.random.split(key, 24)
    f32 = jnp.float32
    nrm = lambda k, s, sc: jax.random.normal(k, s, f32) * sc
    x = jax.random.normal(ks[0], (BATCH, SEQ, D_MODEL), f32)

    s5_norm = 1.0 + nrm(ks[1], (N_S5_LAYERS, D_MODEL), 0.02)
    s5_lambda_re = -0.5 + nrm(ks[2], (N_S5_LAYERS, S5_GROUPS, S5_STATE), 0.01)
    s5_lambda_im = (math.pi * jnp.arange(S5_STATE, dtype=f32))[None, None, :] + nrm(
        ks[3], (N_S5_LAYERS, S5_GROUPS, S5_STATE), 0.01)
    s5_log_step = jax.random.uniform(ks[4], (N_S5_LAYERS, S5_GROUPS), f32,
                                     math.log(S5_DT_MIN), math.log(S5_DT_MAX))
    b_scale = (2.0 * S5_GROUP) ** -0.5
    s5_b_re = nrm(ks[5], (N_S5_LAYERS, S5_GROUPS, S5_STATE, S5_GROUP), b_scale)
    s5_b_im = nrm(ks[6], (N_S5_LAYERS, S5_GROUPS, S5_STATE, S5_GROUP), b_scale)
    c_scale = S5_STATE ** -0.5
    s5_c_re = nrm(ks[7], (N_S5_LAYERS, S5_GROUPS, S5_GROUP, S5_STATE), c_scale)
    s5_c_im = nrm(ks[8], (N_S5_LAYERS, S5_GROUPS, S5_GROUP, S5_STATE), c_scale)
    s5_d = nrm(ks[9], (N_S5_LAYERS, S5_GROUPS, S5_GROUP), 1.0)
    s5_glu_w = nrm(ks[10], (N_S5_LAYERS, D_MODEL, D_MODEL), D_MODEL ** -0.5)
    s5_glu_b = nrm(ks[11], (N_S5_LAYERS, D_MODEL), 0.01)

    ret_norm = 1.0 + nrm(ks[12], (N_RET_LAYERS, D_MODEL), 0.02)
    ret_w_qkvg = nrm(ks[13], (N_RET_LAYERS, D_MODEL, RET_PROJ_WIDTH), D_MODEL ** -0.5)
    ret_gn_w = 1.0 + nrm(ks[14], (N_RET_LAYERS, RET_V_WIDTH), 0.02)
    ret_w_o = nrm(ks[15], (N_RET_LAYERS, RET_V_WIDTH, D_MODEL), RET_V_WIDTH ** -0.5)

    ffn_norm = 1.0 + nrm(ks[16], (DEPTH, D_MODEL), 0.02)
    ffn_w_gate = nrm(ks[17], (DEPTH, D_MODEL, FFN_HIDDEN), D_MODEL ** -0.5)
    ffn_w_up = nrm(ks[18], (DEPTH, D_MODEL, FFN_HIDDEN), D_MODEL ** -0.5)
    ffn_w_down = nrm(ks[19], (DEPTH, FFN_HIDDEN, D_MODEL), FFN_HIDDEN ** -0.5)
    final_norm = 1.0 + nrm(ks[20], (D_MODEL,), 0.02)
    return {'x': x, 's5_norm': s5_norm, 's5_lambda_re': s5_lambda_re, 's5_lambda_im': s5_lambda_im,
            's5_log_step': s5_log_step, 's5_b_re': s5_b_re, 's5_b_im': s5_b_im,
            's5_c_re': s5_c_re, 's5_c_im': s5_c_im, 's5_d': s5_d,
            's5_glu_w': s5_glu_w, 's5_glu_b': s5_glu_b,
            'ret_norm': ret_norm, 'ret_w_qkvg': ret_w_qkvg, 'ret_gn_w': ret_gn_w, 'ret_w_o': ret_w_o,
            'ffn_norm': ffn_norm, 'ffn_w_gate': ffn_w_gate, 'ffn_w_up': ffn_w_up,
            'ffn_w_down': ffn_w_down, 'final_norm': final_norm}


def reference(x, s5_norm, s5_lambda_re, s5_lambda_im, s5_log_step, s5_b_re, s5_b_im,
              s5_c_re, s5_c_im, s5_d, s5_glu_w, s5_glu_b,
              ret_norm, ret_w_qkvg, ret_gn_w, ret_w_o,
              ffn_norm, ffn_w_gate, ffn_w_up, ffn_w_down, final_norm):
    for i in range(DEPTH):
        j = i // N_MIXERS
        if i % N_MIXERS == 0:
            x = x + s5_mixer(rmsnorm(x, s5_norm[j]), s5_lambda_re[j], s5_lambda_im[j],
                             s5_log_step[j], s5_b_re[j], s5_b_im[j], s5_c_re[j], s5_c_im[j],
                             s5_d[j], s5_glu_w[j], s5_glu_b[j])
        else:
            x = x + retention_mixer(rmsnorm(x, ret_norm[j]), ret_w_qkvg[j], ret_gn_w[j], ret_w_o[j])
        x = x + swiglu(rmsnorm(x, ffn_norm[i]), ffn_w_gate[i], ffn_w_up[i], ffn_w_down[i])
    return rmsnorm(x, final_norm)
```

```python
import functools
import math

import jax
import jax.numpy as jnp
from jax import lax
from jax.experimental import pallas as pl
from jax.experimental.pallas import tpu as pltpu

F32 = jnp.float32
BF16 = jnp.bfloat16

NORM_EPS = 1e-6
CHUNK = 64
S5_GROUP = 16
S5_STATE = 64
S5_SUB = 16
RET_QK_DIM = 256
RET_V_DIM = 512
ROPE_BASE = 10000.0

V7X_VMEM_BYTES = 64 * 1024 * 1024
V7X_LANES = 128
VMEM_LIMIT = 56 * 1024 * 1024


def _tiles(seq):
    row = min(512, seq)
    ret = min(256, seq)
    assert seq % row == 0 and seq % ret == 0 and ret % CHUNK == 0
    return row, ret


def _cparams(sem):
    return pltpu.CompilerParams(dimension_semantics=sem, vmem_limit_bytes=VMEM_LIMIT)


def _resident(shape):
    nd = len(shape)
    return pl.BlockSpec(shape, lambda *_: (0,) * nd, pipeline_mode=pl.Buffered(1))


def _rms(xf, g):
    ms = jnp.mean(jnp.square(xf), axis=-1, keepdims=True)
    return xf * lax.rsqrt(ms + NORM_EPS) * g


def _dot(a, b):
    return jnp.dot(a, b, preferred_element_type=F32)


def _norm_kernel(x_ref, g_ref, o_ref):
    o_ref[...] = _rms(x_ref[...], g_ref[...]).astype(o_ref.dtype)


def _norm_call(x, g, row):
    t, d = x.shape
    return pl.pallas_call(
        _norm_kernel,
        grid=(t // row,),
        in_specs=[pl.BlockSpec((row, d), lambda i: (i, 0)), _resident((1, d))],
        out_specs=pl.BlockSpec((row, d), lambda i: (i, 0)),
        out_shape=jax.ShapeDtypeStruct((t, d), BF16),
        compiler_params=_cparams(("parallel",)),
        name="s5_norm",
    )(x, g.reshape(1, d))


def _s5_core_kernel(z_ref, tt_ref, nn_ref, mm_ref, a_ref, y_ref, s_scr, ssw_scr, xp_scr,
                    *, batch, cb):
    z = z_ref[0]
    s = _dot(z, nn_ref[0])
    s_scr[...] = s
    ssw_scr[...] = pltpu.roll(s, 64, axis=1)
    a = a_ref[0]
    ar = a[0:1, :]
    ai_p = a[1:2, :]
    ai_q = a[2:3, :]

    def body(c, carry):
        p, q = carry
        rows = pl.ds(c, batch, stride=cb)
        xp_scr[rows, :] = p
        sp = s_scr[rows, :]
        sq = ssw_scr[rows, :]
        return ar * p + ai_p * q + sp, ar * q + ai_q * p + sq

    zero = jnp.zeros((batch, 128), F32)
    lax.fori_loop(0, cb, body, (zero, zero), unroll=8)
    y = _dot(z, tt_ref[0]) + _dot(xp_scr[...].astype(BF16), mm_ref[0])
    y_ref[0] = y


def _s5_core_call(z, tt, nn, mm, a, batch):
    g, nc, w = z.shape
    cb = nc // batch
    kern = functools.partial(_s5_core_kernel, batch=batch, cb=cb)
    return pl.pallas_call(
        kern,
        grid=(g,),
        in_specs=[
            pl.BlockSpec((1, nc, w), lambda i: (i, 0, 0)),
            pl.BlockSpec((1, w, w), lambda i: (i, 0, 0)),
            pl.BlockSpec((1, w, 128), lambda i: (i, 0, 0)),
            pl.BlockSpec((1, 128, w), lambda i: (i, 0, 0)),
            pl.BlockSpec((1, 4, 128), lambda i: (i, 0, 0)),
        ],
        out_specs=pl.BlockSpec((1, nc, w), lambda i: (i, 0, 0)),
        out_shape=jax.ShapeDtypeStruct((g, nc, w), F32),
        scratch_shapes=[pltpu.VMEM((nc, 128), F32), pltpu.VMEM((nc, 128), F32),
                        pltpu.VMEM((nc, 128), F32)],
        compiler_params=_cparams(("parallel",)),
        name="s5_core",
    )(z, tt, nn, mm, a)


def _s5_out_kernel(x_ref, y_ref, g_ref, d_ref, w_ref, b_ref, o_ref):
    x = x_ref[...]
    u = _rms(x, g_ref[...])
    y = y_ref[...] + d_ref[...] * u
    y = jax.nn.gelu(y)
    gate = jax.nn.sigmoid(_dot(y.astype(BF16), w_ref[...]) + b_ref[...])
    o_ref[...] = x + y * gate


def _s5_out_call(x, y, g, d, w, b, row):
    t, dm = x.shape
    tile = pl.BlockSpec((row, dm), lambda i: (i, 0))
    return pl.pallas_call(
        _s5_out_kernel,
        grid=(t // row,),
        in_specs=[tile, tile, _resident((1, dm)), _resident((1, dm)), _resident((dm, dm)),
                  _resident((1, dm))],
        out_specs=tile,
        out_shape=jax.ShapeDtypeStruct((t, dm), F32),
        compiler_params=_cparams(("parallel",)),
        name="s5_out",
    )(x, y, g.reshape(1, dm), d.reshape(1, dm), w, b.reshape(1, dm))


def _s5_tables(lam_re, lam_im, log_step, b_re, b_im, c_re, c_im):
    sub = S5_SUB
    step = jnp.exp(log_step)[:, None]
    k = jnp.arange(sub + 1, dtype=F32)[:, None, None]
    mag = jnp.exp(k * (lam_re * step)[None])
    ang = k * (lam_im * step)[None]
    pw_re, pw_im = mag * jnp.cos(ang), mag * jnp.sin(ang)
    lam = lax.complex(lam_re, lam_im)
    lam_bar = lax.complex(pw_re[1], pw_im[1])
    coef = (lam_bar - 1.0) / lam
    bb = coef[..., None] * lax.complex(b_re, b_im)
    bb_re, bb_im = jnp.real(bb), jnp.imag(bb)
    hi = lax.Precision.HIGHEST
    cl_re = c_re[None] * pw_re[:sub, :, None, :] - c_im[None] * pw_im[:sub, :, None, :]
    cl_im = c_re[None] * pw_im[:sub, :, None, :] + c_im[None] * pw_re[:sub, :, None, :]
    kk = (jnp.einsum('kgnp,gpm->kgnm', cl_re, bb_re, precision=hi)
          - jnp.einsum('kgnp,gpm->kgnm', cl_im, bb_im, precision=hi))
    s_idx = jnp.arange(sub)[:, None]
    t_idx = jnp.arange(sub)[None, :]
    lag = t_idx - s_idx
    kk_lag = kk[jnp.clip(lag, 0, sub - 1)]
    kk_lag = jnp.where((lag >= 0)[:, :, None, None, None], kk_lag, 0.0)
    tt = kk_lag.transpose(2, 0, 4, 1, 3).reshape(-1, sub * S5_GROUP, sub * S5_GROUP)
    rev_re, rev_im = pw_re[:sub][::-1], pw_im[:sub][::-1]
    nn_re = rev_re[..., None] * bb_re[None] - rev_im[..., None] * bb_im[None]
    nn_im = rev_re[..., None] * bb_im[None] + rev_im[..., None] * bb_re[None]
    nn = jnp.concatenate([nn_re, nn_im], axis=2)
    nn = nn.transpose(1, 0, 3, 2).reshape(-1, sub * S5_GROUP, 2 * S5_STATE)
    c1_re = c_re[None] * pw_re[1:, :, None, :] - c_im[None] * pw_im[1:, :, None, :]
    c1_im = c_re[None] * pw_im[1:, :, None, :] + c_im[None] * pw_re[1:, :, None, :]
    mm = jnp.concatenate([c1_re, -c1_im], axis=3)
    mm = mm.transpose(1, 3, 0, 2).reshape(-1, 2 * S5_STATE, sub * S5_GROUP)
    ar, ai = pw_re[sub], pw_im[sub]
    a = jnp.stack([jnp.concatenate([ar, ar], -1), jnp.concatenate([-ai, ai], -1),
                   jnp.concatenate([ai, -ai], -1), jnp.zeros_like(jnp.concatenate([ar, ar], -1))],
                  axis=1)
    return tt.astype(BF16), nn.astype(BF16), mm.astype(BF16), a


def _s5_layer(x, batch, norm_w, lam_re, lam_im, log_step, b_re, b_im, c_re, c_im, d, glu_w, glu_b,
              row):
    t, dm = x.shape
    groups = dm // S5_GROUP
    nc = t // S5_SUB
    tt, nn, mm, a = _s5_tables(lam_re, lam_im, log_step, b_re, b_im, c_re, c_im)
    u = _norm_call(x, norm_w, row)
    z = u.reshape(nc, S5_SUB, groups, S5_GROUP).transpose(2, 0, 1, 3).reshape(groups, nc, -1)
    yz = _s5_core_call(z, tt, nn, mm, a, batch)
    y = yz.reshape(groups, nc, S5_SUB, S5_GROUP).transpose(1, 2, 0, 3).reshape(t, dm)
    return _s5_out_call(x, y, norm_w, d.reshape(-1), glu_w.astype(BF16), glu_b, row)


def _ffn_kernel(x_ref, g_ref, wg_ref, wu_ref, wd_ref, f_ref, o_ref, acc_ref, *, hc, final):
    x = x_ref[...]
    u = _rms(x, g_ref[...]).astype(BF16)
    hidden = wg_ref.shape[1]
    for j in range(hidden // hc):
        cols = slice(j * hc, (j + 1) * hc)
        gt = _dot(u, wg_ref[:, cols])
        up = _dot(u, wu_ref[:, cols])
        h = (jax.nn.silu(gt) * up).astype(BF16)
        part = _dot(h, wd_ref[cols, :])
        if j == 0:
            acc_ref[...] = part
        else:
            acc_ref[...] += part
    y = x + acc_ref[...]
    if final:
        y = _rms(y, f_ref[...])
    o_ref[...] = y


def _ffn_call(x, g, wg, wu, wd, fw, row, final):
    t, dm = x.shape
    hidden = wg.shape[1]
    hc = 256
    assert hidden % hc == 0
    tile = pl.BlockSpec((row, dm), lambda i: (i, 0))
    kern = functools.partial(_ffn_kernel, hc=hc, final=final)
    return pl.pallas_call(
        kern,
        grid=(t // row,),
        in_specs=[tile, _resident((1, dm)), _resident((dm, hidden)), _resident((dm, hidden)),
                  _resident((hidden, dm)), _resident((1, dm))],
        out_specs=tile,
        out_shape=jax.ShapeDtypeStruct((t, dm), F32),
        scratch_shapes=[pltpu.VMEM((row, dm), F32)],
        compiler_params=_cparams(("parallel",)),
        name="ffn",
    )(x, g.reshape(1, dm), wg.astype(BF16), wu.astype(BF16), wd.astype(BF16), fw.reshape(1, dm))


def _ret_proj_kernel(x_ref, g_ref, w_ref, cos_ref, sin_ref, q_ref, k_ref, v_ref, gt_ref,
                     *, heads):
    u = _rms(x_ref[...], g_ref[...]).astype(BF16)
    cos = cos_ref[...]
    sin = sin_ref[...]
    half = RET_QK_DIM // 2
    qk_w = heads * RET_QK_DIM
    v_w = heads * RET_V_DIM
    k_scale = RET_QK_DIM ** -0.5

    def rotary(tq):
        t1, t2 = tq[:, :half], tq[:, half:]
        return jnp.concatenate([t1 * cos - t2 * sin, t1 * sin + t2 * cos], axis=-1)

    for h in range(heads):
        c0 = h * RET_QK_DIM
        qh = _dot(u, w_ref[:, c0:c0 + RET_QK_DIM])
        q_ref[:, c0:c0 + RET_QK_DIM] = rotary(qh).astype(q_ref.dtype)
        kh = _dot(u, w_ref[:, qk_w + c0:qk_w + c0 + RET_QK_DIM]) * k_scale
        k_ref[:, c0:c0 + RET_QK_DIM] = rotary(kh).astype(k_ref.dtype)
    for h in range(heads):
        c0 = h * RET_V_DIM
        v_ref[:, c0:c0 + RET_V_DIM] = _dot(
            u, w_ref[:, 2 * qk_w + c0:2 * qk_w + c0 + RET_V_DIM]).astype(v_ref.dtype)
        gt_ref[:, c0:c0 + RET_V_DIM] = _dot(
            u, w_ref[:, 2 * qk_w + v_w + c0:2 * qk_w + v_w + c0 + RET_V_DIM]).astype(gt_ref.dtype)


def _ret_proj_call(x, g, w, cos, sin, heads, seq, row):
    t, dm = x.shape
    qk_w = heads * RET_QK_DIM
    v_w = heads * RET_V_DIM
    nb = seq // row
    kern = functools.partial(_ret_proj_kernel, heads=heads)
    rowspec = lambda width: pl.BlockSpec((row, width), lambda i: (i, 0))
    tab = pl.BlockSpec((row, RET_QK_DIM // 2), lambda i: (i % nb, 0))
    return pl.pallas_call(
        kern,
        grid=(t // row,),
        in_specs=[rowspec(dm), _resident((1, dm)), _resident(w.shape), tab, tab],
        out_specs=[rowspec(qk_w), rowspec(qk_w), rowspec(v_w), rowspec(v_w)],
        out_shape=[jax.ShapeDtypeStruct((t, qk_w), BF16), jax.ShapeDtypeStruct((t, qk_w), BF16),
                   jax.ShapeDtypeStruct((t, v_w), BF16), jax.ShapeDtypeStruct((t, v_w), F32)],
        compiler_params=_cparams(("parallel",)),
        name="ret_proj",
    )(x, g.reshape(1, dm), w, cos, sin)


def _ret_core_kernel(q_ref, k_ref, v_ref, gt_ref, gn_ref, dm_ref, qd_ref, kd_ref, cd_ref, y_ref,
                     state_ref):
    @pl.when(pl.program_id(2) == 0)
    def _():
        state_ref[...] = jnp.zeros_like(state_ref)

    q = q_ref[...]
    k = k_ref[...]
    v = v_ref[...]
    s = lax.dot_general(q, k, (((1,), (1,)), ((), ())), preferred_element_type=F32)
    s = s * dm_ref[0]
    o = _dot(s.astype(BF16), v)
    st = state_ref[...]
    o = o + _dot(q, st.astype(BF16)) * qd_ref[0]
    kd = (k.astype(F32) * kd_ref[0]).astype(BF16)
    state_ref[...] = st * cd_ref[0] + lax.dot_general(
        kd, v, (((0,), (0,)), ((), ())), preferred_element_type=F32)
    mean = jnp.mean(o, axis=-1, keepdims=True)
    cen = o - mean
    var = jnp.mean(jnp.square(cen), axis=-1, keepdims=True)
    on = cen * lax.rsqrt(var + NORM_EPS) * gn_ref[...]
    y_ref[...] = (jax.nn.silu(gt_ref[...]) * on).astype(y_ref.dtype)


def _ret_core_call(q, k, v, gt, gn_w, dmask, qdec, kdec, cdec, batch, seq, heads, blk):
    t = q.shape[0]
    nb = seq // blk
    rows = lambda width: pl.BlockSpec((blk, width), lambda b, h, i: (b * nb + i, h))
    per_head = lambda shape: pl.BlockSpec((1,) + shape, lambda b, h, i: (h, 0, 0))
    return pl.pallas_call(
        _ret_core_kernel,
        grid=(batch, heads, nb),
        in_specs=[rows(RET_QK_DIM), rows(RET_QK_DIM), rows(RET_V_DIM), rows(RET_V_DIM),
                  pl.BlockSpec((1, RET_V_DIM), lambda b, h, i: (0, h)),
                  per_head((blk, blk)), per_head((blk, 1)), per_head((blk, 1)), per_head((1, 1))],
        out_specs=rows(RET_V_DIM),
        out_shape=jax.ShapeDtypeStruct((t, heads * RET_V_DIM), BF16),
        scratch_shapes=[pltpu.VMEM((RET_QK_DIM, RET_V_DIM), F32)],
        compiler_params=_cparams(("parallel", "parallel", "arbitrary")),
        name="ret_core",
    )(q, k, v, gt, gn_w.reshape(1, -1), dmask, qdec, kdec, cdec)


def _ret_decays(heads, blk):
    log_gamma = jnp.log1p(-jnp.exp2(-5.0 - jnp.arange(heads, dtype=F32)))
    pos = jnp.arange(blk, dtype=F32)
    diff = pos[:, None] - pos[None, :]
    cn = (jnp.arange(blk) // CHUNK)[:, None]
    cm = (jnp.arange(blk) // CHUNK)[None, :]
    expo = jnp.where(cn == cm, jnp.abs(diff), diff)
    dmask = jnp.where((cm <= cn)[None], jnp.exp(log_gamma[:, None, None] * expo[None]), 0.0)
    qdec = jnp.exp((pos[None, :] + 1.0) * log_gamma[:, None])[..., None]
    kdec = jnp.exp((blk - 1.0 - pos)[None, :] * log_gamma[:, None])[..., None]
    cdec = jnp.exp(blk * log_gamma)[:, None, None]
    return dmask, qdec, kdec, cdec


def _rotary_tables(seq):
    inv_freq = 1.0 / (ROPE_BASE ** jnp.linspace(0.0, 1.0, RET_QK_DIM // 2, dtype=F32))
    ang = jnp.arange(seq, dtype=F32)[:, None] * inv_freq[None, :]
    return jnp.cos(ang), jnp.sin(ang)


def _out_proj_kernel(y_ref, w_ref, x_ref, o_ref):
    o_ref[...] = x_ref[...] + _dot(y_ref[...], w_ref[...])


def _out_proj_call(y, w, x, row):
    t, dm = x.shape
    kdim = y.shape[1]
    return pl.pallas_call(
        _out_proj_kernel,
        grid=(t // row,),
        in_specs=[pl.BlockSpec((row, kdim), lambda i: (i, 0)), _resident((kdim, dm)),
                  pl.BlockSpec((row, dm), lambda i: (i, 0))],
        out_specs=pl.BlockSpec((row, dm), lambda i: (i, 0)),
        out_shape=jax.ShapeDtypeStruct((t, dm), F32),
        compiler_params=_cparams(("parallel",)),
        name="ret_out",
    )(y, w, x)


def _ret_layer(x, batch, seq, norm_w, w_qkvg, gn_w, w_o, row, blk):
    dm = x.shape[1]
    heads = dm // RET_QK_DIM
    cos, sin = _rotary_tables(seq)
    q, k, v, gt = _ret_proj_call(x, norm_w, w_qkvg.astype(BF16), cos, sin, heads, seq, row)
    dmask, qdec, kdec, cdec = _ret_decays(heads, blk)
    y = _ret_core_call(q, k, v, gt, gn_w, dmask, qdec, kdec, cdec, batch, seq, heads, blk)
    return _out_proj_call(y, w_o.astype(BF16), x, row)


def kernel(x, s5_norm, s5_lambda_re, s5_lambda_im, s5_log_step, s5_b_re, s5_b_im, s5_c_re, s5_c_im,
           s5_d, s5_glu_w, s5_glu_b, ret_norm, ret_w_qkvg, ret_gn_w, ret_w_o, ffn_norm, ffn_w_gate,
           ffn_w_up, ffn_w_down, final_norm):
    batch, seq, dm = x.shape
    row, blk = _tiles(seq)
    depth = ffn_norm.shape[0]
    h = x.reshape(batch * seq, dm)
    for i in range(depth):
        j = i // 2
        if i % 2 == 0:
            h = _s5_layer(h, batch, s5_norm[j], s5_lambda_re[j], s5_lambda_im[j], s5_log_step[j],
                          s5_b_re[j], s5_b_im[j], s5_c_re[j], s5_c_im[j], s5_d[j], s5_glu_w[j],
                          s5_glu_b[j], row)
        else:
            h = _ret_layer(h, batch, seq, ret_norm[j], ret_w_qkvg[j], ret_gn_w[j], ret_w_o[j],
                           row, blk)
        h = _ffn_call(h, ffn_norm[i], ffn_w_gate[i], ffn_w_up[i], ffn_w_down[i], final_norm,
                      row, final=(i == depth - 1))
    return h.reshape(batch, seq, dm)
```

```python
import functools

import jax
import jax.numpy as jnp
from jax import lax
from jax.experimental import pallas as pl
from jax.experimental.pallas import tpu as pltpu

F32 = jnp.float32
BF16 = jnp.bfloat16

NORM_EPS = 1e-6
CHUNK = 64
S5_GROUP = 16
S5_STATE = 64
S5_SUB = 16
S5_GB = 16
RET_QK_DIM = 256
RET_V_DIM = 512
ROPE_BASE = 10000.0

V7X_VMEM_BYTES = 64 * 1024 * 1024
V7X_LANES = 128
VMEM_LIMIT = 56 * 1024 * 1024


def _tiles(seq):
    row = min(512, seq)
    ret = min(256, seq)
    assert seq % row == 0 and seq % ret == 0 and ret % CHUNK == 0
    return row, ret


def _cparams(sem):
    return pltpu.CompilerParams(dimension_semantics=sem, vmem_limit_bytes=VMEM_LIMIT)


def _resident(shape):
    nd = len(shape)
    return pl.BlockSpec(shape, lambda *_: (0,) * nd, pipeline_mode=pl.Buffered(1))


def _rms(xf, g):
    ms = jnp.mean(jnp.square(xf), axis=-1, keepdims=True)
    return xf * lax.rsqrt(ms + NORM_EPS) * g


def _dot(a, b):
    return jnp.dot(a, b, preferred_element_type=F32)


def _bdot(a, b):
    return lax.dot_general(a, b, (((2,), (1,)), ((0,), (0,))), preferred_element_type=F32)


def _s5_pre_kernel(x_ref, g_ref, nn_ref, are_ref, aim_ref, zt_ref, er_ref, ei_ref):
    groups = zt_ref.shape[1]
    nseg = zt_ref.shape[3]
    g = g_ref[...]
    for t in range(S5_SUB):
        ut = _rms(x_ref[:, t, :], g)
        zt_ref[0, :, t * S5_GROUP:(t + 1) * S5_GROUP, :] = (
            ut.T.reshape(groups, S5_GROUP, nseg).astype(zt_ref.dtype))
    s = _bdot(nn_ref[...], zt_ref[0])
    sr = s[:, :S5_STATE, :].reshape(groups * S5_STATE, nseg)
    si = s[:, S5_STATE:, :].reshape(groups * S5_STATE, nseg)

    @pl.when(pl.program_id(0) == 0)
    def _():
        er_ref[...] = sr
        ei_ref[...] = si

    @pl.when(pl.program_id(0) > 0)
    def _():
        er, ei = er_ref[...], ei_ref[...]
        ar, ai = are_ref[...], aim_ref[...]
        er_ref[...] = ar * er - ai * ei + sr
        ei_ref[...] = ar * ei + ai * er + si


def _s5_pre_call(x3, g, nn, a_re, a_im):
    nseg, sl, dm = x3.shape
    groups = dm // S5_GROUP
    steps = sl // S5_SUB
    width = S5_SUB * S5_GROUP
    rows = groups * S5_STATE
    return pl.pallas_call(
        _s5_pre_kernel,
        grid=(steps,),
        in_specs=[pl.BlockSpec((nseg, S5_SUB, dm), lambda i: (0, i, 0)), _resident((1, dm)),
                  _resident(nn.shape), _resident((rows, nseg)), _resident((rows, nseg))],
        out_specs=[pl.BlockSpec((1, groups, width, nseg), lambda i: (i, 0, 0, 0)),
                   pl.BlockSpec((rows, nseg), lambda i: (0, 0)),
                   pl.BlockSpec((rows, nseg), lambda i: (0, 0))],
        out_shape=[jax.ShapeDtypeStruct((steps, groups, width, nseg), BF16),
                   jax.ShapeDtypeStruct((rows, nseg), F32),
                   jax.ShapeDtypeStruct((rows, nseg), F32)],
        compiler_params=_cparams(("arbitrary",)),
        name="s5_pre",
    )(x3, g.reshape(1, dm), nn, a_re, a_im)


def _s5_stitch_kernel(er_ref, ei_ref, pr_ref, pi_ref, xr_ref, xi_ref, *, nsb):
    er, ei = er_ref[...], ei_ref[...]
    lane = lax.broadcasted_iota(jnp.int32, er.shape, 1) % nsb
    d, k = 1, 0
    while d < nsb:
        sr, si = pltpu.roll(er, d, axis=1), pltpu.roll(ei, d, axis=1)
        pr, pi = pr_ref[:, k:k + 1], pi_ref[:, k:k + 1]
        ok = lane >= d
        er, ei = (er + jnp.where(ok, pr * sr - pi * si, 0.0),
                  ei + jnp.where(ok, pr * si + pi * sr, 0.0))
        d, k = 2 * d, k + 1
    ok = lane >= 1
    xr_ref[...] = jnp.where(ok, pltpu.roll(er, 1, axis=1), 0.0)
    xi_ref[...] = jnp.where(ok, pltpu.roll(ei, 1, axis=1), 0.0)


def _s5_stitch_call(er, ei, p_re, p_im, nsb):
    shp = jax.ShapeDtypeStruct(er.shape, F32)
    return pl.pallas_call(
        functools.partial(_s5_stitch_kernel, nsb=nsb),
        out_shape=[shp, shp],
        compiler_params=pltpu.CompilerParams(vmem_limit_bytes=VMEM_LIMIT),
        name="s5_stitch",
    )(er, ei, p_re, p_im)


def _s5_core_kernel(zt_ref, tt_ref, nn_ref, mm_ref, are_ref, aim_ref, x0r_ref, x0i_ref, y_ref,
                    xr_scr, xi_scr):
    gb = zt_ref.shape[1]
    nseg = zt_ref.shape[3]

    @pl.when(pl.program_id(1) == 0)
    def _():
        xr_scr[...] = x0r_ref[...]
        xi_scr[...] = x0i_ref[...]

    zt = zt_ref[0]
    xr, xi = xr_scr[...], xi_scr[...]
    xprev = jnp.concatenate([xr.reshape(gb, S5_STATE, nseg), xi.reshape(gb, S5_STATE, nseg)],
                            axis=1).astype(zt.dtype)
    y = _bdot(tt_ref[...], zt) + _bdot(mm_ref[...], xprev)
    s = _bdot(nn_ref[...], zt)
    ar, ai = are_ref[...], aim_ref[...]
    xr_scr[...] = ar * xr - ai * xi + s[:, :S5_STATE, :].reshape(gb * S5_STATE, nseg)
    xi_scr[...] = ar * xi + ai * xr + s[:, S5_STATE:, :].reshape(gb * S5_STATE, nseg)
    for t in range(S5_SUB):
        blk = y[:, t * S5_GROUP:(t + 1) * S5_GROUP, :].reshape(gb * S5_GROUP, nseg)
        y_ref[:, t, :] = blk.T


def _s5_core_call(zt, tt, nn, mm, a_re, a_im, x0r, x0i, dm):
    steps, groups, width, nseg = zt.shape
    gb = min(S5_GB, groups)
    rows = gb * S5_STATE
    per_gb = lambda shape: pl.BlockSpec(shape, lambda j, i: (j,) + (0,) * (len(shape) - 1))
    return pl.pallas_call(
        _s5_core_kernel,
        grid=(groups // gb, steps),
        in_specs=[pl.BlockSpec((1, gb, width, nseg), lambda j, i: (i, j, 0, 0)),
                  per_gb((gb, width, width)), per_gb((gb, 2 * S5_STATE, width)),
                  per_gb((gb, width, 2 * S5_STATE)),
                  per_gb((rows, nseg)), per_gb((rows, nseg)), per_gb((rows, nseg)),
                  per_gb((rows, nseg))],
        out_specs=pl.BlockSpec((nseg, S5_SUB, gb * S5_GROUP), lambda j, i: (0, i, j)),
        out_shape=jax.ShapeDtypeStruct((nseg, steps * S5_SUB, dm), F32),
        scratch_shapes=[pltpu.VMEM((rows, nseg), F32), pltpu.VMEM((rows, nseg), F32)],
        compiler_params=_cparams(("parallel", "arbitrary")),
        name="s5_core",
    )(zt, tt, nn, mm, a_re, a_im, x0r, x0i)


def _s5_out_kernel(x_ref, y_ref, g_ref, d_ref, w_ref, b_ref, o_ref):
    x = x_ref[...]
    u = _rms(x, g_ref[...])
    y = y_ref[...] + d_ref[...] * u
    y = jax.nn.gelu(y)
    gate = jax.nn.sigmoid(_dot(y.astype(BF16), w_ref[...]) + b_ref[...])
    o_ref[...] = x + y * gate


def _s5_out_call(x, y, g, d, w, b, row):
    t, dm = x.shape
    tile = pl.BlockSpec((row, dm), lambda i: (i, 0))
    return pl.pallas_call(
        _s5_out_kernel,
        grid=(t // row,),
        in_specs=[tile, tile, _resident((1, dm)), _resident((1, dm)), _resident((dm, dm)),
                  _resident((1, dm))],
        out_specs=tile,
        out_shape=jax.ShapeDtypeStruct((t, dm), F32),
        compiler_params=_cparams(("parallel",)),
        name="s5_out",
    )(x, y, g.reshape(1, dm), d.reshape(1, dm), w, b.reshape(1, dm))


def _s5_tables(lam_re, lam_im, log_step, b_re, b_im, c_re, c_im, seg_len, nsb, nseg):
    sub = S5_SUB
    step = jnp.exp(log_step)[:, None]
    dre, dim = lam_re * step, lam_im * step

    def power(k):
        kk = jnp.asarray(k, F32)[:, None, None]
        mag = jnp.exp(kk * dre[None])
        return mag * jnp.cos(kk * dim[None]), mag * jnp.sin(kk * dim[None])

    pw_re, pw_im = power(jnp.arange(sub + 1))
    den = lam_re * lam_re + lam_im * lam_im
    nr, ni = pw_re[1] - 1.0, pw_im[1]
    cf_re = (nr * lam_re + ni * lam_im) / den
    cf_im = (ni * lam_re - nr * lam_im) / den
    bb_re = cf_re[..., None] * b_re - cf_im[..., None] * b_im
    bb_im = cf_re[..., None] * b_im + cf_im[..., None] * b_re
    hi = lax.Precision.HIGHEST
    cl_re = c_re[None] * pw_re[:sub, :, None, :] - c_im[None] * pw_im[:sub, :, None, :]
    cl_im = c_re[None] * pw_im[:sub, :, None, :] + c_im[None] * pw_re[:sub, :, None, :]
    kk = (jnp.einsum('kgnp,gpm->kgnm', cl_re, bb_re, precision=hi)
          - jnp.einsum('kgnp,gpm->kgnm', cl_im, bb_im, precision=hi))
    t_idx = jnp.arange(sub)[:, None]
    s_idx = jnp.arange(sub)[None, :]
    lag = t_idx - s_idx
    kk_lag = kk[jnp.clip(lag, 0, sub - 1)]
    kk_lag = jnp.where((lag >= 0)[:, :, None, None, None], kk_lag, 0.0)
    tt = kk_lag.transpose(2, 0, 3, 1, 4).reshape(-1, sub * S5_GROUP, sub * S5_GROUP)
    rev_re, rev_im = pw_re[:sub][::-1], pw_im[:sub][::-1]
    nn_re = rev_re[..., None] * bb_re[None] - rev_im[..., None] * bb_im[None]
    nn_im = rev_re[..., None] * bb_im[None] + rev_im[..., None] * bb_re[None]
    nn = jnp.concatenate([nn_re, nn_im], axis=2)
    nn = nn.transpose(1, 2, 0, 3).reshape(-1, 2 * S5_STATE, sub * S5_GROUP)
    c1_re = c_re[None] * pw_re[1:, :, None, :] - c_im[None] * pw_im[1:, :, None, :]
    c1_im = c_re[None] * pw_im[1:, :, None, :] + c_im[None] * pw_re[1:, :, None, :]
    mm = jnp.concatenate([c1_re, -c1_im], axis=3)
    mm = mm.transpose(1, 0, 2, 3).reshape(-1, sub * S5_GROUP, 2 * S5_STATE)
    a_re = jnp.broadcast_to(pw_re[sub].reshape(-1, 1), (pw_re[sub].size, nseg))
    a_im = jnp.broadcast_to(pw_im[sub].reshape(-1, 1), (pw_im[sub].size, nseg))
    nd = max(1, (nsb - 1).bit_length())
    p_re, p_im = power(seg_len * (2 ** jnp.arange(nd)))
    p_re = p_re.reshape(nd, -1).T
    p_im = p_im.reshape(nd, -1).T
    return tt.astype(BF16), nn.astype(BF16), mm.astype(BF16), a_re, a_im, p_re, p_im


def _s5_layer(x, batch, seq, norm_w, lam_re, lam_im, log_step, b_re, b_im, c_re, c_im, d, glu_w,
              glu_b, row):
    t, dm = x.shape
    nseg = V7X_LANES
    seg_len = t // nseg
    assert t % nseg == 0 and seg_len % S5_SUB == 0 and seq % seg_len == 0
    nsb = seq // seg_len
    tt, nn, mm, a_re, a_im, p_re, p_im = _s5_tables(
        lam_re, lam_im, log_step, b_re, b_im, c_re, c_im, seg_len, nsb, nseg)
    zt, er, ei = _s5_pre_call(x.reshape(nseg, seg_len, dm), norm_w, nn, a_re, a_im)
    x0r, x0i = _s5_stitch_call(er, ei, p_re, p_im, nsb)
    y = _s5_core_call(zt, tt, nn, mm, a_re, a_im, x0r, x0i, dm).reshape(t, dm)
    return _s5_out_call(x, y, norm_w, d.reshape(-1), glu_w.astype(BF16), glu_b, row)


def _ffn_kernel(x_ref, g_ref, wg_ref, wu_ref, wd_ref, f_ref, o_ref, acc_ref, *, hc, final):
    x = x_ref[...]
    u = _rms(x, g_ref[...]).astype(BF16)
    hidden = wg_ref.shape[1]
    for j in range(hidden // hc):
        cols = slice(j * hc, (j + 1) * hc)
        gt = _dot(u, wg_ref[:, cols])
        up = _dot(u, wu_ref[:, cols])
        h = (jax.nn.silu(gt) * up).astype(BF16)
        part = _dot(h, wd_ref[cols, :])
        if j == 0:
            acc_ref[...] = part
        else:
            acc_ref[...] += part
    y = x + acc_ref[...]
    if final:
        y = _rms(y, f_ref[...])
    o_ref[...] = y


def _ffn_call(x, g, wg, wu, wd, fw, row, final):
    t, dm = x.shape
    hidden = wg.shape[1]
    hc = 256
    assert hidden % hc == 0
    tile = pl.BlockSpec((row, dm), lambda i: (i, 0))
    kern = functools.partial(_ffn_kernel, hc=hc, final=final)
    return pl.pallas_call(
        kern,
        grid=(t // row,),
        in_specs=[tile, _resident((1, dm)), _resident((dm, hidden)), _resident((dm, hidden)),
                  _resident((hidden, dm)), _resident((1, dm))],
        out_specs=tile,
        out_shape=jax.ShapeDtypeStruct((t, dm), F32),
        scratch_shapes=[pltpu.VMEM((row, dm), F32)],
        compiler_params=_cparams(("parallel",)),
        name="ffn",
    )(x, g.reshape(1, dm), wg.astype(BF16), wu.astype(BF16), wd.astype(BF16), fw.reshape(1, dm))


def _ret_proj_kernel(x_ref, g_ref, w_ref, cos_ref, sin_ref, q_ref, k_ref, v_ref, gt_ref,
                     *, heads):
    u = _rms(x_ref[...], g_ref[...]).astype(BF16)
    cos = cos_ref[...]
    sin = sin_ref[...]
    half = RET_QK_DIM // 2
    qk_w = heads * RET_QK_DIM
    v_w = heads * RET_V_DIM
    k_scale = RET_QK_DIM ** -0.5

    def rotary(tq):
        t1, t2 = tq[:, :half], tq[:, half:]
        return jnp.concatenate([t1 * cos - t2 * sin, t1 * sin + t2 * cos], axis=-1)

    for h in range(heads):
        c0 = h * RET_QK_DIM
        qh = _dot(u, w_ref[:, c0:c0 + RET_QK_DIM])
        q_ref[:, c0:c0 + RET_QK_DIM] = rotary(qh).astype(q_ref.dtype)
        kh = _dot(u, w_ref[:, qk_w + c0:qk_w + c0 + RET_QK_DIM]) * k_scale
        k_ref[:, c0:c0 + RET_QK_DIM] = rotary(kh).astype(k_ref.dtype)
    for h in range(heads):
        c0 = h * RET_V_DIM
        v_ref[:, c0:c0 + RET_V_DIM] = _dot(
            u, w_ref[:, 2 * qk_w + c0:2 * qk_w + c0 + RET_V_DIM]).astype(v_ref.dtype)
        gt_ref[:, c0:c0 + RET_V_DIM] = _dot(
            u, w_ref[:, 2 * qk_w + v_w + c0:2 * qk_w + v_w + c0 + RET_V_DIM]).astype(gt_ref.dtype)


def _ret_proj_call(x, g, w, cos, sin, heads, seq, row):
    t, dm = x.shape
    qk_w = heads * RET_QK_DIM
    v_w = heads * RET_V_DIM
    nb = seq // row
    kern = functools.partial(_ret_proj_kernel, heads=heads)
    rowspec = lambda width: pl.BlockSpec((row, width), lambda i: (i, 0))
    tab = pl.BlockSpec((row, RET_QK_DIM // 2), lambda i: (i % nb, 0))
    return pl.pallas_call(
        kern,
        grid=(t // row,),
        in_specs=[rowspec(dm), _resident((1, dm)), _resident(w.shape), tab, tab],
        out_specs=[rowspec(qk_w), rowspec(qk_w), rowspec(v_w), rowspec(v_w)],
        out_shape=[jax.ShapeDtypeStruct((t, qk_w), BF16), jax.ShapeDtypeStruct((t, qk_w), BF16),
                   jax.ShapeDtypeStruct((t, v_w), BF16), jax.ShapeDtypeStruct((t, v_w), F32)],
        compiler_params=_cparams(("parallel",)),
        name="ret_proj",
    )(x, g.reshape(1, dm), w, cos, sin)


def _ret_core_kernel(q_ref, k_ref, v_ref, gt_ref, gn_ref, dm_ref, qd_ref, kd_ref, cd_ref, y_ref,
                     state_ref):
    @pl.when(pl.program_id(2) == 0)
    def _():
        state_ref[...] = jnp.zeros_like(state_ref)

    q = q_ref[...]
    k = k_ref[...]
    v = v_ref[...]
    s = lax.dot_general(q, k, (((1,), (1,)), ((), ())), preferred_element_type=F32)
    s = s * dm_ref[0]
    o = _dot(s.astype(BF16), v)
    st = state_ref[...]
    o = o + _dot(q, st.astype(BF16)) * qd_ref[0]
    kd = (k.astype(F32) * kd_ref[0]).astype(BF16)
    state_ref[...] = st * cd_ref[0] + lax.dot_general(
        kd, v, (((0,), (0,)), ((), ())), preferred_element_type=F32)
    mean = jnp.mean(o, axis=-1, keepdims=True)
    cen = o - mean
    var = jnp.mean(jnp.square(cen), axis=-1, keepdims=True)
    on = cen * lax.rsqrt(var + NORM_EPS) * gn_ref[...]
    y_ref[...] = (jax.nn.silu(gt_ref[...]) * on).astype(y_ref.dtype)


def _ret_core_call(q, k, v, gt, gn_w, dmask, qdec, kdec, cdec, batch, seq, heads, blk):
    t = q.shape[0]
    nb = seq // blk
    rows = lambda width: pl.BlockSpec((blk, width), lambda b, h, i: (b * nb + i, h))
    per_head = lambda shape: pl.BlockSpec((1,) + shape, lambda b, h, i: (h, 0, 0))
    return pl.pallas_call(
        _ret_core_kernel,
        grid=(batch, heads, nb),
        in_specs=[rows(RET_QK_DIM), rows(RET_QK_DIM), rows(RET_V_DIM), rows(RET_V_DIM),
                  pl.BlockSpec((1, RET_V_DIM), lambda b, h, i: (0, h)),
                  per_head((blk, blk)), per_head((blk, 1)), per_head((blk, 1)), per_head((1, 1))],
        out_specs=rows(RET_V_DIM),
        out_shape=jax.ShapeDtypeStruct((t, heads * RET_V_DIM), BF16),
        scratch_shapes=[pltpu.VMEM((RET_QK_DIM, RET_V_DIM), F32)],
        compiler_params=_cparams(("parallel", "parallel", "arbitrary")),
        name="ret_core",
    )(q, k, v, gt, gn_w.reshape(1, -1), dmask, qdec, kdec, cdec)


def _ret_decays(heads, blk):
    log_gamma = jnp.log1p(-jnp.exp2(-5.0 - jnp.arange(heads, dtype=F32)))
    pos = jnp.arange(blk, dtype=F32)
    diff = pos[:, None] - pos[None, :]
    cn = (jnp.arange(blk) // CHUNK)[:, None]
    cm = (jnp.arange(blk) // CHUNK)[None, :]
    expo = jnp.where(cn == cm, jnp.abs(diff), diff)
    dmask = jnp.where((cm <= cn)[None], jnp.exp(log_gamma[:, None, None] * expo[None]), 0.0)
    qdec = jnp.exp((pos[None, :] + 1.0) * log_gamma[:, None])[..., None]
    kdec = jnp.exp((blk - 1.0 - pos)[None, :] * log_gamma[:, None])[..., None]
    cdec = jnp.exp(blk * log_gamma)[:, None, None]
    return dmask, qdec, kdec, cdec


def _rotary_tables(seq):
    inv_freq = 1.0 / (ROPE_BASE ** jnp.linspace(0.0, 1.0, RET_QK_DIM // 2, dtype=F32))
    ang = jnp.arange(seq, dtype=F32)[:, None] * inv_freq[None, :]
    return jnp.cos(ang), jnp.sin(ang)


def _out_proj_kernel(y_ref, w_ref, x_ref, o_ref):
    o_ref[...] = x_ref[...] + _dot(y_ref[...], w_ref[...])


def _out_proj_call(y, w, x, row):
    t, dm = x.shape
    kdim = y.shape[1]
    return pl.pallas_call(
        _out_proj_kernel,
        grid=(t // row,),
        in_specs=[pl.BlockSpec((row, kdim), lambda i: (i, 0)), _resident((kdim, dm)),
                  pl.BlockSpec((row, dm), lambda i: (i, 0))],
        out_specs=pl.BlockSpec((row, dm), lambda i: (i, 0)),
        out_shape=jax.ShapeDtypeStruct((t, dm), F32),
        compiler_params=_cparams(("parallel",)),
        name="ret_out",
    )(y, w, x)


def _ret_layer(x, batch, seq, norm_w, w_qkvg, gn_w, w_o, row, blk):
    dm = x.shape[1]
    heads = dm // RET_QK_DIM
    cos, sin = _rotary_tables(seq)
    q, k, v, gt = _ret_proj_call(x, norm_w, w_qkvg.astype(BF16), cos, sin, heads, seq, row)
    dmask, qdec, kdec, cdec = _ret_decays(heads, blk)
    y = _ret_core_call(q, k, v, gt, gn_w, dmask, qdec, kdec, cdec, batch, seq, heads, blk)
    return _out_proj_call(y, w_o.astype(BF16), x, row)


def kernel(x, s5_norm, s5_lambda_re, s5_lambda_im, s5_log_step, s5_b_re, s5_b_im, s5_c_re, s5_c_im,
           s5_d, s5_glu_w, s5_glu_b, ret_norm, ret_w_qkvg, ret_gn_w, ret_w_o, ffn_norm, ffn_w_gate,
           ffn_w_up, ffn_w_down, final_norm):
    batch, seq, dm = x.shape
    row, blk = _tiles(seq)
    depth = ffn_norm.shape[0]
    h = x.reshape(batch * seq, dm)
    for i in range(depth):
        j = i // 2
        if i % 2 == 0:
            h = _s5_layer(h, batch, seq, s5_norm[j], s5_lambda_re[j], s5_lambda_im[j],
                          s5_log_step[j], s5_b_re[j], s5_b_im[j], s5_c_re[j], s5_c_im[j], s5_d[j],
                          s5_glu_w[j], s5_glu_b[j], row)
        else:
            h = _ret_layer(h, batch, seq, ret_norm[j], ret_w_qkvg[j], ret_gn_w[j], ret_w_o[j],
                           row, blk)
        h = _ffn_call(h, ffn_norm[i], ffn_w_gate[i], ffn_w_up[i], ffn_w_down[i], final_norm,
                      row, final=(i == depth - 1))
    return h.reshape(batch, seq, dm)
```

```python
import functools

import jax
import jax.numpy as jnp
from jax import lax
from jax.experimental import pallas as pl
from jax.experimental.pallas import tpu as pltpu

F32 = jnp.float32
BF16 = jnp.bfloat16

NORM_EPS = 1e-6
CHUNK = 64
S5_GROUP = 16
S5_STATE = 64
S5_SUB = 16
S5_GB = 16
RET_QK_DIM = 256
RET_V_DIM = 512
ROPE_BASE = 10000.0

V7X_VMEM_BYTES = 64 * 1024 * 1024
V7X_LANES = 128
VMEM_LIMIT = 56 * 1024 * 1024


def _tiles(seq):
    row = min(512, seq)
    ret = min(256, seq)
    assert seq % row == 0 and seq % ret == 0 and ret % CHUNK == 0
    return row, ret


def _cparams(sem):
    return pltpu.CompilerParams(dimension_semantics=sem, vmem_limit_bytes=VMEM_LIMIT)


def _resident(shape):
    nd = len(shape)
    return pl.BlockSpec(shape, lambda *_: (0,) * nd, pipeline_mode=pl.Buffered(1))


def _rms(xf, g):
    ms = jnp.mean(jnp.square(xf), axis=-1, keepdims=True)
    return xf * lax.rsqrt(ms + NORM_EPS) * g


def _dot(a, b):
    return jnp.dot(a, b, preferred_element_type=F32)


def _bdot(a, b):
    return lax.dot_general(a, b, (((2,), (1,)), ((0,), (0,))), preferred_element_type=F32)


def _s5_pre_kernel(x_ref, g_ref, nn_ref, are_ref, aim_ref, zt_ref, er_ref, ei_ref):
    groups = zt_ref.shape[1]
    nseg = zt_ref.shape[3]
    g = g_ref[...]
    for t in range(S5_SUB):
        ut = _rms(x_ref[:, t, :], g)
        zt_ref[0, :, t * S5_GROUP:(t + 1) * S5_GROUP, :] = (
            ut.T.reshape(groups, S5_GROUP, nseg).astype(zt_ref.dtype))
    s = _bdot(nn_ref[...], zt_ref[0])
    sr = s[:, :S5_STATE, :].reshape(groups * S5_STATE, nseg)
    si = s[:, S5_STATE:, :].reshape(groups * S5_STATE, nseg)

    @pl.when(pl.program_id(0) == 0)
    def _():
        er_ref[...] = sr
        ei_ref[...] = si

    @pl.when(pl.program_id(0) > 0)
    def _():
        er, ei = er_ref[...], ei_ref[...]
        ar, ai = are_ref[...], aim_ref[...]
        er_ref[...] = ar * er - ai * ei + sr
        ei_ref[...] = ar * ei + ai * er + si


def _s5_pre_call(x3, g, nn, a_re, a_im):
    nseg, sl, dm = x3.shape
    groups = dm // S5_GROUP
    steps = sl // S5_SUB
    width = S5_SUB * S5_GROUP
    rows = groups * S5_STATE
    return pl.pallas_call(
        _s5_pre_kernel,
        grid=(steps,),
        in_specs=[pl.BlockSpec((nseg, S5_SUB, dm), lambda i: (0, i, 0)), _resident((1, dm)),
                  _resident(nn.shape), _resident((rows, nseg)), _resident((rows, nseg))],
        out_specs=[pl.BlockSpec((1, groups, width, nseg), lambda i: (i, 0, 0, 0)),
                   pl.BlockSpec((rows, nseg), lambda i: (0, 0)),
                   pl.BlockSpec((rows, nseg), lambda i: (0, 0))],
        out_shape=[jax.ShapeDtypeStruct((steps, groups, width, nseg), BF16),
                   jax.ShapeDtypeStruct((rows, nseg), F32),
                   jax.ShapeDtypeStruct((rows, nseg), F32)],
        compiler_params=_cparams(("arbitrary",)),
        name="s5_pre",
    )(x3, g.reshape(1, dm), nn, a_re, a_im)


def _s5_stitch_kernel(er_ref, ei_ref, pr_ref, pi_ref, xr_ref, xi_ref, *, nsb):
    er, ei = er_ref[...], ei_ref[...]
    lane = lax.broadcasted_iota(jnp.int32, er.shape, 1) % nsb
    d, k = 1, 0
    while d < nsb:
        sr, si = pltpu.roll(er, d, axis=1), pltpu.roll(ei, d, axis=1)
        pr, pi = pr_ref[:, k:k + 1], pi_ref[:, k:k + 1]
        ok = lane >= d
        er, ei = (er + jnp.where(ok, pr * sr - pi * si, 0.0),
                  ei + jnp.where(ok, pr * si + pi * sr, 0.0))
        d, k = 2 * d, k + 1
    ok = lane >= 1
    xr_ref[...] = jnp.where(ok, pltpu.roll(er, 1, axis=1), 0.0)
    xi_ref[...] = jnp.where(ok, pltpu.roll(ei, 1, axis=1), 0.0)


def _s5_stitch_call(er, ei, p_re, p_im, nsb):
    shp = jax.ShapeDtypeStruct(er.shape, F32)
    return pl.pallas_call(
        functools.partial(_s5_stitch_kernel, nsb=nsb),
        out_shape=[shp, shp],
        compiler_params=pltpu.CompilerParams(vmem_limit_bytes=VMEM_LIMIT),
        name="s5_stitch",
    )(er, ei, p_re, p_im)


def _s5_core_kernel(zt_ref, tt_ref, nn_ref, mm_ref, are_ref, aim_ref, x0r_ref, x0i_ref, y_ref,
                    xr_scr, xi_scr):
    gb = zt_ref.shape[1]
    nseg = zt_ref.shape[3]

    @pl.when(pl.program_id(1) == 0)
    def _():
        xr_scr[...] = x0r_ref[...]
        xi_scr[...] = x0i_ref[...]

    zt = zt_ref[0]
    xr, xi = xr_scr[...], xi_scr[...]
    xprev = jnp.concatenate([xr.reshape(gb, S5_STATE, nseg), xi.reshape(gb, S5_STATE, nseg)],
                            axis=1).astype(zt.dtype)
    y = _bdot(tt_ref[...], zt) + _bdot(mm_ref[...], xprev)
    s = _bdot(nn_ref[...], zt)
    ar, ai = are_ref[...], aim_ref[...]
    xr_scr[...] = ar * xr - ai * xi + s[:, :S5_STATE, :].reshape(gb * S5_STATE, nseg)
    xi_scr[...] = ar * xi + ai * xr + s[:, S5_STATE:, :].reshape(gb * S5_STATE, nseg)
    for t in range(S5_SUB):
        blk = y[:, t * S5_GROUP:(t + 1) * S5_GROUP, :].reshape(gb * S5_GROUP, nseg)
        y_ref[:, t, :] = blk.T


def _s5_core_call(zt, tt, nn, mm, a_re, a_im, x0r, x0i, dm):
    steps, groups, width, nseg = zt.shape
    gb = min(S5_GB, groups)
    rows = gb * S5_STATE
    per_gb = lambda shape: pl.BlockSpec(shape, lambda j, i: (j,) + (0,) * (len(shape) - 1))
    return pl.pallas_call(
        _s5_core_kernel,
        grid=(groups // gb, steps),
        in_specs=[pl.BlockSpec((1, gb, width, nseg), lambda j, i: (i, j, 0, 0)),
                  per_gb((gb, width, width)), per_gb((gb, 2 * S5_STATE, width)),
                  per_gb((gb, width, 2 * S5_STATE)),
                  per_gb((rows, nseg)), per_gb((rows, nseg)), per_gb((rows, nseg)),
                  per_gb((rows, nseg))],
        out_specs=pl.BlockSpec((nseg, S5_SUB, gb * S5_GROUP), lambda j, i: (0, i, j)),
        out_shape=jax.ShapeDtypeStruct((nseg, steps * S5_SUB, dm), F32),
        scratch_shapes=[pltpu.VMEM((rows, nseg), F32), pltpu.VMEM((rows, nseg), F32)],
        compiler_params=_cparams(("parallel", "arbitrary")),
        name="s5_core",
    )(zt, tt, nn, mm, a_re, a_im, x0r, x0i)


def _s5_out_kernel(x_ref, y_ref, g_ref, d_ref, w_ref, b_ref, o_ref):
    x = x_ref[...]
    u = _rms(x, g_ref[...])
    y = y_ref[...] + d_ref[...] * u
    y = jax.nn.gelu(y)
    gate = jax.nn.sigmoid(_dot(y.astype(BF16), w_ref[...]) + b_ref[...])
    o_ref[...] = x + y * gate


def _s5_out_call(x, y, g, d, w, b, row):
    t, dm = x.shape
    tile = pl.BlockSpec((row, dm), lambda i: (i, 0))
    return pl.pallas_call(
        _s5_out_kernel,
        grid=(t // row,),
        in_specs=[tile, tile, _resident((1, dm)), _resident((1, dm)), _resident((dm, dm)),
                  _resident((1, dm))],
        out_specs=tile,
        out_shape=jax.ShapeDtypeStruct((t, dm), F32),
        compiler_params=_cparams(("parallel",)),
        name="s5_out",
    )(x, y, g.reshape(1, dm), d.reshape(1, dm), w, b.reshape(1, dm))


def _s5_tables(lam_re, lam_im, log_step, b_re, b_im, c_re, c_im, seg_len, nsb, nseg):
    sub = S5_SUB
    step = jnp.exp(log_step)[:, None]
    dre, dim = lam_re * step, lam_im * step

    def power(k):
        kk = jnp.asarray(k, F32)[:, None, None]
        mag = jnp.exp(kk * dre[None])
        return mag * jnp.cos(kk * dim[None]), mag * jnp.sin(kk * dim[None])

    pw_re, pw_im = power(jnp.arange(sub + 1))
    den = lam_re * lam_re + lam_im * lam_im
    nr, ni = pw_re[1] - 1.0, pw_im[1]
    cf_re = (nr * lam_re + ni * lam_im) / den
    cf_im = (ni * lam_re - nr * lam_im) / den
    bb_re = cf_re[..., None] * b_re - cf_im[..., None] * b_im
    bb_im = cf_re[..., None] * b_im + cf_im[..., None] * b_re
    hi = lax.Precision.HIGHEST
    cl_re = c_re[None] * pw_re[:sub, :, None, :] - c_im[None] * pw_im[:sub, :, None, :]
    cl_im = c_re[None] * pw_im[:sub, :, None, :] + c_im[None] * pw_re[:sub, :, None, :]
    kk = (jnp.einsum('kgnp,gpm->kgnm', cl_re, bb_re, precision=hi)
          - jnp.einsum('kgnp,gpm->kgnm', cl_im, bb_im, precision=hi))
    t_idx = jnp.arange(sub)[:, None]
    s_idx = jnp.arange(sub)[None, :]
    lag = t_idx - s_idx
    kk_lag = kk[jnp.clip(lag, 0, sub - 1)]
    kk_lag = jnp.where((lag >= 0)[:, :, None, None, None], kk_lag, 0.0)
    tt = kk_lag.transpose(2, 0, 3, 1, 4).reshape(-1, sub * S5_GROUP, sub * S5_GROUP)
    rev_re, rev_im = pw_re[:sub][::-1], pw_im[:sub][::-1]
    nn_re = rev_re[..., None] * bb_re[None] - rev_im[..., None] * bb_im[None]
    nn_im = rev_re[..., None] * bb_im[None] + rev_im[..., None] * bb_re[None]
    nn = jnp.concatenate([nn_re, nn_im], axis=2)
    nn = nn.transpose(1, 2, 0, 3).reshape(-1, 2 * S5_STATE, sub * S5_GROUP)
    c1_re = c_re[None] * pw_re[1:, :, None, :] - c_im[None] * pw_im[1:, :, None, :]
    c1_im = c_re[None] * pw_im[1:, :, None, :] + c_im[None] * pw_re[1:, :, None, :]
    mm = jnp.concatenate([c1_re, -c1_im], axis=3)
    mm = mm.transpose(1, 0, 2, 3).reshape(-1, sub * S5_GROUP, 2 * S5_STATE)
    a_re = jnp.broadcast_to(pw_re[sub].reshape(-1, 1), (pw_re[sub].size, nseg))
    a_im = jnp.broadcast_to(pw_im[sub].reshape(-1, 1), (pw_im[sub].size, nseg))
    nd = max(1, (nsb - 1).bit_length())
    p_re, p_im = power(seg_len * (2 ** jnp.arange(nd)))
    p_re = p_re.reshape(nd, -1).T
    p_im = p_im.reshape(nd, -1).T
    return tt.astype(BF16), nn.astype(BF16), mm.astype(BF16), a_re, a_im, p_re, p_im


def _s5_layer(x, batch, seq, norm_w, lam_re, lam_im, log_step, b_re, b_im, c_re, c_im, d, glu_w,
              glu_b, row):
    t, dm = x.shape
    nseg = V7X_LANES
    seg_len = t // nseg
    assert t % nseg == 0 and seg_len % S5_SUB == 0 and seq % seg_len == 0
    nsb = seq // seg_len
    tt, nn, mm, a_re, a_im, p_re, p_im = _s5_tables(
        lam_re, lam_im, log_step, b_re, b_im, c_re, c_im, seg_len, nsb, nseg)
    zt, er, ei = _s5_pre_call(x.reshape(nseg, seg_len, dm), norm_w, nn, a_re, a_im)
    x0r, x0i = _s5_stitch_call(er, ei, p_re, p_im, nsb)
    y = _s5_core_call(zt, tt, nn, mm, a_re, a_im, x0r, x0i, dm).reshape(t, dm)
    return _s5_out_call(x, y, norm_w, d.reshape(-1), glu_w.astype(BF16), glu_b, row)


def _ffn_kernel(x_ref, g_ref, wg_ref, wu_ref, wd_ref, f_ref, o_ref, acc_ref, *, hc, final):
    x = x_ref[...]
    u = _rms(x, g_ref[...]).astype(BF16)
    hidden = wg_ref.shape[1]
    for j in range(hidden // hc):
        cols = slice(j * hc, (j + 1) * hc)
        gt = _dot(u, wg_ref[:, cols])
        up = _dot(u, wu_ref[:, cols])
        h = (jax.nn.silu(gt) * up).astype(BF16)
        part = _dot(h, wd_ref[cols, :])
        if j == 0:
            acc_ref[...] = part
        else:
            acc_ref[...] += part
    y = x + acc_ref[...]
    if final:
        y = _rms(y, f_ref[...])
    o_ref[...] = y


def _ffn_call(x, g, wg, wu, wd, fw, row, final):
    t, dm = x.shape
    hidden = wg.shape[1]
    hc = 256
    assert hidden % hc == 0
    tile = pl.BlockSpec((row, dm), lambda i: (i, 0))
    kern = functools.partial(_ffn_kernel, hc=hc, final=final)
    return pl.pallas_call(
        kern,
        grid=(t // row,),
        in_specs=[tile, _resident((1, dm)), _resident((dm, hidden)), _resident((dm, hidden)),
                  _resident((hidden, dm)), _resident((1, dm))],
        out_specs=tile,
        out_shape=jax.ShapeDtypeStruct((t, dm), F32),
        scratch_shapes=[pltpu.VMEM((row, dm), F32)],
        compiler_params=_cparams(("parallel",)),
        name="ffn",
    )(x, g.reshape(1, dm), wg.astype(BF16), wu.astype(BF16), wd.astype(BF16), fw.reshape(1, dm))


def _ret_kernel(x_ref, g_ref, w_ref, cos_ref, sin_ref, gn_ref, dm_ref, qd_ref, kd_ref, cd_ref,
                y_ref, state_ref, *, heads, blk):
    @pl.when(pl.program_id(1) == 0)
    def _():
        state_ref[...] = jnp.zeros_like(state_ref)

    u = _rms(x_ref[...], g_ref[...]).astype(BF16)
    cos = cos_ref[...]
    sin = sin_ref[...]
    half = RET_QK_DIM // 2
    qk_w = heads * RET_QK_DIM
    v_w = heads * RET_V_DIM
    k_scale = RET_QK_DIM ** -0.5
    row = x_ref.shape[0]

    def rotary(tq):
        t1, t2 = tq[:, :half], tq[:, half:]
        return jnp.concatenate([t1 * cos - t2 * sin, t1 * sin + t2 * cos], axis=-1)

    for h in range(heads):
        cq = h * RET_QK_DIM
        cv = 2 * qk_w + h * RET_V_DIM
        q = rotary(_dot(u, w_ref[:, cq:cq + RET_QK_DIM])).astype(BF16)
        k = rotary(_dot(u, w_ref[:, qk_w + cq:qk_w + cq + RET_QK_DIM]) * k_scale)
        v = _dot(u, w_ref[:, cv:cv + RET_V_DIM]).astype(BF16)
        gt = _dot(u, w_ref[:, v_w + cv:v_w + cv + RET_V_DIM])
        gn = gn_ref[:, h * RET_V_DIM:(h + 1) * RET_V_DIM]
        for r0 in range(0, row, blk):
            qs, ks, vs = q[r0:r0 + blk], k[r0:r0 + blk], v[r0:r0 + blk]
            s = lax.dot_general(qs, ks.astype(BF16), (((1,), (1,)), ((), ())),
                                preferred_element_type=F32)
            o = _dot((s * dm_ref[h]).astype(BF16), vs)
            st = state_ref[h]
            o = o + _dot(qs, st.astype(BF16)) * qd_ref[h]
            kd = (ks * kd_ref[h]).astype(BF16)
            state_ref[h] = st * cd_ref[h] + lax.dot_general(
                kd, vs, (((0,), (0,)), ((), ())), preferred_element_type=F32)
            mean = jnp.mean(o, axis=-1, keepdims=True)
            cen = o - mean
            var = jnp.mean(jnp.square(cen), axis=-1, keepdims=True)
            on = cen * lax.rsqrt(var + NORM_EPS) * gn
            y_ref[r0:r0 + blk, h * RET_V_DIM:(h + 1) * RET_V_DIM] = (
                jax.nn.silu(gt[r0:r0 + blk]) * on).astype(y_ref.dtype)


def _ret_call(x, g, w, cos, sin, gn_w, dmask, qdec, kdec, cdec, batch, seq, heads, row, blk):
    t, dm = x.shape
    v_w = heads * RET_V_DIM
    nb = seq // row
    kern = functools.partial(_ret_kernel, heads=heads, blk=blk)
    rows = lambda width: pl.BlockSpec((row, width), lambda b, i: (b * nb + i, 0))
    tab = pl.BlockSpec((row, RET_QK_DIM // 2), lambda b, i: (i, 0))
    return pl.pallas_call(
        kern,
        grid=(batch, nb),
        in_specs=[rows(dm), _resident((1, dm)), _resident(w.shape), tab, tab,
                  _resident((1, v_w)), _resident(dmask.shape), _resident(qdec.shape),
                  _resident(kdec.shape), _resident(cdec.shape)],
        out_specs=rows(v_w),
        out_shape=jax.ShapeDtypeStruct((t, v_w), BF16),
        scratch_shapes=[pltpu.VMEM((heads, RET_QK_DIM, RET_V_DIM), F32)],
        compiler_params=_cparams(("parallel", "arbitrary")),
        name="ret_mix",
    )(x, g.reshape(1, dm), w, cos, sin, gn_w.reshape(1, v_w), dmask, qdec, kdec, cdec)


def _ret_decays(heads, blk):
    log_gamma = jnp.log1p(-jnp.exp2(-5.0 - jnp.arange(heads, dtype=F32)))
    pos = jnp.arange(blk, dtype=F32)
    diff = pos[:, None] - pos[None, :]
    cn = (jnp.arange(blk) // CHUNK)[:, None]
    cm = (jnp.arange(blk) // CHUNK)[None, :]
    expo = jnp.where(cn == cm, jnp.abs(diff), diff)
    dmask = jnp.where((cm <= cn)[None], jnp.exp(log_gamma[:, None, None] * expo[None]), 0.0)
    qdec = jnp.exp((pos[None, :] + 1.0) * log_gamma[:, None])[..., None]
    kdec = jnp.exp((blk - 1.0 - pos)[None, :] * log_gamma[:, None])[..., None]
    cdec = jnp.exp(blk * log_gamma)[:, None, None]
    return dmask, qdec, kdec, cdec


def _rotary_tables(seq):
    inv_freq = 1.0 / (ROPE_BASE ** jnp.linspace(0.0, 1.0, RET_QK_DIM // 2, dtype=F32))
    ang = jnp.arange(seq, dtype=F32)[:, None] * inv_freq[None, :]
    return jnp.cos(ang), jnp.sin(ang)


def _out_proj_kernel(y_ref, w_ref, x_ref, o_ref):
    o_ref[...] = x_ref[...] + _dot(y_ref[...], w_ref[...])


def _out_proj_call(y, w, x, row):
    t, dm = x.shape
    kdim = y.shape[1]
    return pl.pallas_call(
        _out_proj_kernel,
        grid=(t // row,),
        in_specs=[pl.BlockSpec((row, kdim), lambda i: (i, 0)), _resident((kdim, dm)),
                  pl.BlockSpec((row, dm), lambda i: (i, 0))],
        out_specs=pl.BlockSpec((row, dm), lambda i: (i, 0)),
        out_shape=jax.ShapeDtypeStruct((t, dm), F32),
        compiler_params=_cparams(("parallel",)),
        name="ret_out",
    )(y, w, x)


def _ret_layer(x, batch, seq, norm_w, w_qkvg, gn_w, w_o, row, blk):
    dm = x.shape[1]
    heads = dm // RET_QK_DIM
    cos, sin = _rotary_tables(seq)
    dmask, qdec, kdec, cdec = _ret_decays(heads, blk)
    y = _ret_call(x, norm_w, w_qkvg.astype(BF16), cos, sin, gn_w, dmask, qdec, kdec, cdec,
                  batch, seq, heads, row, blk)
    return _out_proj_call(y, w_o.astype(BF16), x, row)


def kernel(x, s5_norm, s5_lambda_re, s5_lambda_im, s5_log_step, s5_b_re, s5_b_im, s5_c_re, s5_c_im,
           s5_d, s5_glu_w, s5_glu_b, ret_norm, ret_w_qkvg, ret_gn_w, ret_w_o, ffn_norm, ffn_w_gate,
           ffn_w_up, ffn_w_down, final_norm):
    batch, seq, dm = x.shape
    row, blk = _tiles(seq)
    depth = ffn_norm.shape[0]
    h = x.reshape(batch * seq, dm)
    for i in range(depth):
        j = i // 2
        if i % 2 == 0:
            h = _s5_layer(h, batch, seq, s5_norm[j], s5_lambda_re[j], s5_lambda_im[j],
                          s5_log_step[j], s5_b_re[j], s5_b_im[j], s5_c_re[j], s5_c_im[j], s5_d[j],
                          s5_glu_w[j], s5_glu_b[j], row)
        else:
            h = _ret_layer(h, batch, seq, ret_norm[j], ret_w_qkvg[j], ret_gn_w[j], ret_w_o[j],
                           row, blk)
        h = _ffn_call(h, ffn_norm[i], ffn_w_gate[i], ffn_w_up[i], ffn_w_down[i], final_norm,
                      row, final=(i == depth - 1))
    return h.reshape(batch, seq, dm)
```

```python
import functools

import jax
import jax.numpy as jnp
from jax import lax
from jax.experimental import pallas as pl
from jax.experimental.pallas import tpu as pltpu

F32 = jnp.float32
BF16 = jnp.bfloat16

NORM_EPS = 1e-6
CHUNK = 64
S5_GROUP = 16
S5_STATE = 64
S5_SUB = 16
S5_GB = 16
RET_QK_DIM = 256
RET_V_DIM = 512
ROPE_BASE = 10000.0

V7X_VMEM_BYTES = 64 * 1024 * 1024
V7X_LANES = 128
VMEM_LIMIT = 56 * 1024 * 1024


def _tiles(seq):
    row = min(512, seq)
    ret = min(256, seq)
    hc = 256
    assert seq % row == 0 and row % ret == 0 and ret % CHUNK == 0
    return row, ret, hc


def _cparams(sem):
    return pltpu.CompilerParams(dimension_semantics=sem, vmem_limit_bytes=VMEM_LIMIT)


def _resident(shape):
    nd = len(shape)
    return pl.BlockSpec(shape, lambda *_: (0,) * nd, pipeline_mode=pl.Buffered(1))


def _rms(xf, g):
    ms = jnp.mean(jnp.square(xf), axis=-1, keepdims=True)
    return xf * lax.rsqrt(ms + NORM_EPS) * g


def _dot(a, b):
    return jnp.dot(a, b, preferred_element_type=F32)


def _bdot(a, b):
    return lax.dot_general(a, b, (((2,), (1,)), ((0,), (0,))), preferred_element_type=F32)


def _s5_pre_kernel(x_ref, g_ref, nn_ref, are_ref, aim_ref, zt_ref, er_ref, ei_ref):
    groups = zt_ref.shape[1]
    nseg = zt_ref.shape[3]
    g = g_ref[...]
    for t in range(S5_SUB):
        ut = _rms(x_ref[:, t, :], g)
        zt_ref[0, :, t * S5_GROUP:(t + 1) * S5_GROUP, :] = (
            ut.T.reshape(groups, S5_GROUP, nseg).astype(zt_ref.dtype))
    s = _bdot(nn_ref[...], zt_ref[0])
    sr = s[:, :S5_STATE, :].reshape(groups * S5_STATE, nseg)
    si = s[:, S5_STATE:, :].reshape(groups * S5_STATE, nseg)

    @pl.when(pl.program_id(0) == 0)
    def _():
        er_ref[...] = sr
        ei_ref[...] = si

    @pl.when(pl.program_id(0) > 0)
    def _():
        er, ei = er_ref[...], ei_ref[...]
        ar, ai = are_ref[...], aim_ref[...]
        er_ref[...] = ar * er - ai * ei + sr
        ei_ref[...] = ar * ei + ai * er + si


def _s5_pre_call(x3, g, nn, a_re, a_im):
    nseg, sl, dm = x3.shape
    groups = dm // S5_GROUP
    steps = sl // S5_SUB
    width = S5_SUB * S5_GROUP
    rows = groups * S5_STATE
    return pl.pallas_call(
        _s5_pre_kernel,
        grid=(steps,),
        in_specs=[pl.BlockSpec((nseg, S5_SUB, dm), lambda i: (0, i, 0)), _resident((1, dm)),
                  _resident(nn.shape), _resident((rows, nseg)), _resident((rows, nseg))],
        out_specs=[pl.BlockSpec((1, groups, width, nseg), lambda i: (i, 0, 0, 0)),
                   pl.BlockSpec((rows, nseg), lambda i: (0, 0)),
                   pl.BlockSpec((rows, nseg), lambda i: (0, 0))],
        out_shape=[jax.ShapeDtypeStruct((steps, groups, width, nseg), BF16),
                   jax.ShapeDtypeStruct((rows, nseg), F32),
                   jax.ShapeDtypeStruct((rows, nseg), F32)],
        compiler_params=_cparams(("arbitrary",)),
        name="s5_pre",
    )(x3, g.reshape(1, dm), nn, a_re, a_im)


def _s5_stitch_kernel(er_ref, ei_ref, pr_ref, pi_ref, xr_ref, xi_ref, *, nsb):
    er, ei = er_ref[...], ei_ref[...]
    lane = lax.broadcasted_iota(jnp.int32, er.shape, 1) % nsb
    d, k = 1, 0
    while d < nsb:
        sr, si = pltpu.roll(er, d, axis=1), pltpu.roll(ei, d, axis=1)
        pr, pi = pr_ref[:, k:k + 1], pi_ref[:, k:k + 1]
        ok = lane >= d
        er, ei = (er + jnp.where(ok, pr * sr - pi * si, 0.0),
                  ei + jnp.where(ok, pr * si + pi * sr, 0.0))
        d, k = 2 * d, k + 1
    ok = lane >= 1
    xr_ref[...] = jnp.where(ok, pltpu.roll(er, 1, axis=1), 0.0)
    xi_ref[...] = jnp.where(ok, pltpu.roll(ei, 1, axis=1), 0.0)


def _s5_stitch_call(er, ei, p_re, p_im, nsb):
    shp = jax.ShapeDtypeStruct(er.shape, F32)
    return pl.pallas_call(
        functools.partial(_s5_stitch_kernel, nsb=nsb),
        out_shape=[shp, shp],
        compiler_params=pltpu.CompilerParams(vmem_limit_bytes=VMEM_LIMIT),
        name="s5_stitch",
    )(er, ei, p_re, p_im)


def _s5_core_kernel(zt_ref, tt_ref, nn_ref, mm_ref, are_ref, aim_ref, x0r_ref, x0i_ref, y_ref,
                    xr_scr, xi_scr):
    gb = zt_ref.shape[1]
    nseg = zt_ref.shape[3]

    @pl.when(pl.program_id(1) == 0)
    def _():
        xr_scr[...] = x0r_ref[...]
        xi_scr[...] = x0i_ref[...]

    zt = zt_ref[0]
    xr, xi = xr_scr[...], xi_scr[...]
    xprev = jnp.concatenate([xr.reshape(gb, S5_STATE, nseg), xi.reshape(gb, S5_STATE, nseg)],
                            axis=1).astype(zt.dtype)
    y = _bdot(tt_ref[...], zt) + _bdot(mm_ref[...], xprev)
    s = _bdot(nn_ref[...], zt)
    ar, ai = are_ref[...], aim_ref[...]
    xr_scr[...] = ar * xr - ai * xi + s[:, :S5_STATE, :].reshape(gb * S5_STATE, nseg)
    xi_scr[...] = ar * xi + ai * xr + s[:, S5_STATE:, :].reshape(gb * S5_STATE, nseg)
    for t in range(S5_SUB):
        blk = y[:, t * S5_GROUP:(t + 1) * S5_GROUP, :].reshape(gb * S5_GROUP, nseg)
        y_ref[:, t, :] = blk.T


def _s5_core_call(zt, tt, nn, mm, a_re, a_im, x0r, x0i, dm):
    steps, groups, width, nseg = zt.shape
    gb = min(S5_GB, groups)
    rows = gb * S5_STATE
    per_gb = lambda shape: pl.BlockSpec(shape, lambda j, i: (j,) + (0,) * (len(shape) - 1))
    return pl.pallas_call(
        _s5_core_kernel,
        grid=(groups // gb, steps),
        in_specs=[pl.BlockSpec((1, gb, width, nseg), lambda j, i: (i, j, 0, 0)),
                  per_gb((gb, width, width)), per_gb((gb, 2 * S5_STATE, width)),
                  per_gb((gb, width, 2 * S5_STATE)),
                  per_gb((rows, nseg)), per_gb((rows, nseg)), per_gb((rows, nseg)),
                  per_gb((rows, nseg))],
        out_specs=pl.BlockSpec((nseg, S5_SUB, gb * S5_GROUP), lambda j, i: (0, i, j)),
        out_shape=jax.ShapeDtypeStruct((nseg, steps * S5_SUB, dm), F32),
        scratch_shapes=[pltpu.VMEM((rows, nseg), F32), pltpu.VMEM((rows, nseg), F32)],
        compiler_params=_cparams(("parallel", "arbitrary")),
        name="s5_core",
    )(zt, tt, nn, mm, a_re, a_im, x0r, x0i)


def _ffn_rows(x, g_ref, wg_ref, wu_ref, wd_ref, acc_ref, hc):
    u = _rms(x, g_ref[...]).astype(BF16)
    hidden = wg_ref.shape[1]
    for j in range(hidden // hc):
        cols = slice(j * hc, (j + 1) * hc)
        gt = _dot(u, wg_ref[:, cols])
        up = _dot(u, wu_ref[:, cols])
        h = (jax.nn.silu(gt) * up).astype(BF16)
        part = _dot(h, wd_ref[cols, :])
        if j == 0:
            acc_ref[...] = part
        else:
            acc_ref[...] += part
    return x + acc_ref[...]


def _ffn_specs(row, dm, hidden):
    tile = pl.BlockSpec((row, dm), lambda i: (i, 0))
    weights = [_resident((1, dm)), _resident((dm, hidden)), _resident((dm, hidden)),
               _resident((hidden, dm))]
    return tile, weights


def _s5_ffn_kernel(x_ref, y_ref, g_ref, d_ref, w_ref, b_ref, fg_ref, wg_ref, wu_ref, wd_ref,
                   o_ref, acc_ref, *, hc):
    x = x_ref[...]
    u = _rms(x, g_ref[...])
    y = jax.nn.gelu(y_ref[...] + d_ref[...] * u)
    gate = jax.nn.sigmoid(_dot(y.astype(BF16), w_ref[...]) + b_ref[...])
    o_ref[...] = _ffn_rows(x + y * gate, fg_ref, wg_ref, wu_ref, wd_ref, acc_ref, hc)


def _s5_ffn_call(x, y, g, d, w, b, fg, wg, wu, wd, row, hc):
    t, dm = x.shape
    hidden = wg.shape[1]
    tile, weights = _ffn_specs(row, dm, hidden)
    return pl.pallas_call(
        functools.partial(_s5_ffn_kernel, hc=hc),
        grid=(t // row,),
        in_specs=[tile, tile, _resident((1, dm)), _resident((1, dm)), _resident((dm, dm)),
                  _resident((1, dm))] + weights,
        out_specs=tile,
        out_shape=jax.ShapeDtypeStruct((t, dm), F32),
        scratch_shapes=[pltpu.VMEM((row, dm), F32)],
        compiler_params=_cparams(("parallel",)),
        name="s5_ffn",
    )(x, y, g.reshape(1, dm), d.reshape(1, dm), w, b.reshape(1, dm), fg.reshape(1, dm), wg, wu, wd)


def _s5_tables(lam_re, lam_im, log_step, b_re, b_im, c_re, c_im, seg_len, nsb, nseg):
    sub = S5_SUB
    step = jnp.exp(log_step)[:, None]
    dre, dim = lam_re * step, lam_im * step

    def power(k):
        kk = jnp.asarray(k, F32)[:, None, None]
        mag = jnp.exp(kk * dre[None])
        return mag * jnp.cos(kk * dim[None]), mag * jnp.sin(kk * dim[None])

    pw_re, pw_im = power(jnp.arange(sub + 1))
    den = lam_re * lam_re + lam_im * lam_im
    nr, ni = pw_re[1] - 1.0, pw_im[1]
    cf_re = (nr * lam_re + ni * lam_im) / den
    cf_im = (ni * lam_re - nr * lam_im) / den
    bb_re = cf_re[..., None] * b_re - cf_im[..., None] * b_im
    bb_im = cf_re[..., None] * b_im + cf_im[..., None] * b_re
    hi = lax.Precision.HIGHEST
    cl_re = c_re[None] * pw_re[:sub, :, None, :] - c_im[None] * pw_im[:sub, :, None, :]
    cl_im = c_re[None] * pw_im[:sub, :, None, :] + c_im[None] * pw_re[:sub, :, None, :]
    kk = (jnp.einsum('kgnp,gpm->kgnm', cl_re, bb_re, precision=hi)
          - jnp.einsum('kgnp,gpm->kgnm', cl_im, bb_im, precision=hi))
    t_idx = jnp.arange(sub)[:, None]
    s_idx = jnp.arange(sub)[None, :]
    lag = t_idx - s_idx
    kk_lag = kk[jnp.clip(lag, 0, sub - 1)]
    kk_lag = jnp.where((lag >= 0)[:, :, None, None, None], kk_lag, 0.0)
    tt = kk_lag.transpose(2, 0, 3, 1, 4).reshape(-1, sub * S5_GROUP, sub * S5_GROUP)
    rev_re, rev_im = pw_re[:sub][::-1], pw_im[:sub][::-1]
    nn_re = rev_re[..., None] * bb_re[None] - rev_im[..., None] * bb_im[None]
    nn_im = rev_re[..., None] * bb_im[None] + rev_im[..., None] * bb_re[None]
    nn = jnp.concatenate([nn_re, nn_im], axis=2)
    nn = nn.transpose(1, 2, 0, 3).reshape(-1, 2 * S5_STATE, sub * S5_GROUP)
    c1_re = c_re[None] * pw_re[1:, :, None, :] - c_im[None] * pw_im[1:, :, None, :]
    c1_im = c_re[None] * pw_im[1:, :, None, :] + c_im[None] * pw_re[1:, :, None, :]
    mm = jnp.concatenate([c1_re, -c1_im], axis=3)
    mm = mm.transpose(1, 0, 2, 3).reshape(-1, sub * S5_GROUP, 2 * S5_STATE)
    a_re = jnp.broadcast_to(pw_re[sub].reshape(-1, 1), (pw_re[sub].size, nseg))
    a_im = jnp.broadcast_to(pw_im[sub].reshape(-1, 1), (pw_im[sub].size, nseg))
    nd = max(1, (nsb - 1).bit_length())
    p_re, p_im = power(seg_len * (2 ** jnp.arange(nd)))
    p_re = p_re.reshape(nd, -1).T
    p_im = p_im.reshape(nd, -1).T
    return tt.astype(BF16), nn.astype(BF16), mm.astype(BF16), a_re, a_im, p_re, p_im


def _s5_layer(x, batch, seq, norm_w, lam_re, lam_im, log_step, b_re, b_im, c_re, c_im, d, glu_w,
              glu_b, ffn_g, ffn_wg, ffn_wu, ffn_wd, row, hc):
    t, dm = x.shape
    nseg = V7X_LANES
    seg_len = t // nseg
    assert t % nseg == 0 and seg_len % S5_SUB == 0 and seq % seg_len == 0
    nsb = seq // seg_len
    tt, nn, mm, a_re, a_im, p_re, p_im = _s5_tables(
        lam_re, lam_im, log_step, b_re, b_im, c_re, c_im, seg_len, nsb, nseg)
    zt, er, ei = _s5_pre_call(x.reshape(nseg, seg_len, dm), norm_w, nn, a_re, a_im)
    x0r, x0i = _s5_stitch_call(er, ei, p_re, p_im, nsb)
    y = _s5_core_call(zt, tt, nn, mm, a_re, a_im, x0r, x0i, dm).reshape(t, dm)
    return _s5_ffn_call(x, y, norm_w, d.reshape(-1), glu_w.astype(BF16), glu_b, ffn_g,
                        ffn_wg.astype(BF16), ffn_wu.astype(BF16), ffn_wd.astype(BF16), row, hc)


def _ret_kernel(x_ref, g_ref, w_ref, cos_ref, sin_ref, gn_ref, dm_ref, qd_ref, kd_ref, cd_ref,
                y_ref, state_ref, *, heads, blk):
    @pl.when(pl.program_id(1) == 0)
    def _():
        state_ref[...] = jnp.zeros_like(state_ref)

    u = _rms(x_ref[...], g_ref[...]).astype(BF16)
    cos = cos_ref[...]
    sin = sin_ref[...]
    half = RET_QK_DIM // 2
    qk_w = heads * RET_QK_DIM
    v_w = heads * RET_V_DIM
    k_scale = RET_QK_DIM ** -0.5
    row = x_ref.shape[0]

    def rotary(tq):
        t1, t2 = tq[:, :half], tq[:, half:]
        return jnp.concatenate([t1 * cos - t2 * sin, t1 * sin + t2 * cos], axis=-1)

    for h in range(heads):
        cq = h * RET_QK_DIM
        cv = 2 * qk_w + h * RET_V_DIM
        q = rotary(_dot(u, w_ref[:, cq:cq + RET_QK_DIM])).astype(BF16)
        k = rotary(_dot(u, w_ref[:, qk_w + cq:qk_w + cq + RET_QK_DIM]) * k_scale)
        v = _dot(u, w_ref[:, cv:cv + RET_V_DIM]).astype(BF16)
        gt = _dot(u, w_ref[:, v_w + cv:v_w + cv + RET_V_DIM])
        gn = gn_ref[:, h * RET_V_DIM:(h + 1) * RET_V_DIM]
        for r0 in range(0, row, blk):
            qs, ks, vs = q[r0:r0 + blk], k[r0:r0 + blk], v[r0:r0 + blk]
            s = lax.dot_general(qs, ks.astype(BF16), (((1,), (1,)), ((), ())),
                                preferred_element_type=F32)
            o = _dot((s * dm_ref[h]).astype(BF16), vs)
            st = state_ref[h]
            o = o + _dot(qs, st.astype(BF16)) * qd_ref[h]
            kd = (ks * kd_ref[h]).astype(BF16)
            state_ref[h] = st * cd_ref[h] + lax.dot_general(
                kd, vs, (((0,), (0,)), ((), ())), preferred_element_type=F32)
            mean = jnp.mean(o, axis=-1, keepdims=True)
            cen = o - mean
            var = jnp.mean(jnp.square(cen), axis=-1, keepdims=True)
            on = cen * lax.rsqrt(var + NORM_EPS) * gn
            y_ref[r0:r0 + blk, h * RET_V_DIM:(h + 1) * RET_V_DIM] = (
                jax.nn.silu(gt[r0:r0 + blk]) * on).astype(y_ref.dtype)


def _ret_call(x, g, w, cos, sin, gn_w, dmask, qdec, kdec, cdec, batch, seq, heads, row, blk):
    t, dm = x.shape
    v_w = heads * RET_V_DIM
    nb = seq // row
    kern = functools.partial(_ret_kernel, heads=heads, blk=blk)
    rows = lambda width: pl.BlockSpec((row, width), lambda b, i: (b * nb + i, 0))
    tab = pl.BlockSpec((row, RET_QK_DIM // 2), lambda b, i: (i, 0))
    return pl.pallas_call(
        kern,
        grid=(batch, nb),
        in_specs=[rows(dm), _resident((1, dm)), _resident(w.shape), tab, tab,
                  _resident((1, v_w)), _resident(dmask.shape), _resident(qdec.shape),
                  _resident(kdec.shape), _resident(cdec.shape)],
        out_specs=rows(v_w),
        out_shape=jax.ShapeDtypeStruct((t, v_w), BF16),
        scratch_shapes=[pltpu.VMEM((heads, RET_QK_DIM, RET_V_DIM), F32)],
        compiler_params=_cparams(("parallel", "arbitrary")),
        name="ret_mix",
    )(x, g.reshape(1, dm), w, cos, sin, gn_w.reshape(1, v_w), dmask, qdec, kdec, cdec)


def _ret_decays(heads, blk):
    log_gamma = jnp.log1p(-jnp.exp2(-5.0 - jnp.arange(heads, dtype=F32)))
    pos = jnp.arange(blk, dtype=F32)
    diff = pos[:, None] - pos[None, :]
    cn = (jnp.arange(blk) // CHUNK)[:, None]
    cm = (jnp.arange(blk) // CHUNK)[None, :]
    expo = jnp.where(cn == cm, jnp.abs(diff), diff)
    dmask = jnp.where((cm <= cn)[None], jnp.exp(log_gamma[:, None, None] * expo[None]), 0.0)
    qdec = jnp.exp((pos[None, :] + 1.0) * log_gamma[:, None])[..., None]
    kdec = jnp.exp((blk - 1.0 - pos)[None, :] * log_gamma[:, None])[..., None]
    cdec = jnp.exp(blk * log_gamma)[:, None, None]
    return dmask, qdec, kdec, cdec


def _rotary_tables(seq):
    inv_freq = 1.0 / (ROPE_BASE ** jnp.linspace(0.0, 1.0, RET_QK_DIM // 2, dtype=F32))
    ang = jnp.arange(seq, dtype=F32)[:, None] * inv_freq[None, :]
    return jnp.cos(ang), jnp.sin(ang)


def _ret_ffn_kernel(x_ref, y_ref, w_ref, fg_ref, wg_ref, wu_ref, wd_ref, fin_ref, o_ref, acc_ref,
                    *, hc):
    x1 = x_ref[...] + _dot(y_ref[...], w_ref[...])
    out = _ffn_rows(x1, fg_ref, wg_ref, wu_ref, wd_ref, acc_ref, hc)
    o_ref[...] = _rms(out, fin_ref[...])


def _ret_ffn_call(x, y, w, fg, wg, wu, wd, fin, row, hc):
    t, dm = x.shape
    hidden = wg.shape[1]
    kdim = y.shape[1]
    tile, weights = _ffn_specs(row, dm, hidden)
    return pl.pallas_call(
        functools.partial(_ret_ffn_kernel, hc=hc),
        grid=(t // row,),
        in_specs=[tile, pl.BlockSpec((row, kdim), lambda i: (i, 0)), _resident((kdim, dm))]
        + weights + [_resident((1, dm))],
        out_specs=tile,
        out_shape=jax.ShapeDtypeStruct((t, dm), F32),
        scratch_shapes=[pltpu.VMEM((row, dm), F32)],
        compiler_params=_cparams(("parallel",)),
        name="ret_ffn",
    )(x, y, w, fg.reshape(1, dm), wg, wu, wd, fin.reshape(1, dm))


def _ret_layer(x, batch, seq, norm_w, w_qkvg, gn_w, w_o, ffn_g, ffn_wg, ffn_wu, ffn_wd, fin,
               row, blk, hc):
    dm = x.shape[1]
    heads = dm // RET_QK_DIM
    cos, sin = _rotary_tables(seq)
    dmask, qdec, kdec, cdec = _ret_decays(heads, blk)
    y = _ret_call(x, norm_w, w_qkvg.astype(BF16), cos, sin, gn_w, dmask, qdec, kdec, cdec,
                  batch, seq, heads, row, blk)
    return _ret_ffn_call(x, y, w_o.astype(BF16), ffn_g, ffn_wg.astype(BF16), ffn_wu.astype(BF16),
                         ffn_wd.astype(BF16), fin, row, hc)


def kernel(x, s5_norm, s5_lambda_re, s5_lambda_im, s5_log_step, s5_b_re, s5_b_im, s5_c_re, s5_c_im,
           s5_d, s5_glu_w, s5_glu_b, ret_norm, ret_w_qkvg, ret_gn_w, ret_w_o, ffn_norm, ffn_w_gate,
           ffn_w_up, ffn_w_down, final_norm):
    batch, seq, dm = x.shape
    row, blk, hc = _tiles(seq)
    assert ffn_norm.shape[0] == 2 and s5_norm.shape[0] == 1 and ret_norm.shape[0] == 1
    h = x.reshape(batch * seq, dm)
    h = _s5_layer(h, batch, seq, s5_norm[0], s5_lambda_re[0], s5_lambda_im[0], s5_log_step[0],
                  s5_b_re[0], s5_b_im[0], s5_c_re[0], s5_c_im[0], s5_d[0], s5_glu_w[0],
                  s5_glu_b[0], ffn_norm[0], ffn_w_gate[0], ffn_w_up[0], ffn_w_down[0], row, hc)
    h = _ret_layer(h, batch, seq, ret_norm[0], ret_w_qkvg[0], ret_gn_w[0], ret_w_o[0],
                   ffn_norm[1], ffn_w_gate[1], ffn_w_up[1], ffn_w_down[1], final_norm,
                   row, blk, hc)
    return h.reshape(batch, seq, dm)
```

```python
import functools

import jax
import jax.numpy as jnp
from jax import lax
from jax.experimental import pallas as pl
from jax.experimental.pallas import tpu as pltpu

F32 = jnp.float32
BF16 = jnp.bfloat16

NORM_EPS = 1e-6
CHUNK = 64
S5_GROUP = 16
S5_STATE = 64
S5_SUB = 16
S5_GB = 16
RET_QK_DIM = 256
RET_V_DIM = 512
ROPE_BASE = 10000.0

V7X_VMEM_BYTES = 64 * 1024 * 1024
V7X_LANES = 128
VMEM_LIMIT = 56 * 1024 * 1024


def _tiles(seq):
    row = min(512, seq)
    ret = min(256, seq)
    hc = 256
    assert seq % row == 0 and row % ret == 0 and ret % CHUNK == 0
    return row, ret, hc


def _cparams(sem):
    return pltpu.CompilerParams(dimension_semantics=sem, vmem_limit_bytes=VMEM_LIMIT)


def _resident(shape):
    nd = len(shape)
    return pl.BlockSpec(shape, lambda *_: (0,) * nd, pipeline_mode=pl.Buffered(1))


def _rms(xf, g):
    ms = jnp.mean(jnp.square(xf), axis=-1, keepdims=True)
    return xf * lax.rsqrt(ms + NORM_EPS) * g


def _dot(a, b):
    return jnp.dot(a, b, preferred_element_type=F32)


def _bdot(a, b):
    return lax.dot_general(a, b, (((2,), (1,)), ((0,), (0,))), preferred_element_type=F32)


def _s5_pre_kernel(x_ref, g_ref, nn_ref, are_ref, aim_ref, zt_ref, er_ref, ei_ref):
    groups = zt_ref.shape[1]
    nseg = zt_ref.shape[3]
    g = g_ref[...]
    for t in range(S5_SUB):
        ut = _rms(x_ref[:, t, :], g)
        zt_ref[0, :, t * S5_GROUP:(t + 1) * S5_GROUP, :] = (
            ut.T.reshape(groups, S5_GROUP, nseg).astype(zt_ref.dtype))
    s = _bdot(nn_ref[...], zt_ref[0])
    sr = s[:, :S5_STATE, :].reshape(groups * S5_STATE, nseg)
    si = s[:, S5_STATE:, :].reshape(groups * S5_STATE, nseg)

    @pl.when(pl.program_id(0) == 0)
    def _():
        er_ref[...] = sr
        ei_ref[...] = si

    @pl.when(pl.program_id(0) > 0)
    def _():
        er, ei = er_ref[...], ei_ref[...]
        ar, ai = are_ref[...], aim_ref[...]
        er_ref[...] = ar * er - ai * ei + sr
        ei_ref[...] = ar * ei + ai * er + si


def _s5_pre_call(x3, g, nn, a_re, a_im):
    nseg, sl, dm = x3.shape
    groups = dm // S5_GROUP
    steps = sl // S5_SUB
    width = S5_SUB * S5_GROUP
    rows = groups * S5_STATE
    return pl.pallas_call(
        _s5_pre_kernel,
        grid=(steps,),
        in_specs=[pl.BlockSpec((nseg, S5_SUB, dm), lambda i: (0, i, 0)), _resident((1, dm)),
                  _resident(nn.shape), _resident((rows, nseg)), _resident((rows, nseg))],
        out_specs=[pl.BlockSpec((1, groups, width, nseg), lambda i: (i, 0, 0, 0)),
                   pl.BlockSpec((rows, nseg), lambda i: (0, 0)),
                   pl.BlockSpec((rows, nseg), lambda i: (0, 0))],
        out_shape=[jax.ShapeDtypeStruct((steps, groups, width, nseg), BF16),
                   jax.ShapeDtypeStruct((rows, nseg), F32),
                   jax.ShapeDtypeStruct((rows, nseg), F32)],
        compiler_params=_cparams(("arbitrary",)),
        name="s5_pre",
    )(x3, g.reshape(1, dm), nn, a_re, a_im)


def _s5_stitch_kernel(er_ref, ei_ref, pr_ref, pi_ref, xr_ref, xi_ref, *, nsb):
    er, ei = er_ref[...], ei_ref[...]
    lane = lax.broadcasted_iota(jnp.int32, er.shape, 1) % nsb
    d, k = 1, 0
    while d < nsb:
        sr, si = pltpu.roll(er, d, axis=1), pltpu.roll(ei, d, axis=1)
        pr, pi = pr_ref[:, k:k + 1], pi_ref[:, k:k + 1]
        ok = lane >= d
        er, ei = (er + jnp.where(ok, pr * sr - pi * si, 0.0),
                  ei + jnp.where(ok, pr * si + pi * sr, 0.0))
        d, k = 2 * d, k + 1
    ok = lane >= 1
    xr_ref[...] = jnp.where(ok, pltpu.roll(er, 1, axis=1), 0.0)
    xi_ref[...] = jnp.where(ok, pltpu.roll(ei, 1, axis=1), 0.0)


def _s5_stitch_call(er, ei, p_re, p_im, nsb):
    shp = jax.ShapeDtypeStruct(er.shape, F32)
    return pl.pallas_call(
        functools.partial(_s5_stitch_kernel, nsb=nsb),
        out_shape=[shp, shp],
        compiler_params=pltpu.CompilerParams(vmem_limit_bytes=VMEM_LIMIT),
        name="s5_stitch",
    )(er, ei, p_re, p_im)


def _s5_core_kernel(zt_ref, tt_ref, nn_ref, mm_ref, are_ref, aim_ref, x0r_ref, x0i_ref, y_ref,
                    xr_scr, xi_scr):
    gb = zt_ref.shape[1]
    nseg = zt_ref.shape[3]

    @pl.when(pl.program_id(1) == 0)
    def _():
        xr_scr[...] = x0r_ref[...]
        xi_scr[...] = x0i_ref[...]

    zt = zt_ref[0]
    xr, xi = xr_scr[...], xi_scr[...]
    xprev = jnp.concatenate([xr.reshape(gb, S5_STATE, nseg), xi.reshape(gb, S5_STATE, nseg)],
                            axis=1).astype(zt.dtype)
    y = _bdot(tt_ref[...], zt) + _bdot(mm_ref[...], xprev)
    s = _bdot(nn_ref[...], zt)
    ar, ai = are_ref[...], aim_ref[...]
    xr_scr[...] = ar * xr - ai * xi + s[:, :S5_STATE, :].reshape(gb * S5_STATE, nseg)
    xi_scr[...] = ar * xi + ai * xr + s[:, S5_STATE:, :].reshape(gb * S5_STATE, nseg)
    for t in range(S5_SUB):
        blk = y[:, t * S5_GROUP:(t + 1) * S5_GROUP, :].reshape(gb * S5_GROUP, nseg)
        y_ref[:, t, :] = blk.T


def _s5_core_call(zt, tt, nn, mm, a_re, a_im, x0r, x0i, dm):
    steps, groups, width, nseg = zt.shape
    gb = min(S5_GB, groups)
    rows = gb * S5_STATE
    per_gb = lambda shape: pl.BlockSpec(shape, lambda j, i: (j,) + (0,) * (len(shape) - 1))
    return pl.pallas_call(
        _s5_core_kernel,
        grid=(groups // gb, steps),
        in_specs=[pl.BlockSpec((1, gb, width, nseg), lambda j, i: (i, j, 0, 0)),
                  per_gb((gb, width, width)), per_gb((gb, 2 * S5_STATE, width)),
                  per_gb((gb, width, 2 * S5_STATE)),
                  per_gb((rows, nseg)), per_gb((rows, nseg)), per_gb((rows, nseg)),
                  per_gb((rows, nseg))],
        out_specs=pl.BlockSpec((nseg, S5_SUB, gb * S5_GROUP), lambda j, i: (0, i, j)),
        out_shape=jax.ShapeDtypeStruct((nseg, steps * S5_SUB, dm), F32),
        scratch_shapes=[pltpu.VMEM((rows, nseg), F32), pltpu.VMEM((rows, nseg), F32)],
        compiler_params=_cparams(("parallel", "arbitrary")),
        name="s5_core",
    )(zt, tt, nn, mm, a_re, a_im, x0r, x0i)


def _ffn_rows(x, g_ref, wg_ref, wu_ref, wd_ref, acc_ref, hc):
    u = _rms(x, g_ref[...]).astype(BF16)
    hidden = wg_ref.shape[1]
    for j in range(hidden // hc):
        cols = slice(j * hc, (j + 1) * hc)
        gt = _dot(u, wg_ref[:, cols])
        up = _dot(u, wu_ref[:, cols])
        h = (jax.nn.silu(gt) * up).astype(BF16)
        part = _dot(h, wd_ref[cols, :])
        if j == 0:
            acc_ref[...] = part
        else:
            acc_ref[...] += part
    return x + acc_ref[...]


def _ffn_specs(row, dm, hidden):
    tile = pl.BlockSpec((row, dm), lambda i: (i, 0))
    weights = [_resident((1, dm)), _resident((dm, hidden)), _resident((dm, hidden)),
               _resident((hidden, dm))]
    return tile, weights


def _s5_ffn_kernel(x_ref, y_ref, g_ref, d_ref, w_ref, b_ref, fg_ref, wg_ref, wu_ref, wd_ref,
                   o_ref, acc_ref, *, hc):
    x = x_ref[...]
    u = _rms(x, g_ref[...])
    y = jax.nn.gelu(y_ref[...] + d_ref[...] * u)
    gate = jax.nn.sigmoid(_dot(y.astype(BF16), w_ref[...]) + b_ref[...])
    o_ref[...] = _ffn_rows(x + y * gate, fg_ref, wg_ref, wu_ref, wd_ref, acc_ref, hc)


def _s5_ffn_call(x, y, g, d, w, b, fg, wg, wu, wd, row, hc):
    t, dm = x.shape
    hidden = wg.shape[1]
    tile, weights = _ffn_specs(row, dm, hidden)
    return pl.pallas_call(
        functools.partial(_s5_ffn_kernel, hc=hc),
        grid=(t // row,),
        in_specs=[tile, tile, _resident((1, dm)), _resident((1, dm)), _resident((dm, dm)),
                  _resident((1, dm))] + weights,
        out_specs=tile,
        out_shape=jax.ShapeDtypeStruct((t, dm), F32),
        scratch_shapes=[pltpu.VMEM((row, dm), F32)],
        compiler_params=_cparams(("parallel",)),
        name="s5_ffn",
    )(x, y, g.reshape(1, dm), d.reshape(1, dm), w, b.reshape(1, dm), fg.reshape(1, dm), wg, wu, wd)


def _s5_tab_kernel(p_ref, q_ref, c1_ref, c2_ref, b1_ref, b2_ref, bs_ref, tt_ref, nn_ref, mm_ref):
    sub = S5_SUB
    width = sub * S5_GROUP
    lane = lax.broadcasted_iota(jnp.int32, (S5_GROUP, width), 1)
    sgn = jnp.where(lax.broadcasted_iota(jnp.int32, (1, 2 * S5_STATE), 1) < S5_STATE, 1.0, -1.0)
    for gi in range(p_ref.shape[0]):
        p, q = p_ref[gi], q_ref[gi]
        c1, c2, b1, b2 = c1_ref[gi], c2_ref[gi], b1_ref[gi], b2_ref[gi]
        cl = [c1 * p[k:k + 1] + c2 * q[k:k + 1] for k in range(sub + 1)]
        mm_ref[gi] = (jnp.concatenate(cl[1:], axis=0) * sgn).astype(mm_ref.dtype)
        r = jnp.dot(bs_ref[gi], jnp.concatenate(cl[:sub], axis=0).T,
                    precision=lax.Precision.HIGHEST, preferred_element_type=F32)
        rows = [r] + [jnp.where(lane >= S5_GROUP * s, pltpu.roll(r, S5_GROUP * s, axis=1), 0.0)
                      for s in range(1, sub)]
        tt_ref[gi] = jnp.concatenate(rows, axis=0).T.astype(tt_ref.dtype)
        nt = [b1 * p[sub - 1 - s:sub - s] + b2 * q[sub - 1 - s:sub - s] for s in range(sub)]
        nn_ref[gi] = jnp.concatenate(nt, axis=0).T.astype(nn_ref.dtype)


def _s5_tab_call(p, q, c1, c2, b1, b2, bs):
    groups = p.shape[0]
    gt = 4 if groups % 4 == 0 else 1
    width = S5_SUB * S5_GROUP
    st2 = 2 * S5_STATE
    blk = lambda a: pl.BlockSpec((gt,) + a.shape[1:], lambda i: (i, 0, 0))
    out = lambda r, c: pl.BlockSpec((gt, r, c), lambda i: (i, 0, 0))
    return pl.pallas_call(
        _s5_tab_kernel,
        grid=(groups // gt,),
        in_specs=[blk(a) for a in (p, q, c1, c2, b1, b2, bs)],
        out_specs=[out(width, width), out(st2, width), out(width, st2)],
        out_shape=[jax.ShapeDtypeStruct((groups, width, width), BF16),
                   jax.ShapeDtypeStruct((groups, st2, width), BF16),
                   jax.ShapeDtypeStruct((groups, width, st2), BF16)],
        compiler_params=_cparams(("parallel",)),
        name="s5_tab",
    )(p, q, c1, c2, b1, b2, bs)


def _s5_tables(lam_re, lam_im, log_step, b_re, b_im, c_re, c_im, seg_len, nsb, nseg):
    sub = S5_SUB
    step = jnp.exp(log_step)[:, None]
    dre, dim = lam_re * step, lam_im * step

    def power(k):
        kk = jnp.asarray(k, F32)[None, :, None]
        mag = jnp.exp(kk * dre[:, None, :])
        return mag * jnp.cos(kk * dim[:, None, :]), mag * jnp.sin(kk * dim[:, None, :])

    pw_re, pw_im = power(jnp.arange(sub + 1))
    den = lam_re * lam_re + lam_im * lam_im
    nr, ni = pw_re[:, 1] - 1.0, pw_im[:, 1]
    cf_re = ((nr * lam_re + ni * lam_im) / den)[:, None, :]
    cf_im = ((ni * lam_re - nr * lam_im) / den)[:, None, :]
    bt_re, bt_im = b_re.transpose(0, 2, 1), b_im.transpose(0, 2, 1)
    bb_re = cf_re * bt_re - cf_im * bt_im
    bb_im = cf_re * bt_im + cf_im * bt_re
    cat = lambda a, b: jnp.concatenate([a, b], axis=-1)
    tt, nn, mm = _s5_tab_call(cat(pw_re, pw_im), cat(pw_im, pw_re), cat(c_re, c_re),
                              cat(-c_im, c_im), cat(bb_re, bb_re), cat(-bb_im, bb_im),
                              cat(bb_re, -bb_im))
    a_re = jnp.broadcast_to(pw_re[:, sub].reshape(-1, 1), (pw_re[:, sub].size, nseg))
    a_im = jnp.broadcast_to(pw_im[:, sub].reshape(-1, 1), (pw_im[:, sub].size, nseg))
    nd = max(1, (nsb - 1).bit_length())
    p_re, p_im = power(seg_len * (2 ** jnp.arange(nd)))
    p_re = p_re.transpose(0, 2, 1).reshape(-1, nd)
    p_im = p_im.transpose(0, 2, 1).reshape(-1, nd)
    return tt, nn, mm, a_re, a_im, p_re, p_im


def _s5_layer(x, batch, seq, norm_w, lam_re, lam_im, log_step, b_re, b_im, c_re, c_im, d, glu_w,
              glu_b, ffn_g, ffn_wg, ffn_wu, ffn_wd, row, hc):
    t, dm = x.shape
    nseg = V7X_LANES
    seg_len = t // nseg
    assert t % nseg == 0 and seg_len % S5_SUB == 0 and seq % seg_len == 0
    nsb = seq // seg_len
    tt, nn, mm, a_re, a_im, p_re, p_im = _s5_tables(
        lam_re, lam_im, log_step, b_re, b_im, c_re, c_im, seg_len, nsb, nseg)
    zt, er, ei = _s5_pre_call(x.reshape(nseg, seg_len, dm), norm_w, nn, a_re, a_im)
    x0r, x0i = _s5_stitch_call(er, ei, p_re, p_im, nsb)
    y = _s5_core_call(zt, tt, nn, mm, a_re, a_im, x0r, x0i, dm).reshape(t, dm)
    return _s5_ffn_call(x, y, norm_w, d.reshape(-1), glu_w.astype(BF16), glu_b, ffn_g,
                        ffn_wg.astype(BF16), ffn_wu.astype(BF16), ffn_wd.astype(BF16), row, hc)


def _ret_kernel(x_ref, g_ref, w_ref, cos_ref, sin_ref, gn_ref, dm_ref, qd_ref, kd_ref, cd_ref,
                y_ref, state_ref, *, heads, blk):
    @pl.when(pl.program_id(1) == 0)
    def _():
        state_ref[...] = jnp.zeros_like(state_ref)

    u = _rms(x_ref[...], g_ref[...]).astype(BF16)
    cos = cos_ref[...]
    sin = sin_ref[...]
    half = RET_QK_DIM // 2
    qk_w = heads * RET_QK_DIM
    v_w = heads * RET_V_DIM
    k_scale = RET_QK_DIM ** -0.5
    row = x_ref.shape[0]

    def rotary(tq):
        t1, t2 = tq[:, :half], tq[:, half:]
        return jnp.concatenate([t1 * cos - t2 * sin, t1 * sin + t2 * cos], axis=-1)

    for h in range(heads):
        cq = h * RET_QK_DIM
        cv = 2 * qk_w + h * RET_V_DIM
        q = rotary(_dot(u, w_ref[:, cq:cq + RET_QK_DIM])).astype(BF16)
        k = rotary(_dot(u, w_ref[:, qk_w + cq:qk_w + cq + RET_QK_DIM]) * k_scale)
        v = _dot(u, w_ref[:, cv:cv + RET_V_DIM]).astype(BF16)
        gt = _dot(u, w_ref[:, v_w + cv:v_w + cv + RET_V_DIM])
        gn = gn_ref[:, h * RET_V_DIM:(h + 1) * RET_V_DIM]
        for r0 in range(0, row, blk):
            qs, ks, vs = q[r0:r0 + blk], k[r0:r0 + blk], v[r0:r0 + blk]
            s = lax.dot_general(qs, ks.astype(BF16), (((1,), (1,)), ((), ())),
                                preferred_element_type=F32)
            o = _dot((s * dm_ref[h]).astype(BF16), vs)
            st = state_ref[h]
            o = o + _dot(qs, st.astype(BF16)) * qd_ref[h]
            kd = (ks * kd_ref[h]).astype(BF16)
            state_ref[h] = st * cd_ref[h] + lax.dot_general(
                kd, vs, (((0,), (0,)), ((), ())), preferred_element_type=F32)
            mean = jnp.mean(o, axis=-1, keepdims=True)
            cen = o - mean
            var = jnp.mean(jnp.square(cen), axis=-1, keepdims=True)
            on = cen * lax.rsqrt(var + NORM_EPS) * gn
            y_ref[r0:r0 + blk, h * RET_V_DIM:(h + 1) * RET_V_DIM] = (
                jax.nn.silu(gt[r0:r0 + blk]) * on).astype(y_ref.dtype)


def _ret_call(x, g, w, cos, sin, gn_w, dmask, qdec, kdec, cdec, batch, seq, heads, row, blk):
    t, dm = x.shape
    v_w = heads * RET_V_DIM
    nb = seq // row
    kern = functools.partial(_ret_kernel, heads=heads, blk=blk)
    rows = lambda width: pl.BlockSpec((row, width), lambda b, i: (b * nb + i, 0))
    tab = pl.BlockSpec((row, RET_QK_DIM // 2), lambda b, i: (i, 0))
    return pl.pallas_call(
        kern,
        grid=(batch, nb),
        in_specs=[rows(dm), _resident((1, dm)), _resident(w.shape), tab, tab,
                  _resident((1, v_w)), _resident(dmask.shape), _resident(qdec.shape),
                  _resident(kdec.shape), _resident(cdec.shape)],
        out_specs=rows(v_w),
        out_shape=jax.ShapeDtypeStruct((t, v_w), BF16),
        scratch_shapes=[pltpu.VMEM((heads, RET_QK_DIM, RET_V_DIM), F32)],
        compiler_params=_cparams(("parallel", "arbitrary")),
        name="ret_mix",
    )(x, g.reshape(1, dm), w, cos, sin, gn_w.reshape(1, v_w), dmask, qdec, kdec, cdec)


def _ret_decays(heads, blk):
    log_gamma = jnp.log1p(-jnp.exp2(-5.0 - jnp.arange(heads, dtype=F32)))
    pos = jnp.arange(blk, dtype=F32)
    diff = pos[:, None] - pos[None, :]
    cn = (jnp.arange(blk) // CHUNK)[:, None]
    cm = (jnp.arange(blk) // CHUNK)[None, :]
    expo = jnp.where(cn == cm, jnp.abs(diff), diff)
    dmask = jnp.where((cm <= cn)[None], jnp.exp(log_gamma[:, None, None] * expo[None]), 0.0)
    qdec = jnp.exp((pos[None, :] + 1.0) * log_gamma[:, None])[..., None]
    kdec = jnp.exp((blk - 1.0 - pos)[None, :] * log_gamma[:, None])[..., None]
    cdec = jnp.exp(blk * log_gamma)[:, None, None]
    return dmask, qdec, kdec, cdec


def _rotary_tables(seq):
    inv_freq = 1.0 / (ROPE_BASE ** jnp.linspace(0.0, 1.0, RET_QK_DIM // 2, dtype=F32))
    ang = jnp.arange(seq, dtype=F32)[:, None] * inv_freq[None, :]
    return jnp.cos(ang), jnp.sin(ang)


def _ret_ffn_kernel(x_ref, y_ref, w_ref, fg_ref, wg_ref, wu_ref, wd_ref, fin_ref, o_ref, acc_ref,
                    *, hc):
    x1 = x_ref[...] + _dot(y_ref[...], w_ref[...])
    out = _ffn_rows(x1, fg_ref, wg_ref, wu_ref, wd_ref, acc_ref, hc)
    o_ref[...] = _rms(out, fin_ref[...])


def _ret_ffn_call(x, y, w, fg, wg, wu, wd, fin, row, hc):
    t, dm = x.shape
    hidden = wg.shape[1]
    kdim = y.shape[1]
    tile, weights = _ffn_specs(row, dm, hidden)
    return pl.pallas_call(
        functools.partial(_ret_ffn_kernel, hc=hc),
        grid=(t // row,),
        in_specs=[tile, pl.BlockSpec((row, kdim), lambda i: (i, 0)), _resident((kdim, dm))]
        + weights + [_resident((1, dm))],
        out_specs=tile,
        out_shape=jax.ShapeDtypeStruct((t, dm), F32),
        scratch_shapes=[pltpu.VMEM((row, dm), F32)],
        compiler_params=_cparams(("parallel",)),
        name="ret_ffn",
    )(x, y, w, fg.reshape(1, dm), wg, wu, wd, fin.reshape(1, dm))


def _ret_layer(x, batch, seq, norm_w, w_qkvg, gn_w, w_o, ffn_g, ffn_wg, ffn_wu, ffn_wd, fin,
               row, blk, hc):
    dm = x.shape[1]
    heads = dm // RET_QK_DIM
    cos, sin = _rotary_tables(seq)
    dmask, qdec, kdec, cdec = _ret_decays(heads, blk)
    y = _ret_call(x, norm_w, w_qkvg.astype(BF16), cos, sin, gn_w, dmask, qdec, kdec, cdec,
                  batch, seq, heads, row, blk)
    return _ret_ffn_call(x, y, w_o.astype(BF16), ffn_g, ffn_wg.astype(BF16), ffn_wu.astype(BF16),
                         ffn_wd.astype(BF16), fin, row, hc)


def kernel(x, s5_norm, s5_lambda_re, s5_lambda_im, s5_log_step, s5_b_re, s5_b_im, s5_c_re, s5_c_im,
           s5_d, s5_glu_w, s5_glu_b, ret_norm, ret_w_qkvg, ret_gn_w, ret_w_o, ffn_norm, ffn_w_gate,
           ffn_w_up, ffn_w_down, final_norm):
    batch, seq, dm = x.shape
    row, blk, hc = _tiles(seq)
    assert ffn_norm.shape[0] == 2 and s5_norm.shape[0] == 1 and ret_norm.shape[0] == 1
    h = x.reshape(batch * seq, dm)
    h = _s5_layer(h, batch, seq, s5_norm[0], s5_lambda_re[0], s5_lambda_im[0], s5_log_step[0],
                  s5_b_re[0], s5_b_im[0], s5_c_re[0], s5_c_im[0], s5_d[0], s5_glu_w[0],
                  s5_glu_b[0], ffn_norm[0], ffn_w_gate[0], ffn_w_up[0], ffn_w_down[0], row, hc)
    h = _ret_layer(h, batch, seq, ret_norm[0], ret_w_qkvg[0], ret_gn_w[0], ret_w_o[0],
                   ffn_norm[1], ffn_w_gate[1], ffn_w_up[1], ffn_w_down[1], final_norm,
                   row, blk, hc)
    return h.reshape(batch, seq, dm)
```

```python
import functools

import jax
import jax.numpy as jnp
from jax import lax
from jax.experimental import pallas as pl
from jax.experimental.pallas import tpu as pltpu

F32 = jnp.float32
BF16 = jnp.bfloat16

NORM_EPS = 1e-6
CHUNK = 64
S5_GROUP = 16
S5_STATE = 64
S5_SUB = 16
S5_GB = 16
RET_QK_DIM = 256
RET_V_DIM = 512
ROPE_BASE = 10000.0

V7X_VMEM_BYTES = 64 * 1024 * 1024
V7X_LANES = 128
VMEM_LIMIT = 56 * 1024 * 1024


def _tiles(seq):
    row = min(512, seq)
    ret = min(256, seq)
    hc = 256
    assert seq % row == 0 and row % ret == 0 and ret % CHUNK == 0
    return row, ret, hc


def _cparams(sem):
    return pltpu.CompilerParams(dimension_semantics=sem, vmem_limit_bytes=VMEM_LIMIT)


def _resident(shape):
    nd = len(shape)
    return pl.BlockSpec(shape, lambda *_: (0,) * nd, pipeline_mode=pl.Buffered(1))


def _rms(xf, g):
    ms = jnp.mean(jnp.square(xf), axis=-1, keepdims=True)
    return xf * lax.rsqrt(ms + NORM_EPS) * g


def _dot(a, b):
    return jnp.dot(a, b, preferred_element_type=F32)


def _bdot(a, b):
    return lax.dot_general(a, b, (((2,), (1,)), ((0,), (0,))), preferred_element_type=F32)


def _s5_pre_kernel(x_hbm, g_ref, nn_ref, are_ref, aim_ref, zt_ref, er_ref, ei_ref, xs, sem):
    groups = zt_ref.shape[1]
    nseg = zt_ref.shape[3]
    i = pl.program_id(0)
    steps = pl.num_programs(0)

    def gathers(step, slot):
        return [pltpu.make_async_copy(x_hbm.at[:, step * S5_SUB + t, :], xs.at[slot, t],
                                      sem.at[slot]) for t in range(S5_SUB)]

    @pl.when(i == 0)
    def _():
        for c in gathers(0, 0):
            c.start()

    @pl.when(i + 1 < steps)
    def _():
        for c in gathers(i + 1, (i + 1) % 2):
            c.start()

    slot = i % 2
    for c in gathers(i, slot):
        c.wait()
    g = g_ref[...]
    for t in range(S5_SUB):
        ut = _rms(xs[slot, t], g)
        zt_ref[0, :, t * S5_GROUP:(t + 1) * S5_GROUP, :] = (
            ut.T.reshape(groups, S5_GROUP, nseg).astype(zt_ref.dtype))
    s = _bdot(nn_ref[...], zt_ref[0])
    sr = s[:, :S5_STATE, :].reshape(groups * S5_STATE, nseg)
    si = s[:, S5_STATE:, :].reshape(groups * S5_STATE, nseg)

    @pl.when(i == 0)
    def _():
        er_ref[...] = sr
        ei_ref[...] = si

    @pl.when(i > 0)
    def _():
        er, ei = er_ref[...], ei_ref[...]
        ar, ai = are_ref[...], aim_ref[...]
        er_ref[...] = ar * er - ai * ei + sr
        ei_ref[...] = ar * ei + ai * er + si


def _s5_pre_call(x3, g, nn, a_re, a_im):
    nseg, sl, dm = x3.shape
    groups = dm // S5_GROUP
    steps = sl // S5_SUB
    width = S5_SUB * S5_GROUP
    rows = groups * S5_STATE
    return pl.pallas_call(
        _s5_pre_kernel,
        grid=(steps,),
        in_specs=[pl.BlockSpec(memory_space=pl.ANY), _resident((1, dm)),
                  _resident(nn.shape), _resident((rows, nseg)), _resident((rows, nseg))],
        out_specs=[pl.BlockSpec((1, groups, width, nseg), lambda i: (i, 0, 0, 0)),
                   pl.BlockSpec((rows, nseg), lambda i: (0, 0)),
                   pl.BlockSpec((rows, nseg), lambda i: (0, 0))],
        out_shape=[jax.ShapeDtypeStruct((steps, groups, width, nseg), BF16),
                   jax.ShapeDtypeStruct((rows, nseg), F32),
                   jax.ShapeDtypeStruct((rows, nseg), F32)],
        scratch_shapes=[pltpu.VMEM((2, S5_SUB, nseg, dm), F32), pltpu.SemaphoreType.DMA((2,))],
        compiler_params=_cparams(("arbitrary",)),
        name="s5_pre",
    )(x3, g.reshape(1, dm), nn, a_re, a_im)


def _s5_stitch_kernel(er_ref, ei_ref, pr_ref, pi_ref, xr_ref, xi_ref, *, nsb):
    er, ei = er_ref[...], ei_ref[...]
    lane = lax.broadcasted_iota(jnp.int32, er.shape, 1) % nsb
    d, k = 1, 0
    while d < nsb:
        sr, si = pltpu.roll(er, d, axis=1), pltpu.roll(ei, d, axis=1)
        pr, pi = pr_ref[:, k:k + 1], pi_ref[:, k:k + 1]
        ok = lane >= d
        er, ei = (er + jnp.where(ok, pr * sr - pi * si, 0.0),
                  ei + jnp.where(ok, pr * si + pi * sr, 0.0))
        d, k = 2 * d, k + 1
    ok = lane >= 1
    xr_ref[...] = jnp.where(ok, pltpu.roll(er, 1, axis=1), 0.0)
    xi_ref[...] = jnp.where(ok, pltpu.roll(ei, 1, axis=1), 0.0)


def _s5_stitch_call(er, ei, p_re, p_im, nsb):
    shp = jax.ShapeDtypeStruct(er.shape, F32)
    return pl.pallas_call(
        functools.partial(_s5_stitch_kernel, nsb=nsb),
        out_shape=[shp, shp],
        compiler_params=pltpu.CompilerParams(vmem_limit_bytes=VMEM_LIMIT),
        name="s5_stitch",
    )(er, ei, p_re, p_im)


def _s5_core_kernel(zt_ref, tt_ref, nn_ref, mm_ref, are_ref, aim_ref, x0r_ref, x0i_ref, y_hbm,
                    xr_scr, xi_scr, ybuf, sem):
    pair, gb, _, nseg = zt_ref.shape
    cols = gb * S5_GROUP
    j, k = pl.program_id(0), pl.program_id(1)
    n = j * pl.num_programs(1) + k
    last = pl.num_programs(0) * pl.num_programs(1) - 1
    slot = n % 2

    def scatters(sl):
        return [pltpu.make_async_copy(
            ybuf.at[sl, s, t],
            y_hbm.at[:, (k * pair + s) * S5_SUB + t, pl.ds(j * cols, cols)],
            sem.at[sl]) for s in range(pair) for t in range(S5_SUB)]

    @pl.when(k == 0)
    def _():
        xr_scr[...] = x0r_ref[...]
        xi_scr[...] = x0i_ref[...]

    @pl.when(n >= 2)
    def _():
        for c in scatters(slot):
            c.wait()

    zt = jnp.concatenate([zt_ref[s] for s in range(pair)], axis=-1)
    s_all = _bdot(nn_ref[...], zt)
    ar, ai = are_ref[...], aim_ref[...]
    xr, xi = xr_scr[...], xi_scr[...]
    starts = []
    for s in range(pair):
        starts.append(jnp.concatenate([xr.reshape(gb, S5_STATE, nseg),
                                       xi.reshape(gb, S5_STATE, nseg)], axis=1))
        inc = s_all[:, :, s * nseg:(s + 1) * nseg]
        sr = inc[:, :S5_STATE, :].reshape(gb * S5_STATE, nseg)
        si = inc[:, S5_STATE:, :].reshape(gb * S5_STATE, nseg)
        xr, xi = ar * xr - ai * xi + sr, ar * xi + ai * xr + si
    xr_scr[...] = xr
    xi_scr[...] = xi
    xprev = jnp.concatenate(starts, axis=-1).astype(zt.dtype)
    y = _bdot(tt_ref[...], zt) + _bdot(mm_ref[...], xprev)
    for t in range(S5_SUB):
        blk = y[:, t * S5_GROUP:(t + 1) * S5_GROUP, :].reshape(cols, pair * nseg)
        bt = blk.T
        for s in range(pair):
            ybuf[slot, s, t] = bt[s * nseg:(s + 1) * nseg]
    for c in scatters(slot):
        c.start()

    @pl.when(n == last)
    def _():
        for c in scatters(slot):
            c.wait()

    @pl.when(jnp.logical_and(n == last, n >= 1))
    def _():
        for c in scatters(1 - slot):
            c.wait()


def _s5_core_call(zt, tt, nn, mm, a_re, a_im, x0r, x0i, dm):
    steps, groups, width, nseg = zt.shape
    gb = min(S5_GB, groups)
    pair = 2 if steps % 2 == 0 else 1
    rows = gb * S5_STATE
    per_gb = lambda shape: pl.BlockSpec(shape, lambda j, k: (j,) + (0,) * (len(shape) - 1))
    return pl.pallas_call(
        _s5_core_kernel,
        grid=(groups // gb, steps // pair),
        in_specs=[pl.BlockSpec((pair, gb, width, nseg), lambda j, k: (k, j, 0, 0)),
                  per_gb((gb, width, width)), per_gb((gb, 2 * S5_STATE, width)),
                  per_gb((gb, width, 2 * S5_STATE)),
                  per_gb((rows, nseg)), per_gb((rows, nseg)), per_gb((rows, nseg)),
                  per_gb((rows, nseg))],
        out_specs=pl.BlockSpec(memory_space=pl.ANY),
        out_shape=jax.ShapeDtypeStruct((nseg, steps * S5_SUB, dm), F32),
        scratch_shapes=[pltpu.VMEM((rows, nseg), F32), pltpu.VMEM((rows, nseg), F32),
                        pltpu.VMEM((2, pair, S5_SUB, nseg, gb * S5_GROUP), F32),
                        pltpu.SemaphoreType.DMA((2,))],
        compiler_params=_cparams(("arbitrary", "arbitrary")),
        name="s5_core",
    )(zt, tt, nn, mm, a_re, a_im, x0r, x0i)


def _ffn_rows(x, g_ref, wg_ref, wu_ref, wd_ref, acc_ref, hc):
    u = _rms(x, g_ref[...]).astype(BF16)
    hidden = wg_ref.shape[1]
    for j in range(hidden // hc):
        cols = slice(j * hc, (j + 1) * hc)
        gt = _dot(u, wg_ref[:, cols])
        up = _dot(u, wu_ref[:, cols])
        h = (jax.nn.silu(gt) * up).astype(BF16)
        part = _dot(h, wd_ref[cols, :])
        if j == 0:
            acc_ref[...] = part
        else:
            acc_ref[...] += part
    return x + acc_ref[...]


def _ffn_specs(row, dm, hidden):
    tile = pl.BlockSpec((row, dm), lambda i: (i, 0))
    weights = [_resident((1, dm)), _resident((dm, hidden)), _resident((dm, hidden)),
               _resident((hidden, dm))]
    return tile, weights


def _s5_ffn_kernel(x_ref, y_ref, g_ref, d_ref, w_ref, b_ref, fg_ref, wg_ref, wu_ref, wd_ref,
                   o_ref, acc_ref, *, hc):
    x = x_ref[...]
    u = _rms(x, g_ref[...])
    y = jax.nn.gelu(y_ref[...] + d_ref[...] * u)
    gate = jax.nn.sigmoid(_dot(y.astype(BF16), w_ref[...]) + b_ref[...])
    o_ref[...] = _ffn_rows(x + y * gate, fg_ref, wg_ref, wu_ref, wd_ref, acc_ref, hc)


def _s5_ffn_call(x, y, g, d, w, b, fg, wg, wu, wd, row, hc):
    t, dm = x.shape
    hidden = wg.shape[1]
    tile, weights = _ffn_specs(row, dm, hidden)
    return pl.pallas_call(
        functools.partial(_s5_ffn_kernel, hc=hc),
        grid=(t // row,),
        in_specs=[tile, tile, _resident((1, dm)), _resident((1, dm)), _resident((dm, dm)),
                  _resident((1, dm))] + weights,
        out_specs=tile,
        out_shape=jax.ShapeDtypeStruct((t, dm), F32),
        scratch_shapes=[pltpu.VMEM((row, dm), F32)],
        compiler_params=_cparams(("parallel",)),
        name="s5_ffn",
    )(x, y, g.reshape(1, dm), d.reshape(1, dm), w, b.reshape(1, dm), fg.reshape(1, dm), wg, wu, wd)


def _s5_tab_kernel(p_ref, q_ref, c1_ref, c2_ref, b1_ref, b2_ref, bs_ref, tt_ref, nn_ref, mm_ref):
    sub = S5_SUB
    width = sub * S5_GROUP
    lane = lax.broadcasted_iota(jnp.int32, (S5_GROUP, width), 1)
    sgn = jnp.where(lax.broadcasted_iota(jnp.int32, (1, 2 * S5_STATE), 1) < S5_STATE, 1.0, -1.0)
    for gi in range(p_ref.shape[0]):
        p, q = p_ref[gi], q_ref[gi]
        c1, c2, b1, b2 = c1_ref[gi], c2_ref[gi], b1_ref[gi], b2_ref[gi]
        cl = [c1 * p[k:k + 1] + c2 * q[k:k + 1] for k in range(sub + 1)]
        mm_ref[gi] = (jnp.concatenate(cl[1:], axis=0) * sgn).astype(mm_ref.dtype)
        r = jnp.dot(bs_ref[gi], jnp.concatenate(cl[:sub], axis=0).T,
                    precision=lax.Precision.HIGHEST, preferred_element_type=F32)
        rows = [r] + [jnp.where(lane >= S5_GROUP * s, pltpu.roll(r, S5_GROUP * s, axis=1), 0.0)
                      for s in range(1, sub)]
        tt_ref[gi] = jnp.concatenate(rows, axis=0).T.astype(tt_ref.dtype)
        nt = [b1 * p[sub - 1 - s:sub - s] + b2 * q[sub - 1 - s:sub - s] for s in range(sub)]
        nn_ref[gi] = jnp.concatenate(nt, axis=0).T.astype(nn_ref.dtype)


def _s5_tab_call(p, q, c1, c2, b1, b2, bs):
    groups = p.shape[0]
    gt = 4 if groups % 4 == 0 else 1
    width = S5_SUB * S5_GROUP
    st2 = 2 * S5_STATE
    blk = lambda a: pl.BlockSpec((gt,) + a.shape[1:], lambda i: (i, 0, 0))
    out = lambda r, c: pl.BlockSpec((gt, r, c), lambda i: (i, 0, 0))
    return pl.pallas_call(
        _s5_tab_kernel,
        grid=(groups // gt,),
        in_specs=[blk(a) for a in (p, q, c1, c2, b1, b2, bs)],
        out_specs=[out(width, width), out(st2, width), out(width, st2)],
        out_shape=[jax.ShapeDtypeStruct((groups, width, width), BF16),
                   jax.ShapeDtypeStruct((groups, st2, width), BF16),
                   jax.ShapeDtypeStruct((groups, width, st2), BF16)],
        compiler_params=_cparams(("parallel",)),
        name="s5_tab",
    )(p, q, c1, c2, b1, b2, bs)


def _s5_tables(lam_re, lam_im, log_step, b_re, b_im, c_re, c_im, seg_len, nsb, nseg):
    sub = S5_SUB
    step = jnp.exp(log_step)[:, None]
    dre, dim = lam_re * step, lam_im * step

    def power(k):
        kk = jnp.asarray(k, F32)[None, :, None]
        mag = jnp.exp(kk * dre[:, None, :])
        return mag * jnp.cos(kk * dim[:, None, :]), mag * jnp.sin(kk * dim[:, None, :])

    pw_re, pw_im = power(jnp.arange(sub + 1))
    den = lam_re * lam_re + lam_im * lam_im
    nr, ni = pw_re[:, 1] - 1.0, pw_im[:, 1]
    cf_re = ((nr * lam_re + ni * lam_im) / den)[:, None, :]
    cf_im = ((ni * lam_re - nr * lam_im) / den)[:, None, :]
    bt_re, bt_im = b_re.transpose(0, 2, 1), b_im.transpose(0, 2, 1)
    bb_re = cf_re * bt_re - cf_im * bt_im
    bb_im = cf_re * bt_im + cf_im * bt_re
    cat = lambda a, b: jnp.concatenate([a, b], axis=-1)
    tt, nn, mm = _s5_tab_call(cat(pw_re, pw_im), cat(pw_im, pw_re), cat(c_re, c_re),
                              cat(-c_im, c_im), cat(bb_re, bb_re), cat(-bb_im, bb_im),
                              cat(bb_re, -bb_im))
    a_re = jnp.broadcast_to(pw_re[:, sub].reshape(-1, 1), (pw_re[:, sub].size, nseg))
    a_im = jnp.broadcast_to(pw_im[:, sub].reshape(-1, 1), (pw_im[:, sub].size, nseg))
    nd = max(1, (nsb - 1).bit_length())
    p_re, p_im = power(seg_len * (2 ** jnp.arange(nd)))
    p_re = p_re.transpose(0, 2, 1).reshape(-1, nd)
    p_im = p_im.transpose(0, 2, 1).reshape(-1, nd)
    return tt, nn, mm, a_re, a_im, p_re, p_im


def _s5_layer(x, batch, seq, norm_w, lam_re, lam_im, log_step, b_re, b_im, c_re, c_im, d, glu_w,
              glu_b, ffn_g, ffn_wg, ffn_wu, ffn_wd, row, hc):
    t, dm = x.shape
    nseg = V7X_LANES
    seg_len = t // nseg
    assert t % nseg == 0 and seg_len % S5_SUB == 0 and seq % seg_len == 0
    nsb = seq // seg_len
    tt, nn, mm, a_re, a_im, p_re, p_im = _s5_tables(
        lam_re, lam_im, log_step, b_re, b_im, c_re, c_im, seg_len, nsb, nseg)
    zt, er, ei = _s5_pre_call(x.reshape(nseg, seg_len, dm), norm_w, nn, a_re, a_im)
    x0r, x0i = _s5_stitch_call(er, ei, p_re, p_im, nsb)
    y = _s5_core_call(zt, tt, nn, mm, a_re, a_im, x0r, x0i, dm).reshape(t, dm)
    return _s5_ffn_call(x, y, norm_w, d.reshape(-1), glu_w.astype(BF16), glu_b, ffn_g,
                        ffn_wg.astype(BF16), ffn_wu.astype(BF16), ffn_wd.astype(BF16), row, hc)


def _ret_kernel(x_ref, g_ref, w_ref, cos_ref, sin_ref, gn_ref, dm_ref, qd_ref, kd_ref, cd_ref,
                y_ref, state_ref, *, heads, blk):
    @pl.when(pl.program_id(1) == 0)
    def _():
        state_ref[...] = jnp.zeros_like(state_ref)

    u = _rms(x_ref[...], g_ref[...]).astype(BF16)
    cos = cos_ref[...]
    sin = sin_ref[...]
    half = RET_QK_DIM // 2
    qk_w = heads * RET_QK_DIM
    v_w = heads * RET_V_DIM
    k_scale = RET_QK_DIM ** -0.5
    row = x_ref.shape[0]

    def rotary(tq):
        t1, t2 = tq[:, :half], tq[:, half:]
        return jnp.concatenate([t1 * cos - t2 * sin, t1 * sin + t2 * cos], axis=-1)

    for h in range(heads):
        cq = h * RET_QK_DIM
        cv = 2 * qk_w + h * RET_V_DIM
        q = rotary(_dot(u, w_ref[:, cq:cq + RET_QK_DIM])).astype(BF16)
        k = rotary(_dot(u, w_ref[:, qk_w + cq:qk_w + cq + RET_QK_DIM]) * k_scale)
        v = _dot(u, w_ref[:, cv:cv + RET_V_DIM]).astype(BF16)
        gt = _dot(u, w_ref[:, v_w + cv:v_w + cv + RET_V_DIM])
        gn = gn_ref[:, h * RET_V_DIM:(h + 1) * RET_V_DIM]
        for r0 in range(0, row, blk):
            qs, ks, vs = q[r0:r0 + blk], k[r0:r0 + blk], v[r0:r0 + blk]
            s = lax.dot_general(qs, ks.astype(BF16), (((1,), (1,)), ((), ())),
                                preferred_element_type=F32)
            o = _dot((s * dm_ref[h]).astype(BF16), vs)
            st = state_ref[h]
            o = o + _dot(qs, st.astype(BF16)) * qd_ref[h]
            kd = (ks * kd_ref[h]).astype(BF16)
            state_ref[h] = st * cd_ref[h] + lax.dot_general(
                kd, vs, (((0,), (0,)), ((), ())), preferred_element_type=F32)
            mean = jnp.mean(o, axis=-1, keepdims=True)
            cen = o - mean
            var = jnp.mean(jnp.square(cen), axis=-1, keepdims=True)
            on = cen * lax.rsqrt(var + NORM_EPS) * gn
            y_ref[r0:r0 + blk, h * RET_V_DIM:(h + 1) * RET_V_DIM] = (
                jax.nn.silu(gt[r0:r0 + blk]) * on).astype(y_ref.dtype)


def _ret_call(x, g, w, cos, sin, gn_w, dmask, qdec, kdec, cdec, batch, seq, heads, row, blk):
    t, dm = x.shape
    v_w = heads * RET_V_DIM
    nb = seq // row
    kern = functools.partial(_ret_kernel, heads=heads, blk=blk)
    rows = lambda width: pl.BlockSpec((row, width), lambda b, i: (b * nb + i, 0))
    tab = pl.BlockSpec((row, RET_QK_DIM // 2), lambda b, i: (i, 0))
    return pl.pallas_call(
        kern,
        grid=(batch, nb),
        in_specs=[rows(dm), _resident((1, dm)), _resident(w.shape), tab, tab,
                  _resident((1, v_w)), _resident(dmask.shape), _resident(qdec.shape),
                  _resident(kdec.shape), _resident(cdec.shape)],
        out_specs=rows(v_w),
        out_shape=jax.ShapeDtypeStruct((t, v_w), BF16),
        scratch_shapes=[pltpu.VMEM((heads, RET_QK_DIM, RET_V_DIM), F32)],
        compiler_params=_cparams(("parallel", "arbitrary")),
        name="ret_mix",
    )(x, g.reshape(1, dm), w, cos, sin, gn_w.reshape(1, v_w), dmask, qdec, kdec, cdec)


def _ret_decays(heads, blk):
    log_gamma = jnp.log1p(-jnp.exp2(-5.0 - jnp.arange(heads, dtype=F32)))
    pos = jnp.arange(blk, dtype=F32)
    diff = pos[:, None] - pos[None, :]
    cn = (jnp.arange(blk) // CHUNK)[:, None]
    cm = (jnp.arange(blk) // CHUNK)[None, :]
    expo = jnp.where(cn == cm, jnp.abs(diff), diff)
    dmask = jnp.where((cm <= cn)[None], jnp.exp(log_gamma[:, None, None] * expo[None]), 0.0)
    qdec = jnp.exp((pos[None, :] + 1.0) * log_gamma[:, None])[..., None]
    kdec = jnp.exp((blk - 1.0 - pos)[None, :] * log_gamma[:, None])[..., None]
    cdec = jnp.exp(blk * log_gamma)[:, None, None]
    return dmask, qdec, kdec, cdec


def _rotary_tables(seq):
    inv_freq = 1.0 / (ROPE_BASE ** jnp.linspace(0.0, 1.0, RET_QK_DIM // 2, dtype=F32))
    ang = jnp.arange(seq, dtype=F32)[:, None] * inv_freq[None, :]
    return jnp.cos(ang), jnp.sin(ang)


def _ret_ffn_kernel(x_ref, y_ref, w_ref, fg_ref, wg_ref, wu_ref, wd_ref, fin_ref, o_ref, acc_ref,
                    *, hc):
    x1 = x_ref[...] + _dot(y_ref[...], w_ref[...])
    out = _ffn_rows(x1, fg_ref, wg_ref, wu_ref, wd_ref, acc_ref, hc)
    o_ref[...] = _rms(out, fin_ref[...])


def _ret_ffn_call(x, y, w, fg, wg, wu, wd, fin, row, hc):
    t, dm = x.shape
    hidden = wg.shape[1]
    kdim = y.shape[1]
    tile, weights = _ffn_specs(row, dm, hidden)
    return pl.pallas_call(
        functools.partial(_ret_ffn_kernel, hc=hc),
        grid=(t // row,),
        in_specs=[tile, pl.BlockSpec((row, kdim), lambda i: (i, 0)), _resident((kdim, dm))]
        + weights + [_resident((1, dm))],
        out_specs=tile,
        out_shape=jax.ShapeDtypeStruct((t, dm), F32),
        scratch_shapes=[pltpu.VMEM((row, dm), F32)],
        compiler_params=_cparams(("parallel",)),
        name="ret_ffn",
    )(x, y, w, fg.reshape(1, dm), wg, wu, wd, fin.reshape(1, dm))


def _ret_layer(x, batch, seq, norm_w, w_qkvg, gn_w, w_o, ffn_g, ffn_wg, ffn_wu, ffn_wd, fin,
               row, blk, hc):
    dm = x.shape[1]
    heads = dm // RET_QK_DIM
    cos, sin = _rotary_tables(seq)
    dmask, qdec, kdec, cdec = _ret_decays(heads, blk)
    y = _ret_call(x, norm_w, w_qkvg.astype(BF16), cos, sin, gn_w, dmask, qdec, kdec, cdec,
                  batch, seq, heads, row, blk)
    return _ret_ffn_call(x, y, w_o.astype(BF16), ffn_g, ffn_wg.astype(BF16), ffn_wu.astype(BF16),
                         ffn_wd.astype(BF16), fin, row, hc)


def kernel(x, s5_norm, s5_lambda_re, s5_lambda_im, s5_log_step, s5_b_re, s5_b_im, s5_c_re, s5_c_im,
           s5_d, s5_glu_w, s5_glu_b, ret_norm, ret_w_qkvg, ret_gn_w, ret_w_o, ffn_norm, ffn_w_gate,
           ffn_w_up, ffn_w_down, final_norm):
    batch, seq, dm = x.shape
    row, blk, hc = _tiles(seq)
    assert ffn_norm.shape[0] == 2 and s5_norm.shape[0] == 1 and ret_norm.shape[0] == 1
    h = x.reshape(batch * seq, dm)
    h = _s5_layer(h, batch, seq, s5_norm[0], s5_lambda_re[0], s5_lambda_im[0], s5_log_step[0],
                  s5_b_re[0], s5_b_im[0], s5_c_re[0], s5_c_im[0], s5_d[0], s5_glu_w[0],
                  s5_glu_b[0], ffn_norm[0], ffn_w_gate[0], ffn_w_up[0], ffn_w_down[0], row, hc)
    h = _ret_layer(h, batch, seq, ret_norm[0], ret_w_qkvg[0], ret_gn_w[0], ret_w_o[0],
                   ffn_norm[1], ffn_w_gate[1], ffn_w_up[1], ffn_w_down[1], final_norm,
                   row, blk, hc)
    return h.reshape(batch, seq, dm)
```

```python
import functools

import jax
import jax.numpy as jnp
from jax import lax
from jax.experimental import pallas as pl
from jax.experimental.pallas import tpu as pltpu

F32 = jnp.float32
BF16 = jnp.bfloat16

NORM_EPS = 1e-6
CHUNK = 64
S5_GROUP = 16
S5_STATE = 64
S5_SUB = 16
S5_GB = 16
RET_QK_DIM = 256
RET_V_DIM = 512
ROPE_BASE = 10000.0

V7X_VMEM_BYTES = 64 * 1024 * 1024
V7X_LANES = 128
VMEM_LIMIT = 56 * 1024 * 1024


def _tiles(seq):
    row = min(1024, seq)
    ret = min(256, seq)
    hc = 256
    assert seq % row == 0 and row % ret == 0 and ret % CHUNK == 0
    return row, ret, hc


def _cparams(sem):
    return pltpu.CompilerParams(dimension_semantics=sem, vmem_limit_bytes=VMEM_LIMIT)


def _resident(shape):
    nd = len(shape)
    return pl.BlockSpec(shape, lambda *_: (0,) * nd, pipeline_mode=pl.Buffered(1))


def _rms(xf, g):
    ms = jnp.mean(jnp.square(xf), axis=-1, keepdims=True)
    return xf * lax.rsqrt(ms + NORM_EPS) * g


def _dot(a, b):
    return jnp.dot(a, b, preferred_element_type=F32)


def _bdot(a, b):
    return lax.dot_general(a, b, (((2,), (1,)), ((0,), (0,))), preferred_element_type=F32)


def _s5_pre_kernel(x_hbm, g_ref, nn_ref, are_ref, aim_ref, zt_ref, er_ref, ei_ref, xs, sem):
    groups = zt_ref.shape[1]
    nseg = zt_ref.shape[3]
    i = pl.program_id(0)
    steps = pl.num_programs(0)

    def gathers(step, slot):
        return [pltpu.make_async_copy(x_hbm.at[:, step * S5_SUB + t, :], xs.at[slot, t],
                                      sem.at[slot]) for t in range(S5_SUB)]

    @pl.when(i == 0)
    def _():
        for c in gathers(0, 0):
            c.start()

    @pl.when(i + 1 < steps)
    def _():
        for c in gathers(i + 1, (i + 1) % 2):
            c.start()

    slot = i % 2
    for c in gathers(i, slot):
        c.wait()
    g = g_ref[...]
    for t in range(S5_SUB):
        ut = _rms(xs[slot, t], g)
        zt_ref[0, :, t * S5_GROUP:(t + 1) * S5_GROUP, :] = (
            ut.T.reshape(groups, S5_GROUP, nseg).astype(zt_ref.dtype))
    s = _bdot(nn_ref[...], zt_ref[0])
    sr = s[:, :S5_STATE, :].reshape(groups * S5_STATE, nseg)
    si = s[:, S5_STATE:, :].reshape(groups * S5_STATE, nseg)

    @pl.when(i == 0)
    def _():
        er_ref[...] = sr
        ei_ref[...] = si

    @pl.when(i > 0)
    def _():
        er, ei = er_ref[...], ei_ref[...]
        ar, ai = are_ref[...], aim_ref[...]
        er_ref[...] = ar * er - ai * ei + sr
        ei_ref[...] = ar * ei + ai * er + si


def _s5_pre_call(x3, g, nn, a_re, a_im):
    nseg, sl, dm = x3.shape
    groups = dm // S5_GROUP
    steps = sl // S5_SUB
    width = S5_SUB * S5_GROUP
    rows = groups * S5_STATE
    return pl.pallas_call(
        _s5_pre_kernel,
        grid=(steps,),
        in_specs=[pl.BlockSpec(memory_space=pl.ANY), _resident((1, dm)),
                  _resident(nn.shape), _resident((rows, nseg)), _resident((rows, nseg))],
        out_specs=[pl.BlockSpec((1, groups, width, nseg), lambda i: (i, 0, 0, 0)),
                   pl.BlockSpec((rows, nseg), lambda i: (0, 0)),
                   pl.BlockSpec((rows, nseg), lambda i: (0, 0))],
        out_shape=[jax.ShapeDtypeStruct((steps, groups, width, nseg), BF16),
                   jax.ShapeDtypeStruct((rows, nseg), F32),
                   jax.ShapeDtypeStruct((rows, nseg), F32)],
        scratch_shapes=[pltpu.VMEM((2, S5_SUB, nseg, dm), F32), pltpu.SemaphoreType.DMA((2,))],
        compiler_params=_cparams(("arbitrary",)),
        name="s5_pre",
    )(x3, g.reshape(1, dm), nn, a_re, a_im)


def _s5_stitch_kernel(er_ref, ei_ref, pr_ref, pi_ref, xr_ref, xi_ref, *, nsb):
    er, ei = er_ref[...], ei_ref[...]
    lane = lax.broadcasted_iota(jnp.int32, er.shape, 1) % nsb
    d, k = 1, 0
    while d < nsb:
        sr, si = pltpu.roll(er, d, axis=1), pltpu.roll(ei, d, axis=1)
        pr, pi = pr_ref[:, k:k + 1], pi_ref[:, k:k + 1]
        ok = lane >= d
        er, ei = (er + jnp.where(ok, pr * sr - pi * si, 0.0),
                  ei + jnp.where(ok, pr * si + pi * sr, 0.0))
        d, k = 2 * d, k + 1
    ok = lane >= 1
    xr_ref[...] = jnp.where(ok, pltpu.roll(er, 1, axis=1), 0.0)
    xi_ref[...] = jnp.where(ok, pltpu.roll(ei, 1, axis=1), 0.0)


def _s5_stitch_call(er, ei, p_re, p_im, nsb):
    shp = jax.ShapeDtypeStruct(er.shape, F32)
    return pl.pallas_call(
        functools.partial(_s5_stitch_kernel, nsb=nsb),
        out_shape=[shp, shp],
        compiler_params=pltpu.CompilerParams(vmem_limit_bytes=VMEM_LIMIT),
        name="s5_stitch",
    )(er, ei, p_re, p_im)


def _s5_core_kernel(zt_ref, tt_ref, nn_ref, mm_ref, are_ref, aim_ref, x0r_ref, x0i_ref, y_hbm,
                    xr_scr, xi_scr, ybuf, sem):
    pair, gb, _, nseg = zt_ref.shape
    cols = gb * S5_GROUP
    j, k = pl.program_id(0), pl.program_id(1)
    n = j * pl.num_programs(1) + k
    last = pl.num_programs(0) * pl.num_programs(1) - 1
    slot = n % 2

    def scatters(sl):
        return [pltpu.make_async_copy(
            ybuf.at[sl, s, t],
            y_hbm.at[:, (k * pair + s) * S5_SUB + t, pl.ds(j * cols, cols)],
            sem.at[sl]) for s in range(pair) for t in range(S5_SUB)]

    @pl.when(k == 0)
    def _():
        xr_scr[...] = x0r_ref[...]
        xi_scr[...] = x0i_ref[...]

    @pl.when(n >= 2)
    def _():
        for c in scatters(slot):
            c.wait()

    zt = jnp.concatenate([zt_ref[s] for s in range(pair)], axis=-1)
    s_all = _bdot(nn_ref[...], zt)
    ar, ai = are_ref[...], aim_ref[...]
    xr, xi = xr_scr[...], xi_scr[...]
    starts = []
    for s in range(pair):
        starts.append(jnp.concatenate([xr.reshape(gb, S5_STATE, nseg),
                                       xi.reshape(gb, S5_STATE, nseg)], axis=1))
        inc = s_all[:, :, s * nseg:(s + 1) * nseg]
        sr = inc[:, :S5_STATE, :].reshape(gb * S5_STATE, nseg)
        si = inc[:, S5_STATE:, :].reshape(gb * S5_STATE, nseg)
        xr, xi = ar * xr - ai * xi + sr, ar * xi + ai * xr + si
    xr_scr[...] = xr
    xi_scr[...] = xi
    xprev = jnp.concatenate(starts, axis=-1).astype(zt.dtype)
    y = _bdot(tt_ref[...], zt) + _bdot(mm_ref[...], xprev)
    for t in range(S5_SUB):
        blk = y[:, t * S5_GROUP:(t + 1) * S5_GROUP, :].reshape(cols, pair * nseg)
        bt = blk.T
        for s in range(pair):
            ybuf[slot, s, t] = bt[s * nseg:(s + 1) * nseg]
    for c in scatters(slot):
        c.start()

    @pl.when(n == last)
    def _():
        for c in scatters(slot):
            c.wait()

    @pl.when(jnp.logical_and(n == last, n >= 1))
    def _():
        for c in scatters(1 - slot):
            c.wait()


def _s5_core_call(zt, tt, nn, mm, a_re, a_im, x0r, x0i, dm):
    steps, groups, width, nseg = zt.shape
    gb = min(S5_GB, groups)
    pair = 2 if steps % 2 == 0 else 1
    rows = gb * S5_STATE
    per_gb = lambda shape: pl.BlockSpec(shape, lambda j, k: (j,) + (0,) * (len(shape) - 1))
    return pl.pallas_call(
        _s5_core_kernel,
        grid=(groups // gb, steps // pair),
        in_specs=[pl.BlockSpec((pair, gb, width, nseg), lambda j, k: (k, j, 0, 0)),
                  per_gb((gb, width, width)), per_gb((gb, 2 * S5_STATE, width)),
                  per_gb((gb, width, 2 * S5_STATE)),
                  per_gb((rows, nseg)), per_gb((rows, nseg)), per_gb((rows, nseg)),
                  per_gb((rows, nseg))],
        out_specs=pl.BlockSpec(memory_space=pl.ANY),
        out_shape=jax.ShapeDtypeStruct((nseg, steps * S5_SUB, dm), F32),
        scratch_shapes=[pltpu.VMEM((rows, nseg), F32), pltpu.VMEM((rows, nseg), F32),
                        pltpu.VMEM((2, pair, S5_SUB, nseg, gb * S5_GROUP), F32),
                        pltpu.SemaphoreType.DMA((2,))],
        compiler_params=_cparams(("arbitrary", "arbitrary")),
        name="s5_core",
    )(zt, tt, nn, mm, a_re, a_im, x0r, x0i)


def _ffn_rows(x, g_ref, wg_ref, wu_ref, wd_ref, acc_ref, hc):
    u = _rms(x, g_ref[...]).astype(BF16)
    hidden = wg_ref.shape[1]
    for j in range(hidden // hc):
        cols = slice(j * hc, (j + 1) * hc)
        gt = _dot(u, wg_ref[:, cols])
        up = _dot(u, wu_ref[:, cols])
        h = (jax.nn.silu(gt) * up).astype(BF16)
        part = _dot(h, wd_ref[cols, :])
        if j == 0:
            acc_ref[...] = part
        else:
            acc_ref[...] += part
    return x + acc_ref[...]


def _ffn_specs(row, dm, hidden):
    tile = pl.BlockSpec((row, dm), lambda i: (i, 0))
    weights = [_resident((1, dm)), _resident((dm, hidden)), _resident((dm, hidden)),
               _resident((hidden, dm))]
    return tile, weights


def _s5_ffn_kernel(x_ref, y_ref, g_ref, d_ref, w_ref, b_ref, fg_ref, wg_ref, wu_ref, wd_ref,
                   o_ref, acc_ref, *, hc):
    x = x_ref[...]
    u = _rms(x, g_ref[...])
    y = jax.nn.gelu(y_ref[...] + d_ref[...] * u)
    gate = jax.nn.sigmoid(_dot(y.astype(BF16), w_ref[...]) + b_ref[...])
    o_ref[...] = _ffn_rows(x + y * gate, fg_ref, wg_ref, wu_ref, wd_ref, acc_ref, hc)


def _s5_ffn_call(x, y, g, d, w, b, fg, wg, wu, wd, row, hc):
    t, dm = x.shape
    hidden = wg.shape[1]
    tile, weights = _ffn_specs(row, dm, hidden)
    return pl.pallas_call(
        functools.partial(_s5_ffn_kernel, hc=hc),
        grid=(t // row,),
        in_specs=[tile, tile, _resident((1, dm)), _resident((1, dm)), _resident((dm, dm)),
                  _resident((1, dm))] + weights,
        out_specs=tile,
        out_shape=jax.ShapeDtypeStruct((t, dm), F32),
        scratch_shapes=[pltpu.VMEM((row, dm), F32)],
        compiler_params=_cparams(("parallel",)),
        name="s5_ffn",
    )(x, y, g.reshape(1, dm), d.reshape(1, dm), w, b.reshape(1, dm), fg.reshape(1, dm), wg, wu, wd)


def _s5_tab_kernel(p_ref, q_ref, c1_ref, c2_ref, b1_ref, b2_ref, bs_ref, tt_ref, nn_ref, mm_ref):
    sub = S5_SUB
    width = sub * S5_GROUP
    lane = lax.broadcasted_iota(jnp.int32, (S5_GROUP, width), 1)
    sgn = jnp.where(lax.broadcasted_iota(jnp.int32, (1, 2 * S5_STATE), 1) < S5_STATE, 1.0, -1.0)
    for gi in range(p_ref.shape[0]):
        p, q = p_ref[gi], q_ref[gi]
        c1, c2, b1, b2 = c1_ref[gi], c2_ref[gi], b1_ref[gi], b2_ref[gi]
        cl = [c1 * p[k:k + 1] + c2 * q[k:k + 1] for k in range(sub + 1)]
        mm_ref[gi] = (jnp.concatenate(cl[1:], axis=0) * sgn).astype(mm_ref.dtype)
        r = jnp.dot(bs_ref[gi], jnp.concatenate(cl[:sub], axis=0).T,
                    precision=lax.Precision.HIGHEST, preferred_element_type=F32)
        rows = [r] + [jnp.where(lane >= S5_GROUP * s, pltpu.roll(r, S5_GROUP * s, axis=1), 0.0)
                      for s in range(1, sub)]
        tt_ref[gi] = jnp.concatenate(rows, axis=0).T.astype(tt_ref.dtype)
        nt = [b1 * p[sub - 1 - s:sub - s] + b2 * q[sub - 1 - s:sub - s] for s in range(sub)]
        nn_ref[gi] = jnp.concatenate(nt, axis=0).T.astype(nn_ref.dtype)


def _s5_tab_call(p, q, c1, c2, b1, b2, bs):
    groups = p.shape[0]
    gt = 4 if groups % 4 == 0 else 1
    width = S5_SUB * S5_GROUP
    st2 = 2 * S5_STATE
    blk = lambda a: pl.BlockSpec((gt,) + a.shape[1:], lambda i: (i, 0, 0))
    out = lambda r, c: pl.BlockSpec((gt, r, c), lambda i: (i, 0, 0))
    return pl.pallas_call(
        _s5_tab_kernel,
        grid=(groups // gt,),
        in_specs=[blk(a) for a in (p, q, c1, c2, b1, b2, bs)],
        out_specs=[out(width, width), out(st2, width), out(width, st2)],
        out_shape=[jax.ShapeDtypeStruct((groups, width, width), BF16),
                   jax.ShapeDtypeStruct((groups, st2, width), BF16),
                   jax.ShapeDtypeStruct((groups, width, st2), BF16)],
        compiler_params=_cparams(("parallel",)),
        name="s5_tab",
    )(p, q, c1, c2, b1, b2, bs)


def _s5_tables(lam_re, lam_im, log_step, b_re, b_im, c_re, c_im, seg_len, nsb, nseg):
    sub = S5_SUB
    step = jnp.exp(log_step)[:, None]
    dre, dim = lam_re * step, lam_im * step

    def power(k):
        kk = jnp.asarray(k, F32)[None, :, None]
        mag = jnp.exp(kk * dre[:, None, :])
        return mag * jnp.cos(kk * dim[:, None, :]), mag * jnp.sin(kk * dim[:, None, :])

    pw_re, pw_im = power(jnp.arange(sub + 1))
    den = lam_re * lam_re + lam_im * lam_im
    nr, ni = pw_re[:, 1] - 1.0, pw_im[:, 1]
    cf_re = ((nr * lam_re + ni * lam_im) / den)[:, None, :]
    cf_im = ((ni * lam_re - nr * lam_im) / den)[:, None, :]
    bt_re, bt_im = b_re.transpose(0, 2, 1), b_im.transpose(0, 2, 1)
    bb_re = cf_re * bt_re - cf_im * bt_im
    bb_im = cf_re * bt_im + cf_im * bt_re
    cat = lambda a, b: jnp.concatenate([a, b], axis=-1)
    tt, nn, mm = _s5_tab_call(cat(pw_re, pw_im), cat(pw_im, pw_re), cat(c_re, c_re),
                              cat(-c_im, c_im), cat(bb_re, bb_re), cat(-bb_im, bb_im),
                              cat(bb_re, -bb_im))
    a_re = jnp.broadcast_to(pw_re[:, sub].reshape(-1, 1), (pw_re[:, sub].size, nseg))
    a_im = jnp.broadcast_to(pw_im[:, sub].reshape(-1, 1), (pw_im[:, sub].size, nseg))
    nd = max(1, (nsb - 1).bit_length())
    p_re, p_im = power(seg_len * (2 ** jnp.arange(nd)))
    p_re = p_re.transpose(0, 2, 1).reshape(-1, nd)
    p_im = p_im.transpose(0, 2, 1).reshape(-1, nd)
    return tt, nn, mm, a_re, a_im, p_re, p_im


def _s5_layer(x, batch, seq, norm_w, lam_re, lam_im, log_step, b_re, b_im, c_re, c_im, d, glu_w,
              glu_b, ffn_g, ffn_wg, ffn_wu, ffn_wd, row, hc):
    t, dm = x.shape
    nseg = V7X_LANES
    seg_len = t // nseg
    assert t % nseg == 0 and seg_len % S5_SUB == 0 and seq % seg_len == 0
    nsb = seq // seg_len
    tt, nn, mm, a_re, a_im, p_re, p_im = _s5_tables(
        lam_re, lam_im, log_step, b_re, b_im, c_re, c_im, seg_len, nsb, nseg)
    zt, er, ei = _s5_pre_call(x.reshape(nseg, seg_len, dm), norm_w, nn, a_re, a_im)
    x0r, x0i = _s5_stitch_call(er, ei, p_re, p_im, nsb)
    y = _s5_core_call(zt, tt, nn, mm, a_re, a_im, x0r, x0i, dm).reshape(t, dm)
    return _s5_ffn_call(x, y, norm_w, d.reshape(-1), glu_w.astype(BF16), glu_b, ffn_g,
                        ffn_wg.astype(BF16), ffn_wu.astype(BF16), ffn_wd.astype(BF16), row, hc)


def _ret_kernel(x_ref, g_ref, w_ref, cr_ref, sr_ref, cb_ref, sb_ref, gn_ref, dm_ref, qd_ref, kd_ref, cd_ref,
                y_ref, state_ref, *, heads, blk):
    @pl.when(pl.program_id(1) == 0)
    def _():
        state_ref[...] = jnp.zeros_like(state_ref)

    u = _rms(x_ref[...], g_ref[...]).astype(BF16)
    cb = cb_ref[pl.ds(pl.program_id(1), 1), :]
    sb = sb_ref[pl.ds(pl.program_id(1), 1), :]
    cos = cr_ref[...] * cb - sr_ref[...] * sb
    sin = sr_ref[...] * cb + cr_ref[...] * sb
    half = RET_QK_DIM // 2
    qk_w = heads * RET_QK_DIM
    v_w = heads * RET_V_DIM
    k_scale = RET_QK_DIM ** -0.5
    row = x_ref.shape[0]

    def rotary(tq):
        t1, t2 = tq[:, :half], tq[:, half:]
        return jnp.concatenate([t1 * cos - t2 * sin, t1 * sin + t2 * cos], axis=-1)

    for h in range(heads):
        cq = h * RET_QK_DIM
        cv = 2 * qk_w + h * RET_V_DIM
        q = rotary(_dot(u, w_ref[:, cq:cq + RET_QK_DIM])).astype(BF16)
        k = rotary(_dot(u, w_ref[:, qk_w + cq:qk_w + cq + RET_QK_DIM]) * k_scale)
        v = _dot(u, w_ref[:, cv:cv + RET_V_DIM]).astype(BF16)
        gt = _dot(u, w_ref[:, v_w + cv:v_w + cv + RET_V_DIM])
        gn = gn_ref[:, h * RET_V_DIM:(h + 1) * RET_V_DIM]
        for r0 in range(0, row, blk):
            qs, ks, vs = q[r0:r0 + blk], k[r0:r0 + blk], v[r0:r0 + blk]
            s = lax.dot_general(qs, ks.astype(BF16), (((1,), (1,)), ((), ())),
                                preferred_element_type=F32)
            o = _dot((s * dm_ref[h]).astype(BF16), vs)
            st = state_ref[h]
            o = o + _dot(qs, st.astype(BF16)) * qd_ref[h]
            kd = (ks * kd_ref[h]).astype(BF16)
            state_ref[h] = st * cd_ref[h] + lax.dot_general(
                kd, vs, (((0,), (0,)), ((), ())), preferred_element_type=F32)
            mean = jnp.mean(o, axis=-1, keepdims=True)
            cen = o - mean
            var = jnp.mean(jnp.square(cen), axis=-1, keepdims=True)
            on = cen * lax.rsqrt(var + NORM_EPS) * gn
            y_ref[r0:r0 + blk, h * RET_V_DIM:(h + 1) * RET_V_DIM] = (
                jax.nn.silu(gt[r0:r0 + blk]) * on).astype(y_ref.dtype)


def _ret_call(x, g, w, rot, gn_w, dmask, qdec, kdec, cdec, batch, seq, heads, row, blk):
    t, dm = x.shape
    v_w = heads * RET_V_DIM
    nb = seq // row
    kern = functools.partial(_ret_kernel, heads=heads, blk=blk)
    rows = lambda width: pl.BlockSpec((row, width), lambda b, i: (b * nb + i, 0))
    tabs = [_resident(a.shape) for a in rot]
    return pl.pallas_call(
        kern,
        grid=(batch, nb),
        in_specs=[rows(dm), _resident((1, dm)), _resident(w.shape)] + tabs + [
                  _resident((1, v_w)), _resident(dmask.shape), _resident(qdec.shape),
                  _resident(kdec.shape), _resident(cdec.shape)],
        out_specs=rows(v_w),
        out_shape=jax.ShapeDtypeStruct((t, v_w), BF16),
        scratch_shapes=[pltpu.VMEM((heads, RET_QK_DIM, RET_V_DIM), F32)],
        compiler_params=_cparams(("parallel", "arbitrary")),
        name="ret_mix",
    )(x, g.reshape(1, dm), w, *rot, gn_w.reshape(1, v_w), dmask, qdec, kdec, cdec)


def _ret_decays(heads, blk):
    log_gamma = jnp.log1p(-jnp.exp2(-5.0 - jnp.arange(heads, dtype=F32)))
    pos = jnp.arange(blk, dtype=F32)
    diff = pos[:, None] - pos[None, :]
    cn = (jnp.arange(blk) // CHUNK)[:, None]
    cm = (jnp.arange(blk) // CHUNK)[None, :]
    expo = jnp.where(cn == cm, jnp.abs(diff), diff)
    dmask = jnp.where((cm <= cn)[None], jnp.exp(log_gamma[:, None, None] * expo[None]), 0.0)
    qdec = jnp.exp((pos[None, :] + 1.0) * log_gamma[:, None])[..., None]
    kdec = jnp.exp((blk - 1.0 - pos)[None, :] * log_gamma[:, None])[..., None]
    cdec = jnp.exp(blk * log_gamma)[:, None, None]
    return dmask, qdec, kdec, cdec


def _rotary_tables(seq, row):
    inv_freq = 1.0 / (ROPE_BASE ** jnp.linspace(0.0, 1.0, RET_QK_DIM // 2, dtype=F32))
    ang_r = jnp.arange(row, dtype=F32)[:, None] * inv_freq[None, :]
    ang_b = (jnp.arange(seq // row, dtype=F32) * row)[:, None] * inv_freq[None, :]
    return jnp.cos(ang_r), jnp.sin(ang_r), jnp.cos(ang_b), jnp.sin(ang_b)


def _ret_ffn_kernel(x_ref, y_ref, w_ref, fg_ref, wg_ref, wu_ref, wd_ref, fin_ref, o_ref, acc_ref,
                    *, hc):
    x1 = x_ref[...] + _dot(y_ref[...], w_ref[...])
    out = _ffn_rows(x1, fg_ref, wg_ref, wu_ref, wd_ref, acc_ref, hc)
    o_ref[...] = _rms(out, fin_ref[...])


def _ret_ffn_call(x, y, w, fg, wg, wu, wd, fin, row, hc):
    t, dm = x.shape
    hidden = wg.shape[1]
    kdim = y.shape[1]
    tile, weights = _ffn_specs(row, dm, hidden)
    return pl.pallas_call(
        functools.partial(_ret_ffn_kernel, hc=hc),
        grid=(t // row,),
        in_specs=[tile, pl.BlockSpec((row, kdim), lambda i: (i, 0)), _resident((kdim, dm))]
        + weights + [_resident((1, dm))],
        out_specs=tile,
        out_shape=jax.ShapeDtypeStruct((t, dm), F32),
        scratch_shapes=[pltpu.VMEM((row, dm), F32)],
        compiler_params=_cparams(("parallel",)),
        name="ret_ffn",
    )(x, y, w, fg.reshape(1, dm), wg, wu, wd, fin.reshape(1, dm))


def _ret_layer(x, batch, seq, norm_w, w_qkvg, gn_w, w_o, ffn_g, ffn_wg, ffn_wu, ffn_wd, fin,
               row, blk, hc):
    dm = x.shape[1]
    heads = dm // RET_QK_DIM
    rot = _rotary_tables(seq, row)
    dmask, qdec, kdec, cdec = _ret_decays(heads, blk)
    y = _ret_call(x, norm_w, w_qkvg.astype(BF16), rot, gn_w, dmask, qdec, kdec, cdec,
                  batch, seq, heads, row, blk)
    return _ret_ffn_call(x, y, w_o.astype(BF16), ffn_g, ffn_wg.astype(BF16), ffn_wu.astype(BF16),
                         ffn_wd.astype(BF16), fin, row, hc)


def kernel(x, s5_norm, s5_lambda_re, s5_lambda_im, s5_log_step, s5_b_re, s5_b_im, s5_c_re, s5_c_im,
           s5_d, s5_glu_w, s5_glu_b, ret_norm, ret_w_qkvg, ret_gn_w, ret_w_o, ffn_norm, ffn_w_gate,
           ffn_w_up, ffn_w_down, final_norm):
    batch, seq, dm = x.shape
    row, blk, hc = _tiles(seq)
    assert ffn_norm.shape[0] == 2 and s5_norm.shape[0] == 1 and ret_norm.shape[0] == 1
    h = x.reshape(batch * seq, dm)
    h = _s5_layer(h, batch, seq, s5_norm[0], s5_lambda_re[0], s5_lambda_im[0], s5_log_step[0],
                  s5_b_re[0], s5_b_im[0], s5_c_re[0], s5_c_im[0], s5_d[0], s5_glu_w[0],
                  s5_glu_b[0], ffn_norm[0], ffn_w_gate[0], ffn_w_up[0], ffn_w_down[0], row, hc)
    h = _ret_layer(h, batch, seq, ret_norm[0], ret_w_qkvg[0], ret_gn_w[0], ret_w_o[0],
                   ffn_norm[1], ffn_w_gate[1], ffn_w_up[1], ffn_w_down[1], final_norm,
                   row, blk, hc)
    return h.reshape(batch, seq, dm)
```

```python
import functools

import jax
import jax.numpy as jnp
from jax import lax
from jax.experimental import pallas as pl
from jax.experimental.pallas import tpu as pltpu

F32 = jnp.float32
BF16 = jnp.bfloat16

NORM_EPS = 1e-6
CHUNK = 64
S5_GROUP = 16
S5_STATE = 64
S5_SUB = 16
S5_GB = 16
RET_QK_DIM = 256
RET_V_DIM = 512
ROPE_BASE = 10000.0

V7X_VMEM_BYTES = 64 * 1024 * 1024
V7X_LANES = 128
VMEM_LIMIT = 56 * 1024 * 1024


def _tiles(seq):
    row = min(512, seq)
    ret = min(256, seq)
    hc = 256
    assert seq % row == 0 and row % ret == 0 and ret % CHUNK == 0
    return row, ret, hc


def _cparams(sem):
    return pltpu.CompilerParams(dimension_semantics=sem, vmem_limit_bytes=VMEM_LIMIT)


def _resident(shape):
    nd = len(shape)
    return pl.BlockSpec(shape, lambda *_: (0,) * nd, pipeline_mode=pl.Buffered(1))


def _rms(xf, g):
    ms = jnp.mean(jnp.square(xf), axis=-1, keepdims=True)
    return xf * lax.rsqrt(ms + NORM_EPS) * g


def _dot(a, b):
    return jnp.dot(a, b, preferred_element_type=F32)


def _bdot(a, b):
    return lax.dot_general(a, b, (((2,), (1,)), ((0,), (0,))), preferred_element_type=F32)


def _s5_pre_kernel(x_hbm, g_ref, nn_ref, are_ref, aim_ref, zt_ref, er_ref, ei_ref, xs, sem):
    groups = zt_ref.shape[1]
    nseg = zt_ref.shape[3]
    i = pl.program_id(0)
    steps = pl.num_programs(0)

    def gathers(step, slot):
        return [pltpu.make_async_copy(x_hbm.at[:, step * S5_SUB + t, :], xs.at[slot, t],
                                      sem.at[slot]) for t in range(S5_SUB)]

    @pl.when(i == 0)
    def _():
        for c in gathers(0, 0):
            c.start()

    @pl.when(i + 1 < steps)
    def _():
        for c in gathers(i + 1, (i + 1) % 2):
            c.start()

    slot = i % 2
    for c in gathers(i, slot):
        c.wait()
    g = g_ref[...]
    for t in range(S5_SUB):
        ut = _rms(xs[slot, t], g)
        zt_ref[0, :, t * S5_GROUP:(t + 1) * S5_GROUP, :] = (
            ut.T.reshape(groups, S5_GROUP, nseg).astype(zt_ref.dtype))
    s = _bdot(nn_ref[...], zt_ref[0])
    sr = s[:, :S5_STATE, :].reshape(groups * S5_STATE, nseg)
    si = s[:, S5_STATE:, :].reshape(groups * S5_STATE, nseg)

    @pl.when(i == 0)
    def _():
        er_ref[...] = sr
        ei_ref[...] = si

    @pl.when(i > 0)
    def _():
        er, ei = er_ref[...], ei_ref[...]
        ar, ai = are_ref[...], aim_ref[...]
        er_ref[...] = ar * er - ai * ei + sr
        ei_ref[...] = ar * ei + ai * er + si


def _s5_pre_call(x3, g, nn, a_re, a_im):
    nseg, sl, dm = x3.shape
    groups = dm // S5_GROUP
    steps = sl // S5_SUB
    width = S5_SUB * S5_GROUP
    rows = groups * S5_STATE
    return pl.pallas_call(
        _s5_pre_kernel,
        grid=(steps,),
        in_specs=[pl.BlockSpec(memory_space=pl.ANY), _resident((1, dm)),
                  _resident(nn.shape), _resident((rows, nseg)), _resident((rows, nseg))],
        out_specs=[pl.BlockSpec((1, groups, width, nseg), lambda i: (i, 0, 0, 0)),
                   pl.BlockSpec((rows, nseg), lambda i: (0, 0)),
                   pl.BlockSpec((rows, nseg), lambda i: (0, 0))],
        out_shape=[jax.ShapeDtypeStruct((steps, groups, width, nseg), BF16),
                   jax.ShapeDtypeStruct((rows, nseg), F32),
                   jax.ShapeDtypeStruct((rows, nseg), F32)],
        scratch_shapes=[pltpu.VMEM((2, S5_SUB, nseg, dm), F32), pltpu.SemaphoreType.DMA((2,))],
        compiler_params=_cparams(("arbitrary",)),
        name="s5_pre",
    )(x3, g.reshape(1, dm), nn, a_re, a_im)


def _s5_stitch_kernel(er_ref, ei_ref, pr_ref, pi_ref, xr_ref, xi_ref, *, nsb):
    er, ei = er_ref[...], ei_ref[...]
    lane = lax.broadcasted_iota(jnp.int32, er.shape, 1) % nsb
    d, k = 1, 0
    while d < nsb:
        sr, si = pltpu.roll(er, d, axis=1), pltpu.roll(ei, d, axis=1)
        pr, pi = pr_ref[:, k:k + 1], pi_ref[:, k:k + 1]
        ok = lane >= d
        er, ei = (er + jnp.where(ok, pr * sr - pi * si, 0.0),
                  ei + jnp.where(ok, pr * si + pi * sr, 0.0))
        d, k = 2 * d, k + 1
    ok = lane >= 1
    xr_ref[...] = jnp.where(ok, pltpu.roll(er, 1, axis=1), 0.0)
    xi_ref[...] = jnp.where(ok, pltpu.roll(ei, 1, axis=1), 0.0)


def _s5_stitch_call(er, ei, p_re, p_im, nsb):
    shp = jax.ShapeDtypeStruct(er.shape, F32)
    return pl.pallas_call(
        functools.partial(_s5_stitch_kernel, nsb=nsb),
        out_shape=[shp, shp],
        compiler_params=pltpu.CompilerParams(vmem_limit_bytes=VMEM_LIMIT),
        name="s5_stitch",
    )(er, ei, p_re, p_im)


def _s5_core_kernel(zt_ref, tt_ref, nn_ref, mm_ref, are_ref, aim_ref, x0r_ref, x0i_ref, y_hbm,
                    xr_scr, xi_scr, ybuf, sem):
    pair, gb, _, nseg = zt_ref.shape
    cols = gb * S5_GROUP
    j, k = pl.program_id(0), pl.program_id(1)
    n = j * pl.num_programs(1) + k
    last = pl.num_programs(0) * pl.num_programs(1) - 1
    slot = n % 2

    def scatters(sl):
        return [pltpu.make_async_copy(
            ybuf.at[sl, s, t],
            y_hbm.at[:, (k * pair + s) * S5_SUB + t, pl.ds(j * cols, cols)],
            sem.at[sl]) for s in range(pair) for t in range(S5_SUB)]

    @pl.when(k == 0)
    def _():
        xr_scr[...] = x0r_ref[...]
        xi_scr[...] = x0i_ref[...]

    @pl.when(n >= 2)
    def _():
        for c in scatters(slot):
            c.wait()

    zt = jnp.concatenate([zt_ref[s] for s in range(pair)], axis=-1)
    s_all = _bdot(nn_ref[...], zt)
    ar, ai = are_ref[...], aim_ref[...]
    xr, xi = xr_scr[...], xi_scr[...]
    starts = []
    for s in range(pair):
        starts.append(jnp.concatenate([xr.reshape(gb, S5_STATE, nseg),
                                       xi.reshape(gb, S5_STATE, nseg)], axis=1))
        inc = s_all[:, :, s * nseg:(s + 1) * nseg]
        sr = inc[:, :S5_STATE, :].reshape(gb * S5_STATE, nseg)
        si = inc[:, S5_STATE:, :].reshape(gb * S5_STATE, nseg)
        xr, xi = ar * xr - ai * xi + sr, ar * xi + ai * xr + si
    xr_scr[...] = xr
    xi_scr[...] = xi
    xprev = jnp.concatenate(starts, axis=-1).astype(zt.dtype)
    y = _bdot(tt_ref[...], zt) + _bdot(mm_ref[...], xprev)
    for t in range(S5_SUB):
        blk = y[:, t * S5_GROUP:(t + 1) * S5_GROUP, :].reshape(cols, pair * nseg)
        bt = blk.T
        for s in range(pair):
            ybuf[slot, s, t] = bt[s * nseg:(s + 1) * nseg]
    for c in scatters(slot):
        c.start()

    @pl.when(n == last)
    def _():
        for c in scatters(slot):
            c.wait()

    @pl.when(jnp.logical_and(n == last, n >= 1))
    def _():
        for c in scatters(1 - slot):
            c.wait()


def _s5_core_call(zt, tt, nn, mm, a_re, a_im, x0r, x0i, dm):
    steps, groups, width, nseg = zt.shape
    gb = min(S5_GB, groups)
    pair = 2 if steps % 2 == 0 else 1
    rows = gb * S5_STATE
    per_gb = lambda shape: pl.BlockSpec(shape, lambda j, k: (j,) + (0,) * (len(shape) - 1))
    return pl.pallas_call(
        _s5_core_kernel,
        grid=(groups // gb, steps // pair),
        in_specs=[pl.BlockSpec((pair, gb, width, nseg), lambda j, k: (k, j, 0, 0)),
                  per_gb((gb, width, width)), per_gb((gb, 2 * S5_STATE, width)),
                  per_gb((gb, width, 2 * S5_STATE)),
                  per_gb((rows, nseg)), per_gb((rows, nseg)), per_gb((rows, nseg)),
                  per_gb((rows, nseg))],
        out_specs=pl.BlockSpec(memory_space=pl.ANY),
        out_shape=jax.ShapeDtypeStruct((nseg, steps * S5_SUB, dm), F32),
        scratch_shapes=[pltpu.VMEM((rows, nseg), F32), pltpu.VMEM((rows, nseg), F32),
                        pltpu.VMEM((2, pair, S5_SUB, nseg, gb * S5_GROUP), F32),
                        pltpu.SemaphoreType.DMA((2,))],
        compiler_params=_cparams(("arbitrary", "arbitrary")),
        name="s5_core",
    )(zt, tt, nn, mm, a_re, a_im, x0r, x0i)


def _ffn_rows(x1_ref, u_ref, wg_ref, wu_ref, wd_ref, acc_ref, hc):
    u = u_ref[...]
    hidden = wg_ref.shape[1]
    for j in range(hidden // hc):
        cols = slice(j * hc, (j + 1) * hc)
        gt = _dot(u, wg_ref[:, cols])
        up = _dot(u, wu_ref[:, cols])
        h = (jax.nn.silu(gt) * up).astype(BF16)
        part = _dot(h, wd_ref[cols, :])
        if j == 0:
            acc_ref[...] = part
        else:
            acc_ref[...] += part
    return x1_ref[...] + acc_ref[...]


def _two_stage(front, back, bufs_a, bufs_b):
    i = pl.program_id(0)

    @pl.when(i == 0)
    def _():
        for r in bufs_a:
            r[...] = jnp.zeros_like(r)

    @pl.when(i % 2 == 0)
    def _():
        front(bufs_b)
        back(bufs_a)

    @pl.when(i % 2 == 1)
    def _():
        front(bufs_a)
        back(bufs_b)


def _ffn_pipeline_specs(n_tiles, row, dm, hidden):
    tile_in = lambda width: pl.BlockSpec((row, width), lambda i: (jnp.minimum(i, n_tiles - 1), 0))
    tile_out = pl.BlockSpec((row, dm), lambda i: (jnp.maximum(i - 1, 0), 0))
    weights = [_resident((1, dm)), _resident((dm, hidden)), _resident((dm, hidden)),
               _resident((hidden, dm))]
    scratch = [pltpu.VMEM((row, dm), F32)] + 2 * [pltpu.VMEM((row, dm), F32),
                                                  pltpu.VMEM((row, dm), BF16)]
    return tile_in, tile_out, weights, scratch


def _s5_ffn_kernel(x_ref, y_ref, g_ref, d_ref, w_ref, b_ref, fg_ref, wg_ref, wu_ref, wd_ref,
                   o_ref, acc_ref, x1a, ua, x1b, ub, *, hc):
    def front(bufs):
        x = x_ref[...]
        u = _rms(x, g_ref[...])
        y = jax.nn.gelu(y_ref[...] + d_ref[...] * u)
        gate = jax.nn.sigmoid(_dot(y.astype(BF16), w_ref[...]) + b_ref[...])
        x1 = x + y * gate
        bufs[0][...] = x1
        bufs[1][...] = _rms(x1, fg_ref[...]).astype(BF16)

    def back(bufs):
        o_ref[...] = _ffn_rows(bufs[0], bufs[1], wg_ref, wu_ref, wd_ref, acc_ref, hc)

    _two_stage(front, back, (x1a, ua), (x1b, ub))


def _s5_ffn_call(x, y, g, d, w, b, fg, wg, wu, wd, row, hc):
    t, dm = x.shape
    hidden = wg.shape[1]
    n_tiles = t // row
    tile_in, tile_out, weights, scratch = _ffn_pipeline_specs(n_tiles, row, dm, hidden)
    return pl.pallas_call(
        functools.partial(_s5_ffn_kernel, hc=hc),
        grid=(n_tiles + 1,),
        in_specs=[tile_in(dm), tile_in(dm), _resident((1, dm)), _resident((1, dm)),
                  _resident((dm, dm)), _resident((1, dm))] + weights,
        out_specs=tile_out,
        out_shape=jax.ShapeDtypeStruct((t, dm), F32),
        scratch_shapes=scratch,
        compiler_params=_cparams(("arbitrary",)),
        name="s5_ffn",
    )(x, y, g.reshape(1, dm), d.reshape(1, dm), w, b.reshape(1, dm), fg.reshape(1, dm), wg, wu, wd)


def _s5_tab_kernel(p_ref, q_ref, c1_ref, c2_ref, b1_ref, b2_ref, bs_ref, tt_ref, nn_ref, mm_ref):
    sub = S5_SUB
    width = sub * S5_GROUP
    lane = lax.broadcasted_iota(jnp.int32, (S5_GROUP, width), 1)
    sgn = jnp.where(lax.broadcasted_iota(jnp.int32, (1, 2 * S5_STATE), 1) < S5_STATE, 1.0, -1.0)
    for gi in range(p_ref.shape[0]):
        p, q = p_ref[gi], q_ref[gi]
        c1, c2, b1, b2 = c1_ref[gi], c2_ref[gi], b1_ref[gi], b2_ref[gi]
        cl = [c1 * p[k:k + 1] + c2 * q[k:k + 1] for k in range(sub + 1)]
        mm_ref[gi] = (jnp.concatenate(cl[1:], axis=0) * sgn).astype(mm_ref.dtype)
        r = jnp.dot(bs_ref[gi], jnp.concatenate(cl[:sub], axis=0).T,
                    precision=lax.Precision.HIGHEST, preferred_element_type=F32)
        rows = [r] + [jnp.where(lane >= S5_GROUP * s, pltpu.roll(r, S5_GROUP * s, axis=1), 0.0)
                      for s in range(1, sub)]
        tt_ref[gi] = jnp.concatenate(rows, axis=0).T.astype(tt_ref.dtype)
        nt = [b1 * p[sub - 1 - s:sub - s] + b2 * q[sub - 1 - s:sub - s] for s in range(sub)]
        nn_ref[gi] = jnp.concatenate(nt, axis=0).T.astype(nn_ref.dtype)


def _s5_tab_call(p, q, c1, c2, b1, b2, bs):
    groups = p.shape[0]
    gt = 4 if groups % 4 == 0 else 1
    width = S5_SUB * S5_GROUP
    st2 = 2 * S5_STATE
    blk = lambda a: pl.BlockSpec((gt,) + a.shape[1:], lambda i: (i, 0, 0))
    out = lambda r, c: pl.BlockSpec((gt, r, c), lambda i: (i, 0, 0))
    return pl.pallas_call(
        _s5_tab_kernel,
        grid=(groups // gt,),
        in_specs=[blk(a) for a in (p, q, c1, c2, b1, b2, bs)],
        out_specs=[out(width, width), out(st2, width), out(width, st2)],
        out_shape=[jax.ShapeDtypeStruct((groups, width, width), BF16),
                   jax.ShapeDtypeStruct((groups, st2, width), BF16),
                   jax.ShapeDtypeStruct((groups, width, st2), BF16)],
        compiler_params=_cparams(("parallel",)),
        name="s5_tab",
    )(p, q, c1, c2, b1, b2, bs)


def _s5_tables(lam_re, lam_im, log_step, b_re, b_im, c_re, c_im, seg_len, nsb, nseg):
    sub = S5_SUB
    step = jnp.exp(log_step)[:, None]
    dre, dim = lam_re * step, lam_im * step

    def power(k):
        kk = jnp.asarray(k, F32)[None, :, None]
        mag = jnp.exp(kk * dre[:, None, :])
        return mag * jnp.cos(kk * dim[:, None, :]), mag * jnp.sin(kk * dim[:, None, :])

    pw_re, pw_im = power(jnp.arange(sub + 1))
    den = lam_re * lam_re + lam_im * lam_im
    nr, ni = pw_re[:, 1] - 1.0, pw_im[:, 1]
    cf_re = ((nr * lam_re + ni * lam_im) / den)[:, None, :]
    cf_im = ((ni * lam_re - nr * lam_im) / den)[:, None, :]
    bt_re, bt_im = b_re.transpose(0, 2, 1), b_im.transpose(0, 2, 1)
    bb_re = cf_re * bt_re - cf_im * bt_im
    bb_im = cf_re * bt_im + cf_im * bt_re
    cat = lambda a, b: jnp.concatenate([a, b], axis=-1)
    tt, nn, mm = _s5_tab_call(cat(pw_re, pw_im), cat(pw_im, pw_re), cat(c_re, c_re),
                              cat(-c_im, c_im), cat(bb_re, bb_re), cat(-bb_im, bb_im),
                              cat(bb_re, -bb_im))
    a_re = jnp.broadcast_to(pw_re[:, sub].reshape(-1, 1), (pw_re[:, sub].size, nseg))
    a_im = jnp.broadcast_to(pw_im[:, sub].reshape(-1, 1), (pw_im[:, sub].size, nseg))
    nd = max(1, (nsb - 1).bit_length())
    p_re, p_im = power(seg_len * (2 ** jnp.arange(nd)))
    p_re = p_re.transpose(0, 2, 1).reshape(-1, nd)
    p_im = p_im.transpose(0, 2, 1).reshape(-1, nd)
    return tt, nn, mm, a_re, a_im, p_re, p_im


def _s5_layer(x, batch, seq, norm_w, lam_re, lam_im, log_step, b_re, b_im, c_re, c_im, d, glu_w,
              glu_b, ffn_g, ffn_wg, ffn_wu, ffn_wd, row, hc):
    t, dm = x.shape
    nseg = V7X_LANES
    seg_len = t // nseg
    assert t % nseg == 0 and seg_len % S5_SUB == 0 and seq % seg_len == 0
    nsb = seq // seg_len
    tt, nn, mm, a_re, a_im, p_re, p_im = _s5_tables(
        lam_re, lam_im, log_step, b_re, b_im, c_re, c_im, seg_len, nsb, nseg)
    zt, er, ei = _s5_pre_call(x.reshape(nseg, seg_len, dm), norm_w, nn, a_re, a_im)
    x0r, x0i = _s5_stitch_call(er, ei, p_re, p_im, nsb)
    y = _s5_core_call(zt, tt, nn, mm, a_re, a_im, x0r, x0i, dm).reshape(t, dm)
    return _s5_ffn_call(x, y, norm_w, d.reshape(-1), glu_w.astype(BF16), glu_b, ffn_g,
                        ffn_wg.astype(BF16), ffn_wu.astype(BF16), ffn_wd.astype(BF16), row, hc)


def _ret_kernel(x_ref, g_ref, w_ref, cr_ref, sr_ref, cb_ref, sb_ref, gn_ref, dm_ref, qd_ref, kd_ref, cd_ref,
                y_ref, state_ref, *, heads, blk):
    @pl.when(pl.program_id(1) == 0)
    def _():
        state_ref[...] = jnp.zeros_like(state_ref)

    u = _rms(x_ref[...], g_ref[...]).astype(BF16)
    cb = cb_ref[pl.ds(pl.program_id(1), 1), :]
    sb = sb_ref[pl.ds(pl.program_id(1), 1), :]
    cos = cr_ref[...] * cb - sr_ref[...] * sb
    sin = sr_ref[...] * cb + cr_ref[...] * sb
    half = RET_QK_DIM // 2
    qk_w = heads * RET_QK_DIM
    v_w = heads * RET_V_DIM
    k_scale = RET_QK_DIM ** -0.5
    row = x_ref.shape[0]

    def rotary(tq):
        t1, t2 = tq[:, :half], tq[:, half:]
        return jnp.concatenate([t1 * cos - t2 * sin, t1 * sin + t2 * cos], axis=-1)

    for h in range(heads):
        cq = h * RET_QK_DIM
        cv = 2 * qk_w + h * RET_V_DIM
        q = rotary(_dot(u, w_ref[:, cq:cq + RET_QK_DIM])).astype(BF16)
        k = rotary(_dot(u, w_ref[:, qk_w + cq:qk_w + cq + RET_QK_DIM]) * k_scale)
        v = _dot(u, w_ref[:, cv:cv + RET_V_DIM]).astype(BF16)
        gt = _dot(u, w_ref[:, v_w + cv:v_w + cv + RET_V_DIM])
        gn = gn_ref[:, h * RET_V_DIM:(h + 1) * RET_V_DIM]
        for r0 in range(0, row, blk):
            qs, ks, vs = q[r0:r0 + blk], k[r0:r0 + blk], v[r0:r0 + blk]
            s = lax.dot_general(qs, ks.astype(BF16), (((1,), (1,)), ((), ())),
                                preferred_element_type=F32)
            o = _dot((s * dm_ref[h]).astype(BF16), vs)
            st = state_ref[h]
            o = o + _dot(qs, st.astype(BF16)) * qd_ref[h]
            kd = (ks * kd_ref[h]).astype(BF16)
            state_ref[h] = st * cd_ref[h] + lax.dot_general(
                kd, vs, (((0,), (0,)), ((), ())), preferred_element_type=F32)
            mean = jnp.mean(o, axis=-1, keepdims=True)
            cen = o - mean
            var = jnp.mean(jnp.square(cen), axis=-1, keepdims=True)
            on = cen * lax.rsqrt(var + NORM_EPS) * gn
            y_ref[r0:r0 + blk, h * RET_V_DIM:(h + 1) * RET_V_DIM] = (
                jax.nn.silu(gt[r0:r0 + blk]) * on).astype(y_ref.dtype)


def _ret_call(x, g, w, rot, gn_w, dmask, qdec, kdec, cdec, batch, seq, heads, row, blk):
    t, dm = x.shape
    v_w = heads * RET_V_DIM
    nb = seq // row
    kern = functools.partial(_ret_kernel, heads=heads, blk=blk)
    rows = lambda width: pl.BlockSpec((row, width), lambda b, i: (b * nb + i, 0))
    tabs = [_resident(a.shape) for a in rot]
    return pl.pallas_call(
        kern,
        grid=(batch, nb),
        in_specs=[rows(dm), _resident((1, dm)), _resident(w.shape)] + tabs + [
                  _resident((1, v_w)), _resident(dmask.shape), _resident(qdec.shape),
                  _resident(kdec.shape), _resident(cdec.shape)],
        out_specs=rows(v_w),
        out_shape=jax.ShapeDtypeStruct((t, v_w), BF16),
        scratch_shapes=[pltpu.VMEM((heads, RET_QK_DIM, RET_V_DIM), F32)],
        compiler_params=_cparams(("parallel", "arbitrary")),
        name="ret_mix",
    )(x, g.reshape(1, dm), w, *rot, gn_w.reshape(1, v_w), dmask, qdec, kdec, cdec)


def _ret_decays(heads, blk):
    log_gamma = jnp.log1p(-jnp.exp2(-5.0 - jnp.arange(heads, dtype=F32)))
    pos = jnp.arange(blk, dtype=F32)
    diff = pos[:, None] - pos[None, :]
    cn = (jnp.arange(blk) // CHUNK)[:, None]
    cm = (jnp.arange(blk) // CHUNK)[None, :]
    expo = jnp.where(cn == cm, jnp.abs(diff), diff)
    dmask = jnp.where((cm <= cn)[None], jnp.exp(log_gamma[:, None, None] * expo[None]), 0.0)
    qdec = jnp.exp((pos[None, :] + 1.0) * log_gamma[:, None])[..., None]
    kdec = jnp.exp((blk - 1.0 - pos)[None, :] * log_gamma[:, None])[..., None]
    cdec = jnp.exp(blk * log_gamma)[:, None, None]
    return dmask, qdec, kdec, cdec


def _rotary_tables(seq, row):
    inv_freq = 1.0 / (ROPE_BASE ** jnp.linspace(0.0, 1.0, RET_QK_DIM // 2, dtype=F32))
    ang_r = jnp.arange(row, dtype=F32)[:, None] * inv_freq[None, :]
    ang_b = (jnp.arange(seq // row, dtype=F32) * row)[:, None] * inv_freq[None, :]
    return jnp.cos(ang_r), jnp.sin(ang_r), jnp.cos(ang_b), jnp.sin(ang_b)


def _ret_ffn_kernel(x_ref, y_ref, w_ref, fg_ref, wg_ref, wu_ref, wd_ref, fin_ref, o_ref, acc_ref,
                    x1a, ua, x1b, ub, *, hc):
    def front(bufs):
        x1 = x_ref[...] + _dot(y_ref[...], w_ref[...])
        bufs[0][...] = x1
        bufs[1][...] = _rms(x1, fg_ref[...]).astype(BF16)

    def back(bufs):
        out = _ffn_rows(bufs[0], bufs[1], wg_ref, wu_ref, wd_ref, acc_ref, hc)
        o_ref[...] = _rms(out, fin_ref[...])

    _two_stage(front, back, (x1a, ua), (x1b, ub))


def _ret_ffn_call(x, y, w, fg, wg, wu, wd, fin, row, hc):
    t, dm = x.shape
    hidden = wg.shape[1]
    kdim = y.shape[1]
    n_tiles = t // row
    tile_in, tile_out, weights, scratch = _ffn_pipeline_specs(n_tiles, row, dm, hidden)
    return pl.pallas_call(
        functools.partial(_ret_ffn_kernel, hc=hc),
        grid=(n_tiles + 1,),
        in_specs=[tile_in(dm), tile_in(kdim), _resident((kdim, dm))] + weights
        + [_resident((1, dm))],
        out_specs=tile_out,
        out_shape=jax.ShapeDtypeStruct((t, dm), F32),
        scratch_shapes=scratch,
        compiler_params=_cparams(("arbitrary",)),
        name="ret_ffn",
    )(x, y, w, fg.reshape(1, dm), wg, wu, wd, fin.reshape(1, dm))


def _ret_layer(x, batch, seq, norm_w, w_qkvg, gn_w, w_o, ffn_g, ffn_wg, ffn_wu, ffn_wd, fin,
               row, blk, hc):
    dm = x.shape[1]
    heads = dm // RET_QK_DIM
    rot = _rotary_tables(seq, row)
    dmask, qdec, kdec, cdec = _ret_decays(heads, blk)
    y = _ret_call(x, norm_w, w_qkvg.astype(BF16), rot, gn_w, dmask, qdec, kdec, cdec,
                  batch, seq, heads, row, blk)
    return _ret_ffn_call(x, y, w_o.astype(BF16), ffn_g, ffn_wg.astype(BF16), ffn_wu.astype(BF16),
                         ffn_wd.astype(BF16), fin, row, hc)


def kernel(x, s5_norm, s5_lambda_re, s5_lambda_im, s5_log_step, s5_b_re, s5_b_im, s5_c_re, s5_c_im,
           s5_d, s5_glu_w, s5_glu_b, ret_norm, ret_w_qkvg, ret_gn_w, ret_w_o, ffn_norm, ffn_w_gate,
           ffn_w_up, ffn_w_down, final_norm):
    batch, seq, dm = x.shape
    row, blk, hc = _tiles(seq)
    assert ffn_norm.shape[0] == 2 and s5_norm.shape[0] == 1 and ret_norm.shape[0] == 1
    h = x.reshape(batch * seq, dm)
    h = _s5_layer(h, batch, seq, s5_norm[0], s5_lambda_re[0], s5_lambda_im[0], s5_log_step[0],
                  s5_b_re[0], s5_b_im[0], s5_c_re[0], s5_c_im[0], s5_d[0], s5_glu_w[0],
                  s5_glu_b[0], ffn_norm[0], ffn_w_gate[0], ffn_w_up[0], ffn_w_down[0], row, hc)
    h = _ret_layer(h, batch, seq, ret_norm[0], ret_w_qkvg[0], ret_gn_w[0], ret_w_o[0],
                   ffn_norm[1], ffn_w_gate[1], ffn_w_up[1], ffn_w_down[1], final_norm,
                   row, blk, hc)
    return h.reshape(batch, seq, dm)
```

```python
import functools

import jax
import jax.numpy as jnp
from jax import lax
from jax.experimental import pallas as pl
from jax.experimental.pallas import tpu as pltpu

F32 = jnp.float32
BF16 = jnp.bfloat16

NORM_EPS = 1e-6
CHUNK = 64
S5_GROUP = 16
S5_STATE = 64
S5_SUB = 16
S5_GB = 16
RET_QK_DIM = 256
RET_V_DIM = 512
ROPE_BASE = 10000.0

V7X_VMEM_BYTES = 64 * 1024 * 1024
V7X_LANES = 128
VMEM_LIMIT = 56 * 1024 * 1024


def _tiles(seq):
    row = min(512, seq)
    ret = min(256, seq)
    hc = 256
    assert seq % row == 0 and row % ret == 0 and ret % CHUNK == 0
    return row, ret, hc


def _cparams(sem):
    return pltpu.CompilerParams(dimension_semantics=sem, vmem_limit_bytes=VMEM_LIMIT)


def _resident(shape):
    nd = len(shape)
    return pl.BlockSpec(shape, lambda *_: (0,) * nd, pipeline_mode=pl.Buffered(1))


def _rms(xf, g):
    ms = jnp.mean(jnp.square(xf), axis=-1, keepdims=True)
    return xf * lax.rsqrt(ms + NORM_EPS) * g


def _dot(a, b):
    return jnp.dot(a, b, preferred_element_type=F32)


def _bdot(a, b):
    return lax.dot_general(a, b, (((2,), (1,)), ((0,), (0,))), preferred_element_type=F32)


def _s5_pre_kernel(x_hbm, g_ref, nn_ref, are_ref, aim_ref, zt_ref, er_ref, ei_ref, xs, sem):
    groups = zt_ref.shape[1]
    nseg = zt_ref.shape[3]
    i = pl.program_id(0)
    steps = pl.num_programs(0)

    def gathers(step, slot):
        return [pltpu.make_async_copy(x_hbm.at[:, step * S5_SUB + t, :], xs.at[slot, t],
                                      sem.at[slot]) for t in range(S5_SUB)]

    @pl.when(i == 0)
    def _():
        for c in gathers(0, 0):
            c.start()

    @pl.when(i + 1 < steps)
    def _():
        for c in gathers(i + 1, (i + 1) % 2):
            c.start()

    slot = i % 2
    for c in gathers(i, slot):
        c.wait()
    g = g_ref[...]
    for t in range(S5_SUB):
        ut = _rms(xs[slot, t], g)
        zt_ref[0, :, t * S5_GROUP:(t + 1) * S5_GROUP, :] = (
            ut.T.reshape(groups, S5_GROUP, nseg).astype(zt_ref.dtype))
    s = _bdot(nn_ref[...], zt_ref[0])
    sr = s[:, :S5_STATE, :].reshape(groups * S5_STATE, nseg)
    si = s[:, S5_STATE:, :].reshape(groups * S5_STATE, nseg)

    @pl.when(i == 0)
    def _():
        er_ref[...] = sr
        ei_ref[...] = si

    @pl.when(i > 0)
    def _():
        er, ei = er_ref[...], ei_ref[...]
        ar, ai = are_ref[...], aim_ref[...]
        er_ref[...] = ar * er - ai * ei + sr
        ei_ref[...] = ar * ei + ai * er + si


def _s5_pre_call(x3, g, nn, a_re, a_im):
    nseg, sl, dm = x3.shape
    groups = dm // S5_GROUP
    steps = sl // S5_SUB
    width = S5_SUB * S5_GROUP
    rows = groups * S5_STATE
    return pl.pallas_call(
        _s5_pre_kernel,
        grid=(steps,),
        in_specs=[pl.BlockSpec(memory_space=pl.ANY), _resident((1, dm)),
                  _resident(nn.shape), _resident((rows, nseg)), _resident((rows, nseg))],
        out_specs=[pl.BlockSpec((1, groups, width, nseg), lambda i: (i, 0, 0, 0)),
                   pl.BlockSpec((rows, nseg), lambda i: (0, 0)),
                   pl.BlockSpec((rows, nseg), lambda i: (0, 0))],
        out_shape=[jax.ShapeDtypeStruct((steps, groups, width, nseg), BF16),
                   jax.ShapeDtypeStruct((rows, nseg), F32),
                   jax.ShapeDtypeStruct((rows, nseg), F32)],
        scratch_shapes=[pltpu.VMEM((2, S5_SUB, nseg, dm), F32), pltpu.SemaphoreType.DMA((2,))],
        compiler_params=_cparams(("arbitrary",)),
        name="s5_pre",
    )(x3, g.reshape(1, dm), nn, a_re, a_im)


def _s5_stitch_kernel(er_ref, ei_ref, pr_ref, pi_ref, xr_ref, xi_ref, *, nsb):
    er, ei = er_ref[...], ei_ref[...]
    lane = lax.broadcasted_iota(jnp.int32, er.shape, 1) % nsb
    d, k = 1, 0
    while d < nsb:
        sr, si = pltpu.roll(er, d, axis=1), pltpu.roll(ei, d, axis=1)
        pr, pi = pr_ref[:, k:k + 1], pi_ref[:, k:k + 1]
        ok = lane >= d
        er, ei = (er + jnp.where(ok, pr * sr - pi * si, 0.0),
                  ei + jnp.where(ok, pr * si + pi * sr, 0.0))
        d, k = 2 * d, k + 1
    ok = lane >= 1
    xr_ref[...] = jnp.where(ok, pltpu.roll(er, 1, axis=1), 0.0)
    xi_ref[...] = jnp.where(ok, pltpu.roll(ei, 1, axis=1), 0.0)


def _s5_stitch_call(er, ei, p_re, p_im, nsb):
    shp = jax.ShapeDtypeStruct(er.shape, F32)
    return pl.pallas_call(
        functools.partial(_s5_stitch_kernel, nsb=nsb),
        out_shape=[shp, shp],
        compiler_params=pltpu.CompilerParams(vmem_limit_bytes=VMEM_LIMIT),
        name="s5_stitch",
    )(er, ei, p_re, p_im)


def _s5_core_kernel(zt_ref, tt_ref, nn_ref, mm_ref, are_ref, aim_ref, x0r_ref, x0i_ref, y_hbm,
                    xr_scr, xi_scr, ybuf, sem):
    pair, gb, _, nseg = zt_ref.shape
    cols = gb * S5_GROUP
    j, k = pl.program_id(0), pl.program_id(1)
    n = j * pl.num_programs(1) + k
    last = pl.num_programs(0) * pl.num_programs(1) - 1
    slot = n % 2

    def scatters(sl):
        return [pltpu.make_async_copy(
            ybuf.at[sl, s, t],
            y_hbm.at[:, (k * pair + s) * S5_SUB + t, pl.ds(j * cols, cols)],
            sem.at[sl]) for s in range(pair) for t in range(S5_SUB)]

    @pl.when(k == 0)
    def _():
        xr_scr[...] = x0r_ref[...]
        xi_scr[...] = x0i_ref[...]

    @pl.when(n >= 2)
    def _():
        for c in scatters(slot):
            c.wait()

    zt = jnp.concatenate([zt_ref[s] for s in range(pair)], axis=-1)
    s_all = _bdot(nn_ref[...], zt)
    ar, ai = are_ref[...], aim_ref[...]
    xr, xi = xr_scr[...], xi_scr[...]
    starts = []
    for s in range(pair):
        starts.append(jnp.concatenate([xr.reshape(gb, S5_STATE, nseg),
                                       xi.reshape(gb, S5_STATE, nseg)], axis=1))
        inc = s_all[:, :, s * nseg:(s + 1) * nseg]
        sr = inc[:, :S5_STATE, :].reshape(gb * S5_STATE, nseg)
        si = inc[:, S5_STATE:, :].reshape(gb * S5_STATE, nseg)
        xr, xi = ar * xr - ai * xi + sr, ar * xi + ai * xr + si
    xr_scr[...] = xr
    xi_scr[...] = xi
    xprev = jnp.concatenate(starts, axis=-1).astype(zt.dtype)
    y = _bdot(tt_ref[...], zt) + _bdot(mm_ref[...], xprev)
    for t in range(S5_SUB):
        blk = y[:, t * S5_GROUP:(t + 1) * S5_GROUP, :].reshape(cols, pair * nseg)
        bt = blk.T
        for s in range(pair):
            ybuf[slot, s, t] = bt[s * nseg:(s + 1) * nseg]
    for c in scatters(slot):
        c.start()

    @pl.when(n == last)
    def _():
        for c in scatters(slot):
            c.wait()

    @pl.when(jnp.logical_and(n == last, n >= 1))
    def _():
        for c in scatters(1 - slot):
            c.wait()


def _s5_core_call(zt, tt, nn, mm, a_re, a_im, x0r, x0i, dm):
    steps, groups, width, nseg = zt.shape
    gb = min(S5_GB, groups)
    pair = 2 if steps % 2 == 0 else 1
    rows = gb * S5_STATE
    per_gb = lambda shape: pl.BlockSpec(shape, lambda j, k: (j,) + (0,) * (len(shape) - 1))
    return pl.pallas_call(
        _s5_core_kernel,
        grid=(groups // gb, steps // pair),
        in_specs=[pl.BlockSpec((pair, gb, width, nseg), lambda j, k: (k, j, 0, 0)),
                  per_gb((gb, width, width)), per_gb((gb, 2 * S5_STATE, width)),
                  per_gb((gb, width, 2 * S5_STATE)),
                  per_gb((rows, nseg)), per_gb((rows, nseg)), per_gb((rows, nseg)),
                  per_gb((rows, nseg))],
        out_specs=pl.BlockSpec(memory_space=pl.ANY),
        out_shape=jax.ShapeDtypeStruct((nseg, steps * S5_SUB, dm), F32),
        scratch_shapes=[pltpu.VMEM((rows, nseg), F32), pltpu.VMEM((rows, nseg), F32),
                        pltpu.VMEM((2, pair, S5_SUB, nseg, gb * S5_GROUP), F32),
                        pltpu.SemaphoreType.DMA((2,))],
        compiler_params=_cparams(("arbitrary", "arbitrary")),
        name="s5_core",
    )(zt, tt, nn, mm, a_re, a_im, x0r, x0i)


def _ffn_rows(x, g_ref, wg_ref, wu_ref, wd_ref, acc_ref, hc):
    u = _rms(x, g_ref[...]).astype(BF16)
    hidden = wg_ref.shape[1]
    for j in range(hidden // hc):
        cols = slice(j * hc, (j + 1) * hc)
        gt = _dot(u, wg_ref[:, cols])
        up = _dot(u, wu_ref[:, cols])
        h = (jax.nn.silu(gt) * up).astype(BF16)
        part = _dot(h, wd_ref[cols, :])
        if j == 0:
            acc_ref[...] = part
        else:
            acc_ref[...] += part
    return x + acc_ref[...]


def _ffn_specs(row, dm, hidden):
    tile = pl.BlockSpec((row, dm), lambda i: (i, 0))
    weights = [_resident((1, dm)), _resident((dm, hidden)), _resident((dm, hidden)),
               _resident((hidden, dm))]
    return tile, weights


def _s5_ffn_kernel(x_ref, y_ref, g_ref, d_ref, w_ref, b_ref, fg_ref, wg_ref, wu_ref, wd_ref,
                   o_ref, acc_ref, *, hc):
    x = x_ref[...]
    u = _rms(x, g_ref[...])
    y = jax.nn.gelu(y_ref[...] + d_ref[...] * u)
    gate = jax.nn.sigmoid(_dot(y.astype(BF16), w_ref[...]) + b_ref[...])
    o_ref[...] = _ffn_rows(x + y * gate, fg_ref, wg_ref, wu_ref, wd_ref, acc_ref, hc)


def _s5_ffn_call(x, y, g, d, w, b, fg, wg, wu, wd, row, hc):
    t, dm = x.shape
    hidden = wg.shape[1]
    tile, weights = _ffn_specs(row, dm, hidden)
    return pl.pallas_call(
        functools.partial(_s5_ffn_kernel, hc=hc),
        grid=(t // row,),
        in_specs=[tile, tile, _resident((1, dm)), _resident((1, dm)), _resident((dm, dm)),
                  _resident((1, dm))] + weights,
        out_specs=tile,
        out_shape=jax.ShapeDtypeStruct((t, dm), F32),
        scratch_shapes=[pltpu.VMEM((row, dm), F32)],
        compiler_params=_cparams(("parallel",)),
        name="s5_ffn",
    )(x, y, g.reshape(1, dm), d.reshape(1, dm), w, b.reshape(1, dm), fg.reshape(1, dm), wg, wu, wd)


def _s5_tab_kernel(p_ref, q_ref, c1_ref, c2_ref, b1_ref, b2_ref, bs_ref, tt_ref, nn_ref, mm_ref):
    sub = S5_SUB
    width = sub * S5_GROUP
    lane = lax.broadcasted_iota(jnp.int32, (S5_GROUP, width), 1)
    sgn = jnp.where(lax.broadcasted_iota(jnp.int32, (1, 2 * S5_STATE), 1) < S5_STATE, 1.0, -1.0)
    for gi in range(p_ref.shape[0]):
        p, q = p_ref[gi], q_ref[gi]
        c1, c2, b1, b2 = c1_ref[gi], c2_ref[gi], b1_ref[gi], b2_ref[gi]
        cl = [c1 * p[k:k + 1] + c2 * q[k:k + 1] for k in range(sub + 1)]
        mm_ref[gi] = (jnp.concatenate(cl[1:], axis=0) * sgn).astype(mm_ref.dtype)
        r = jnp.dot(bs_ref[gi], jnp.concatenate(cl[:sub], axis=0).T,
                    precision=lax.Precision.HIGHEST, preferred_element_type=F32)
        rows = [r] + [jnp.where(lane >= S5_GROUP * s, pltpu.roll(r, S5_GROUP * s, axis=1), 0.0)
                      for s in range(1, sub)]
        tt_ref[gi] = jnp.concatenate(rows, axis=0).T.astype(tt_ref.dtype)
        nt = [b1 * p[sub - 1 - s:sub - s] + b2 * q[sub - 1 - s:sub - s] for s in range(sub)]
        nn_ref[gi] = jnp.concatenate(nt, axis=0).T.astype(nn_ref.dtype)


def _s5_tab_call(p, q, c1, c2, b1, b2, bs):
    groups = p.shape[0]
    gt = 4 if groups % 4 == 0 else 1
    width = S5_SUB * S5_GROUP
    st2 = 2 * S5_STATE
    blk = lambda a: pl.BlockSpec((gt,) + a.shape[1:], lambda i: (i, 0, 0))
    out = lambda r, c: pl.BlockSpec((gt, r, c), lambda i: (i, 0, 0))
    return pl.pallas_call(
        _s5_tab_kernel,
        grid=(groups // gt,),
        in_specs=[blk(a) for a in (p, q, c1, c2, b1, b2, bs)],
        out_specs=[out(width, width), out(st2, width), out(width, st2)],
        out_shape=[jax.ShapeDtypeStruct((groups, width, width), BF16),
                   jax.ShapeDtypeStruct((groups, st2, width), BF16),
                   jax.ShapeDtypeStruct((groups, width, st2), BF16)],
        compiler_params=_cparams(("parallel",)),
        name="s5_tab",
    )(p, q, c1, c2, b1, b2, bs)


def _s5_tables(lam_re, lam_im, log_step, b_re, b_im, c_re, c_im, seg_len, nsb, nseg):
    sub = S5_SUB
    step = jnp.exp(log_step)[:, None]
    dre, dim = lam_re * step, lam_im * step

    def power(k):
        kk = jnp.asarray(k, F32)[None, :, None]
        mag = jnp.exp(kk * dre[:, None, :])
        return mag * jnp.cos(kk * dim[:, None, :]), mag * jnp.sin(kk * dim[:, None, :])

    pw_re, pw_im = power(jnp.arange(sub + 1))
    den = lam_re * lam_re + lam_im * lam_im
    nr, ni = pw_re[:, 1] - 1.0, pw_im[:, 1]
    cf_re = ((nr * lam_re + ni * lam_im) / den)[:, None, :]
    cf_im = ((ni * lam_re - nr * lam_im) / den)[:, None, :]
    bt_re, bt_im = b_re.transpose(0, 2, 1), b_im.transpose(0, 2, 1)
    bb_re = cf_re * bt_re - cf_im * bt_im
    bb_im = cf_re * bt_im + cf_im * bt_re
    cat = lambda a, b: jnp.concatenate([a, b], axis=-1)
    tt, nn, mm = _s5_tab_call(cat(pw_re, pw_im), cat(pw_im, pw_re), cat(c_re, c_re),
                              cat(-c_im, c_im), cat(bb_re, bb_re), cat(-bb_im, bb_im),
                              cat(bb_re, -bb_im))
    a_re = jnp.broadcast_to(pw_re[:, sub].reshape(-1, 1), (pw_re[:, sub].size, nseg))
    a_im = jnp.broadcast_to(pw_im[:, sub].reshape(-1, 1), (pw_im[:, sub].size, nseg))
    nd = max(1, (nsb - 1).bit_length())
    p_re, p_im = power(seg_len * (2 ** jnp.arange(nd)))
    p_re = p_re.transpose(0, 2, 1).reshape(-1, nd)
    p_im = p_im.transpose(0, 2, 1).reshape(-1, nd)
    return tt, nn, mm, a_re, a_im, p_re, p_im


def _s5_layer(x, batch, seq, norm_w, lam_re, lam_im, log_step, b_re, b_im, c_re, c_im, d, glu_w,
              glu_b, ffn_g, ffn_wg, ffn_wu, ffn_wd, row, hc):
    t, dm = x.shape
    nseg = V7X_LANES
    seg_len = t // nseg
    assert t % nseg == 0 and seg_len % S5_SUB == 0 and seq % seg_len == 0
    nsb = seq // seg_len
    tt, nn, mm, a_re, a_im, p_re, p_im = _s5_tables(
        lam_re, lam_im, log_step, b_re, b_im, c_re, c_im, seg_len, nsb, nseg)
    zt, er, ei = _s5_pre_call(x.reshape(nseg, seg_len, dm), norm_w, nn, a_re, a_im)
    x0r, x0i = _s5_stitch_call(er, ei, p_re, p_im, nsb)
    y = _s5_core_call(zt, tt, nn, mm, a_re, a_im, x0r, x0i, dm).reshape(t, dm)
    return _s5_ffn_call(x, y, norm_w, d.reshape(-1), glu_w.astype(BF16), glu_b, ffn_g,
                        ffn_wg.astype(BF16), ffn_wu.astype(BF16), ffn_wd.astype(BF16), row, hc)


def _ret_kernel(x_ref, g_ref, w_ref, cr_ref, sr_ref, cb_ref, sb_ref, gn_ref, dm_ref, qd_ref, kd_ref, cd_ref,
                y_ref, state_ref, *, heads, blk):
    @pl.when(pl.program_id(1) == 0)
    def _():
        state_ref[...] = jnp.zeros_like(state_ref)

    u = _rms(x_ref[...], g_ref[...]).astype(BF16)
    cb = cb_ref[pl.ds(pl.program_id(1), 1), :]
    sb = sb_ref[pl.ds(pl.program_id(1), 1), :]
    cos = cr_ref[...] * cb - sr_ref[...] * sb
    sin = sr_ref[...] * cb + cr_ref[...] * sb
    half = RET_QK_DIM // 2
    qk_w = heads * RET_QK_DIM
    v_w = heads * RET_V_DIM
    k_scale = RET_QK_DIM ** -0.5
    row = x_ref.shape[0]

    def rotary(tq):
        t1, t2 = tq[:, :half], tq[:, half:]
        return jnp.concatenate([t1 * cos - t2 * sin, t1 * sin + t2 * cos], axis=-1)

    for h in range(heads):
        cq = h * RET_QK_DIM
        cv = 2 * qk_w + h * RET_V_DIM
        q = rotary(_dot(u, w_ref[:, cq:cq + RET_QK_DIM])).astype(BF16)
        k = rotary(_dot(u, w_ref[:, qk_w + cq:qk_w + cq + RET_QK_DIM]) * k_scale)
        v = _dot(u, w_ref[:, cv:cv + RET_V_DIM]).astype(BF16)
        gt = _dot(u, w_ref[:, v_w + cv:v_w + cv + RET_V_DIM])
        gn = gn_ref[:, h * RET_V_DIM:(h + 1) * RET_V_DIM]
        for r0 in range(0, row, blk):
            qs, ks, vs = q[r0:r0 + blk], k[r0:r0 + blk], v[r0:r0 + blk]
            s = lax.dot_general(qs, ks.astype(BF16), (((1,), (1,)), ((), ())),
                                preferred_element_type=F32)
            o = _dot((s * dm_ref[h]).astype(BF16), vs)
            st = state_ref[h]
            o = o + _dot(qs, st.astype(BF16)) * qd_ref[h]
            kd = (ks * kd_ref[h]).astype(BF16)
            state_ref[h] = st * cd_ref[h] + lax.dot_general(
                kd, vs, (((0,), (0,)), ((), ())), preferred_element_type=F32)
            mean = jnp.mean(o, axis=-1, keepdims=True)
            cen = o - mean
            var = jnp.mean(jnp.square(cen), axis=-1, keepdims=True)
            on = cen * lax.rsqrt(var + NORM_EPS) * gn
            y_ref[r0:r0 + blk, h * RET_V_DIM:(h + 1) * RET_V_DIM] = (
                jax.nn.silu(gt[r0:r0 + blk]) * on).astype(y_ref.dtype)


def _ret_call(x, g, w, rot, gn_w, dmask, qdec, kdec, cdec, batch, seq, heads, row, blk):
    t, dm = x.shape
    v_w = heads * RET_V_DIM
    nb = seq // row
    kern = functools.partial(_ret_kernel, heads=heads, blk=blk)
    rows = lambda width: pl.BlockSpec((row, width), lambda b, i: (b * nb + i, 0))
    tabs = [_resident(a.shape) for a in rot]
    return pl.pallas_call(
        kern,
        grid=(batch, nb),
        in_specs=[rows(dm), _resident((1, dm)), _resident(w.shape)] + tabs + [
                  _resident((1, v_w)), _resident(dmask.shape), _resident(qdec.shape),
                  _resident(kdec.shape), _resident(cdec.shape)],
        out_specs=rows(v_w),
        out_shape=jax.ShapeDtypeStruct((t, v_w), BF16),
        scratch_shapes=[pltpu.VMEM((heads, RET_QK_DIM, RET_V_DIM), F32)],
        compiler_params=_cparams(("parallel", "arbitrary")),
        name="ret_mix",
    )(x, g.reshape(1, dm), w, *rot, gn_w.reshape(1, v_w), dmask, qdec, kdec, cdec)


def _ret_decays(heads, blk):
    log_gamma = jnp.log1p(-jnp.exp2(-5.0 - jnp.arange(heads, dtype=F32)))
    pos = jnp.arange(blk, dtype=F32)
    diff = pos[:, None] - pos[None, :]
    cn = (jnp.arange(blk) // CHUNK)[:, None]
    cm = (jnp.arange(blk) // CHUNK)[None, :]
    expo = jnp.where(cn == cm, jnp.abs(diff), diff)
    dmask = jnp.where((cm <= cn)[None], jnp.exp(log_gamma[:, None, None] * expo[None]), 0.0)
    qdec = jnp.exp((pos[None, :] + 1.0) * log_gamma[:, None])[..., None]
    kdec = jnp.exp((blk - 1.0 - pos)[None, :] * log_gamma[:, None])[..., None]
    cdec = jnp.exp(blk * log_gamma)[:, None, None]
    return dmask, qdec, kdec, cdec


def _rotary_tables(seq, row):
    inv_freq = 1.0 / (ROPE_BASE ** jnp.linspace(0.0, 1.0, RET_QK_DIM // 2, dtype=F32))
    ang_r = jnp.arange(row, dtype=F32)[:, None] * inv_freq[None, :]
    ang_b = (jnp.arange(seq // row, dtype=F32) * row)[:, None] * inv_freq[None, :]
    return jnp.cos(ang_r), jnp.sin(ang_r), jnp.cos(ang_b), jnp.sin(ang_b)


def _ret_ffn_kernel(x_ref, y_ref, w_ref, fg_ref, wg_ref, wu_ref, wd_ref, fin_ref, o_ref, acc_ref,
                    *, hc):
    x1 = x_ref[...] + _dot(y_ref[...], w_ref[...])
    out = _ffn_rows(x1, fg_ref, wg_ref, wu_ref, wd_ref, acc_ref, hc)
    o_ref[...] = _rms(out, fin_ref[...])


def _ret_ffn_call(x, y, w, fg, wg, wu, wd, fin, row, hc):
    t, dm = x.shape
    hidden = wg.shape[1]
    kdim = y.shape[1]
    tile, weights = _ffn_specs(row, dm, hidden)
    return pl.pallas_call(
        functools.partial(_ret_ffn_kernel, hc=hc),
        grid=(t // row,),
        in_specs=[tile, pl.BlockSpec((row, kdim), lambda i: (i, 0)), _resident((kdim, dm))]
        + weights + [_resident((1, dm))],
        out_specs=tile,
        out_shape=jax.ShapeDtypeStruct((t, dm), F32),
        scratch_shapes=[pltpu.VMEM((row, dm), F32)],
        compiler_params=_cparams(("parallel",)),
        name="ret_ffn",
    )(x, y, w, fg.reshape(1, dm), wg, wu, wd, fin.reshape(1, dm))


def _ret_layer(x, batch, seq, norm_w, w_qkvg, gn_w, w_o, ffn_g, ffn_wg, ffn_wu, ffn_wd, fin,
               row, blk, hc):
    dm = x.shape[1]
    heads = dm // RET_QK_DIM
    rot = _rotary_tables(seq, row)
    dmask, qdec, kdec, cdec = _ret_decays(heads, blk)
    y = _ret_call(x, norm_w, w_qkvg.astype(BF16), rot, gn_w, dmask, qdec, kdec, cdec,
                  batch, seq, heads, row, blk)
    return _ret_ffn_call(x, y, w_o.astype(BF16), ffn_g, ffn_wg.astype(BF16), ffn_wu.astype(BF16),
                         ffn_wd.astype(BF16), fin, row, hc)


def kernel(x, s5_norm, s5_lambda_re, s5_lambda_im, s5_log_step, s5_b_re, s5_b_im, s5_c_re, s5_c_im,
           s5_d, s5_glu_w, s5_glu_b, ret_norm, ret_w_qkvg, ret_gn_w, ret_w_o, ffn_norm, ffn_w_gate,
           ffn_w_up, ffn_w_down, final_norm):
    batch, seq, dm = x.shape
    row, blk, hc = _tiles(seq)
    assert ffn_norm.shape[0] == 2 and s5_norm.shape[0] == 1 and ret_norm.shape[0] == 1
    h = x.reshape(batch * seq, dm)
    h = _s5_layer(h, batch, seq, s5_norm[0], s5_lambda_re[0], s5_lambda_im[0], s5_log_step[0],
                  s5_b_re[0], s5_b_im[0], s5_c_re[0], s5_c_im[0], s5_d[0], s5_glu_w[0],
                  s5_glu_b[0], ffn_norm[0], ffn_w_gate[0], ffn_w_up[0], ffn_w_down[0], row, hc)
    h = _ret_layer(h, batch, seq, ret_norm[0], ret_w_qkvg[0], ret_gn_w[0], ret_w_o[0],
                   ffn_norm[1], ffn_w_gate[1], ffn_w_up[1], ffn_w_down[1], final_norm,
                   row, blk, hc)
    return h.reshape(batch, seq, dm)
```

```python
import functools

import jax
import jax.numpy as jnp
from jax import lax
from jax.experimental import pallas as pl
from jax.experimental.pallas import tpu as pltpu

F32 = jnp.float32
BF16 = jnp.bfloat16

NORM_EPS = 1e-6
CHUNK = 64
S5_GROUP = 16
S5_STATE = 64
S5_SUB = 16
S5_GB = 16
RET_QK_DIM = 256
RET_V_DIM = 512
ROPE_BASE = 10000.0

V7X_VMEM_BYTES = 64 * 1024 * 1024
V7X_LANES = 128
VMEM_LIMIT = 56 * 1024 * 1024


def _tiles(seq):
    row = min(512, seq)
    row_out = min(1024, seq)
    ret = min(256, seq)
    hc = 256
    assert seq % row == 0 and seq % row_out == 0 and row % ret == 0 and ret % CHUNK == 0
    return row, row_out, ret, hc


def _cparams(sem):
    return pltpu.CompilerParams(dimension_semantics=sem, vmem_limit_bytes=VMEM_LIMIT)


def _resident(shape):
    nd = len(shape)
    return pl.BlockSpec(shape, lambda *_: (0,) * nd, pipeline_mode=pl.Buffered(1))


def _rms(xf, g):
    ms = jnp.mean(jnp.square(xf), axis=-1, keepdims=True)
    return xf * lax.rsqrt(ms + NORM_EPS) * g


def _dot(a, b):
    return jnp.dot(a, b, preferred_element_type=F32)


def _bdot(a, b):
    return lax.dot_general(a, b, (((2,), (1,)), ((0,), (0,))), preferred_element_type=F32)


def _s5_pre_kernel(x_hbm, g_ref, nn_ref, are_ref, aim_ref, zt_ref, er_ref, ei_ref, xs, sem):
    groups = zt_ref.shape[1]
    nseg = zt_ref.shape[3]
    i = pl.program_id(0)
    steps = pl.num_programs(0)

    def gathers(step, slot):
        return [pltpu.make_async_copy(x_hbm.at[:, step * S5_SUB + t, :], xs.at[slot, t],
                                      sem.at[slot]) for t in range(S5_SUB)]

    @pl.when(i == 0)
    def _():
        for c in gathers(0, 0):
            c.start()

    @pl.when(i + 1 < steps)
    def _():
        for c in gathers(i + 1, (i + 1) % 2):
            c.start()

    slot = i % 2
    for c in gathers(i, slot):
        c.wait()
    g = g_ref[...]
    for t in range(S5_SUB):
        ut = _rms(xs[slot, t], g)
        zt_ref[0, :, t * S5_GROUP:(t + 1) * S5_GROUP, :] = (
            ut.T.reshape(groups, S5_GROUP, nseg).astype(zt_ref.dtype))
    s = _bdot(nn_ref[...], zt_ref[0])
    sr = s[:, :S5_STATE, :].reshape(groups * S5_STATE, nseg)
    si = s[:, S5_STATE:, :].reshape(groups * S5_STATE, nseg)

    @pl.when(i == 0)
    def _():
        er_ref[...] = sr
        ei_ref[...] = si

    @pl.when(i > 0)
    def _():
        er, ei = er_ref[...], ei_ref[...]
        ar, ai = are_ref[...], aim_ref[...]
        er_ref[...] = ar * er - ai * ei + sr
        ei_ref[...] = ar * ei + ai * er + si


def _s5_pre_call(x3, g, nn, a_re, a_im):
    nseg, sl, dm = x3.shape
    groups = dm // S5_GROUP
    steps = sl // S5_SUB
    width = S5_SUB * S5_GROUP
    rows = groups * S5_STATE
    return pl.pallas_call(
        _s5_pre_kernel,
        grid=(steps,),
        in_specs=[pl.BlockSpec(memory_space=pl.ANY), _resident((1, dm)),
                  _resident(nn.shape), _resident((rows, nseg)), _resident((rows, nseg))],
        out_specs=[pl.BlockSpec((1, groups, width, nseg), lambda i: (i, 0, 0, 0)),
                   pl.BlockSpec((rows, nseg), lambda i: (0, 0)),
                   pl.BlockSpec((rows, nseg), lambda i: (0, 0))],
        out_shape=[jax.ShapeDtypeStruct((steps, groups, width, nseg), BF16),
                   jax.ShapeDtypeStruct((rows, nseg), F32),
                   jax.ShapeDtypeStruct((rows, nseg), F32)],
        scratch_shapes=[pltpu.VMEM((2, S5_SUB, nseg, dm), F32), pltpu.SemaphoreType.DMA((2,))],
        compiler_params=_cparams(("arbitrary",)),
        name="s5_pre",
    )(x3, g.reshape(1, dm), nn, a_re, a_im)


def _s5_stitch(er_ref, ei_ref, pr_ref, pi_ref, xr_ref, xi_ref, nsb):
    er, ei = er_ref[...], ei_ref[...]
    lane = lax.broadcasted_iota(jnp.int32, er.shape, 1) % nsb
    d, k = 1, 0
    while d < nsb:
        sr, si = pltpu.roll(er, d, axis=1), pltpu.roll(ei, d, axis=1)
        pr, pi = pr_ref[:, k:k + 1], pi_ref[:, k:k + 1]
        ok = lane >= d
        er, ei = (er + jnp.where(ok, pr * sr - pi * si, 0.0),
                  ei + jnp.where(ok, pr * si + pi * sr, 0.0))
        d, k = 2 * d, k + 1
    ok = lane >= 1
    xr_ref[...] = jnp.where(ok, pltpu.roll(er, 1, axis=1), 0.0)
    xi_ref[...] = jnp.where(ok, pltpu.roll(ei, 1, axis=1), 0.0)


def _s5_core_kernel(zt_ref, tt_ref, nn_ref, mm_ref, are_ref, aim_ref, er_ref, ei_ref, pr_ref,
                    pi_ref, y_hbm, xr_scr, xi_scr, ybuf, sem, *, nsb):
    pair, gb, _, nseg = zt_ref.shape
    cols = gb * S5_GROUP
    j, k = pl.program_id(0), pl.program_id(1)
    n = j * pl.num_programs(1) + k
    last = pl.num_programs(0) * pl.num_programs(1) - 1
    slot = n % 2

    def scatters(sl):
        return [pltpu.make_async_copy(
            ybuf.at[sl, s, t],
            y_hbm.at[:, (k * pair + s) * S5_SUB + t, pl.ds(j * cols, cols)],
            sem.at[sl]) for s in range(pair) for t in range(S5_SUB)]

    @pl.when(k == 0)
    def _():
        _s5_stitch(er_ref, ei_ref, pr_ref, pi_ref, xr_scr, xi_scr, nsb)

    @pl.when(n >= 2)
    def _():
        for c in scatters(slot):
            c.wait()

    zt = jnp.concatenate([zt_ref[s] for s in range(pair)], axis=-1)
    s_all = _bdot(nn_ref[...], zt)
    ar, ai = are_ref[...], aim_ref[...]
    xr, xi = xr_scr[...], xi_scr[...]
    starts = []
    for s in range(pair):
        starts.append(jnp.concatenate([xr.reshape(gb, S5_STATE, nseg),
                                       xi.reshape(gb, S5_STATE, nseg)], axis=1))
        inc = s_all[:, :, s * nseg:(s + 1) * nseg]
        sr = inc[:, :S5_STATE, :].reshape(gb * S5_STATE, nseg)
        si = inc[:, S5_STATE:, :].reshape(gb * S5_STATE, nseg)
        xr, xi = ar * xr - ai * xi + sr, ar * xi + ai * xr + si
    xr_scr[...] = xr
    xi_scr[...] = xi
    xprev = jnp.concatenate(starts, axis=-1).astype(zt.dtype)
    y = _bdot(tt_ref[...], zt) + _bdot(mm_ref[...], xprev)
    for t in range(S5_SUB):
        blk = y[:, t * S5_GROUP:(t + 1) * S5_GROUP, :].reshape(cols, pair * nseg)
        bt = blk.T
        for s in range(pair):
            ybuf[slot, s, t] = bt[s * nseg:(s + 1) * nseg]
    for c in scatters(slot):
        c.start()

    @pl.when(n == last)
    def _():
        for c in scatters(slot):
            c.wait()

    @pl.when(jnp.logical_and(n == last, n >= 1))
    def _():
        for c in scatters(1 - slot):
            c.wait()


def _s5_core_call(zt, tt, nn, mm, a_re, a_im, er, ei, p_re, p_im, nsb, dm):
    steps, groups, width, nseg = zt.shape
    gb = min(S5_GB, groups)
    pair = 2 if steps % 2 == 0 else 1
    rows = gb * S5_STATE
    per_gb = lambda shape: pl.BlockSpec(shape, lambda j, k: (j,) + (0,) * (len(shape) - 1))
    return pl.pallas_call(
        functools.partial(_s5_core_kernel, nsb=nsb),
        grid=(groups // gb, steps // pair),
        in_specs=[pl.BlockSpec((pair, gb, width, nseg), lambda j, k: (k, j, 0, 0)),
                  per_gb((gb, width, width)), per_gb((gb, 2 * S5_STATE, width)),
                  per_gb((gb, width, 2 * S5_STATE)),
                  per_gb((rows, nseg)), per_gb((rows, nseg)), per_gb((rows, nseg)),
                  per_gb((rows, nseg)), per_gb((rows, p_re.shape[1])),
                  per_gb((rows, p_im.shape[1]))],
        out_specs=pl.BlockSpec(memory_space=pl.ANY),
        out_shape=jax.ShapeDtypeStruct((nseg, steps * S5_SUB, dm), F32),
        scratch_shapes=[pltpu.VMEM((rows, nseg), F32), pltpu.VMEM((rows, nseg), F32),
                        pltpu.VMEM((2, pair, S5_SUB, nseg, gb * S5_GROUP), F32),
                        pltpu.SemaphoreType.DMA((2,))],
        compiler_params=_cparams(("arbitrary", "arbitrary")),
        name="s5_core",
    )(zt, tt, nn, mm, a_re, a_im, er, ei, p_re, p_im)


def _ffn_rows(x, g_ref, wg_ref, wu_ref, wd_ref, acc_ref, hc):
    u = _rms(x, g_ref[...]).astype(BF16)
    hidden = wg_ref.shape[1]
    for j in range(hidden // hc):
        cols = slice(j * hc, (j + 1) * hc)
        gt = _dot(u, wg_ref[:, cols])
        up = _dot(u, wu_ref[:, cols])
        h = (jax.nn.silu(gt) * up).astype(BF16)
        part = _dot(h, wd_ref[cols, :])
        if j == 0:
            acc_ref[...] = part
        else:
            acc_ref[...] += part
    return x + acc_ref[...]


def _ffn_specs(row, dm, hidden):
    tile = pl.BlockSpec((row, dm), lambda i: (i, 0))
    weights = [_resident((1, dm)), _resident((dm, hidden)), _resident((dm, hidden)),
               _resident((hidden, dm))]
    return tile, weights


def _s5_ffn_kernel(x_ref, y_ref, g_ref, d_ref, w_ref, b_ref, fg_ref, wg_ref, wu_ref, wd_ref,
                   o_ref, acc_ref, *, hc):
    x = x_ref[...]
    u = _rms(x, g_ref[...])
    y = jax.nn.gelu(y_ref[...] + d_ref[...] * u)
    gate = jax.nn.sigmoid(_dot(y.astype(BF16), w_ref[...]) + b_ref[...])
    o_ref[...] = _ffn_rows(x + y * gate, fg_ref, wg_ref, wu_ref, wd_ref, acc_ref, hc)


def _s5_ffn_call(x, y, g, d, w, b, fg, wg, wu, wd, row, hc):
    t, dm = x.shape
    hidden = wg.shape[1]
    tile, weights = _ffn_specs(row, dm, hidden)
    return pl.pallas_call(
        functools.partial(_s5_ffn_kernel, hc=hc),
        grid=(t // row,),
        in_specs=[tile, tile, _resident((1, dm)), _resident((1, dm)), _resident((dm, dm)),
                  _resident((1, dm))] + weights,
        out_specs=tile,
        out_shape=jax.ShapeDtypeStruct((t, dm), F32),
        scratch_shapes=[pltpu.VMEM((row, dm), F32)],
        compiler_params=_cparams(("parallel",)),
        name="s5_ffn",
    )(x, y, g.reshape(1, dm), d.reshape(1, dm), w, b.reshape(1, dm), fg.reshape(1, dm), wg, wu, wd)


def _s5_tab_kernel(p_ref, q_ref, c1_ref, c2_ref, b1_ref, b2_ref, bs_ref, tt_ref, nn_ref, mm_ref):
    sub = S5_SUB
    width = sub * S5_GROUP
    lane = lax.broadcasted_iota(jnp.int32, (S5_GROUP, width), 1)
    sgn = jnp.where(lax.broadcasted_iota(jnp.int32, (1, 2 * S5_STATE), 1) < S5_STATE, 1.0, -1.0)
    for gi in range(p_ref.shape[0]):
        p, q = p_ref[gi], q_ref[gi]
        c1, c2, b1, b2 = c1_ref[gi], c2_ref[gi], b1_ref[gi], b2_ref[gi]
        cl = [c1 * p[k:k + 1] + c2 * q[k:k + 1] for k in range(sub + 1)]
        mm_ref[gi] = (jnp.concatenate(cl[1:], axis=0) * sgn).astype(mm_ref.dtype)
        r = jnp.dot(bs_ref[gi], jnp.concatenate(cl[:sub], axis=0).T,
                    precision=lax.Precision.HIGHEST, preferred_element_type=F32)
        rows = [r] + [jnp.where(lane >= S5_GROUP * s, pltpu.roll(r, S5_GROUP * s, axis=1), 0.0)
                      for s in range(1, sub)]
        tt_ref[gi] = jnp.concatenate(rows, axis=0).T.astype(tt_ref.dtype)
        nt = [b1 * p[sub - 1 - s:sub - s] + b2 * q[sub - 1 - s:sub - s] for s in range(sub)]
        nn_ref[gi] = jnp.concatenate(nt, axis=0).T.astype(nn_ref.dtype)


def _s5_tab_call(p, q, c1, c2, b1, b2, bs):
    groups = p.shape[0]
    gt = 4 if groups % 4 == 0 else 1
    width = S5_SUB * S5_GROUP
    st2 = 2 * S5_STATE
    blk = lambda a: pl.BlockSpec((gt,) + a.shape[1:], lambda i: (i, 0, 0))
    out = lambda r, c: pl.BlockSpec((gt, r, c), lambda i: (i, 0, 0))
    return pl.pallas_call(
        _s5_tab_kernel,
        grid=(groups // gt,),
        in_specs=[blk(a) for a in (p, q, c1, c2, b1, b2, bs)],
        out_specs=[out(width, width), out(st2, width), out(width, st2)],
        out_shape=[jax.ShapeDtypeStruct((groups, width, width), BF16),
                   jax.ShapeDtypeStruct((groups, st2, width), BF16),
                   jax.ShapeDtypeStruct((groups, width, st2), BF16)],
        compiler_params=_cparams(("parallel",)),
        name="s5_tab",
    )(p, q, c1, c2, b1, b2, bs)


def _s5_tables(lam_re, lam_im, log_step, b_re, b_im, c_re, c_im, seg_len, nsb, nseg):
    sub = S5_SUB
    step = jnp.exp(log_step)[:, None]
    dre, dim = lam_re * step, lam_im * step

    def power(k):
        kk = jnp.asarray(k, F32)[None, :, None]
        mag = jnp.exp(kk * dre[:, None, :])
        return mag * jnp.cos(kk * dim[:, None, :]), mag * jnp.sin(kk * dim[:, None, :])

    pw_re, pw_im = power(jnp.arange(sub + 1))
    den = lam_re * lam_re + lam_im * lam_im
    nr, ni = pw_re[:, 1] - 1.0, pw_im[:, 1]
    cf_re = ((nr * lam_re + ni * lam_im) / den)[:, None, :]
    cf_im = ((ni * lam_re - nr * lam_im) / den)[:, None, :]
    bt_re, bt_im = b_re.transpose(0, 2, 1), b_im.transpose(0, 2, 1)
    bb_re = cf_re * bt_re - cf_im * bt_im
    bb_im = cf_re * bt_im + cf_im * bt_re
    cat = lambda a, b: jnp.concatenate([a, b], axis=-1)
    tt, nn, mm = _s5_tab_call(cat(pw_re, pw_im), cat(pw_im, pw_re), cat(c_re, c_re),
                              cat(-c_im, c_im), cat(bb_re, bb_re), cat(-bb_im, bb_im),
                              cat(bb_re, -bb_im))
    a_re = jnp.broadcast_to(pw_re[:, sub].reshape(-1, 1), (pw_re[:, sub].size, nseg))
    a_im = jnp.broadcast_to(pw_im[:, sub].reshape(-1, 1), (pw_im[:, sub].size, nseg))
    nd = max(1, (nsb - 1).bit_length())
    p_re, p_im = power(seg_len * (2 ** jnp.arange(nd)))
    p_re = p_re.transpose(0, 2, 1).reshape(-1, nd)
    p_im = p_im.transpose(0, 2, 1).reshape(-1, nd)
    return tt, nn, mm, a_re, a_im, p_re, p_im


def _s5_layer(x, batch, seq, norm_w, lam_re, lam_im, log_step, b_re, b_im, c_re, c_im, d, glu_w,
              glu_b, ffn_g, ffn_wg, ffn_wu, ffn_wd, row, hc):
    t, dm = x.shape
    nseg = V7X_LANES
    seg_len = t // nseg
    assert t % nseg == 0 and seg_len % S5_SUB == 0 and seq % seg_len == 0
    nsb = seq // seg_len
    tt, nn, mm, a_re, a_im, p_re, p_im = _s5_tables(
        lam_re, lam_im, log_step, b_re, b_im, c_re, c_im, seg_len, nsb, nseg)
    zt, er, ei = _s5_pre_call(x.reshape(nseg, seg_len, dm), norm_w, nn, a_re, a_im)
    y = _s5_core_call(zt, tt, nn, mm, a_re, a_im, er, ei, p_re, p_im, nsb, dm).reshape(t, dm)
    return _s5_ffn_call(x, y, norm_w, d.reshape(-1), glu_w.astype(BF16), glu_b, ffn_g,
                        ffn_wg.astype(BF16), ffn_wu.astype(BF16), ffn_wd.astype(BF16), row, hc)


def _ret_kernel(x_ref, g_ref, w_ref, cr_ref, sr_ref, cb_ref, sb_ref, gn_ref, dm_ref, qd_ref, kd_ref, cd_ref,
                y_ref, state_ref, *, heads, blk):
    @pl.when(pl.program_id(1) == 0)
    def _():
        state_ref[...] = jnp.zeros_like(state_ref)

    u = _rms(x_ref[...], g_ref[...]).astype(BF16)
    cb = cb_ref[pl.ds(pl.program_id(1), 1), :]
    sb = sb_ref[pl.ds(pl.program_id(1), 1), :]
    cos = cr_ref[...] * cb - sr_ref[...] * sb
    sin = sr_ref[...] * cb + cr_ref[...] * sb
    half = RET_QK_DIM // 2
    qk_w = heads * RET_QK_DIM
    v_w = heads * RET_V_DIM
    k_scale = RET_QK_DIM ** -0.5
    row = x_ref.shape[0]

    def rotary(tq):
        t1, t2 = tq[:, :half], tq[:, half:]
        return jnp.concatenate([t1 * cos - t2 * sin, t1 * sin + t2 * cos], axis=-1)

    for h in range(heads):
        cq = h * RET_QK_DIM
        cv = 2 * qk_w + h * RET_V_DIM
        q = rotary(_dot(u, w_ref[:, cq:cq + RET_QK_DIM])).astype(BF16)
        k = rotary(_dot(u, w_ref[:, qk_w + cq:qk_w + cq + RET_QK_DIM]) * k_scale)
        v = _dot(u, w_ref[:, cv:cv + RET_V_DIM]).astype(BF16)
        gt = _dot(u, w_ref[:, v_w + cv:v_w + cv + RET_V_DIM])
        gn = gn_ref[:, h * RET_V_DIM:(h + 1) * RET_V_DIM]
        for r0 in range(0, row, blk):
            qs, ks, vs = q[r0:r0 + blk], k[r0:r0 + blk], v[r0:r0 + blk]
            s = lax.dot_general(qs, ks.astype(BF16), (((1,), (1,)), ((), ())),
                                preferred_element_type=F32)
            o = _dot((s * dm_ref[h]).astype(BF16), vs)
            st = state_ref[h]
            o = o + _dot(qs, st.astype(BF16)) * qd_ref[h]
            kd = (ks * kd_ref[h]).astype(BF16)
            state_ref[h] = st * cd_ref[h] + lax.dot_general(
                kd, vs, (((0,), (0,)), ((), ())), preferred_element_type=F32)
            mean = jnp.mean(o, axis=-1, keepdims=True)
            cen = o - mean
            var = jnp.mean(jnp.square(cen), axis=-1, keepdims=True)
            on = cen * lax.rsqrt(var + NORM_EPS) * gn
            y_ref[r0:r0 + blk, h * RET_V_DIM:(h + 1) * RET_V_DIM] = (
                jax.nn.silu(gt[r0:r0 + blk]) * on).astype(y_ref.dtype)


def _ret_call(x, g, w, rot, gn_w, dmask, qdec, kdec, cdec, batch, seq, heads, row, blk):
    t, dm = x.shape
    v_w = heads * RET_V_DIM
    nb = seq // row
    kern = functools.partial(_ret_kernel, heads=heads, blk=blk)
    rows = lambda width: pl.BlockSpec((row, width), lambda b, i: (b * nb + i, 0))
    tabs = [_resident(a.shape) for a in rot]
    return pl.pallas_call(
        kern,
        grid=(batch, nb),
        in_specs=[rows(dm), _resident((1, dm)), _resident(w.shape)] + tabs + [
                  _resident((1, v_w)), _resident(dmask.shape), _resident(qdec.shape),
                  _resident(kdec.shape), _resident(cdec.shape)],
        out_specs=rows(v_w),
        out_shape=jax.ShapeDtypeStruct((t, v_w), BF16),
        scratch_shapes=[pltpu.VMEM((heads, RET_QK_DIM, RET_V_DIM), F32)],
        compiler_params=_cparams(("parallel", "arbitrary")),
        name="ret_mix",
    )(x, g.reshape(1, dm), w, *rot, gn_w.reshape(1, v_w), dmask, qdec, kdec, cdec)


def _ret_decays(heads, blk):
    log_gamma = jnp.log1p(-jnp.exp2(-5.0 - jnp.arange(heads, dtype=F32)))
    pos = jnp.arange(blk, dtype=F32)
    diff = pos[:, None] - pos[None, :]
    cn = (jnp.arange(blk) // CHUNK)[:, None]
    cm = (jnp.arange(blk) // CHUNK)[None, :]
    expo = jnp.where(cn == cm, jnp.abs(diff), diff)
    dmask = jnp.where((cm <= cn)[None], jnp.exp(log_gamma[:, None, None] * expo[None]), 0.0)
    qdec = jnp.exp((pos[None, :] + 1.0) * log_gamma[:, None])[..., None]
    kdec = jnp.exp((blk - 1.0 - pos)[None, :] * log_gamma[:, None])[..., None]
    cdec = jnp.exp(blk * log_gamma)[:, None, None]
    return dmask, qdec, kdec, cdec


def _rotary_tables(seq, row):
    inv_freq = 1.0 / (ROPE_BASE ** jnp.linspace(0.0, 1.0, RET_QK_DIM // 2, dtype=F32))
    ang_r = jnp.arange(row, dtype=F32)[:, None] * inv_freq[None, :]
    ang_b = (jnp.arange(seq // row, dtype=F32) * row)[:, None] * inv_freq[None, :]
    return jnp.cos(ang_r), jnp.sin(ang_r), jnp.cos(ang_b), jnp.sin(ang_b)


def _ret_ffn_kernel(x_ref, y_ref, w_ref, fg_ref, wg_ref, wu_ref, wd_ref, fin_ref, o_ref, acc_ref,
                    *, hc):
    x1 = x_ref[...] + _dot(y_ref[...], w_ref[...])
    out = _ffn_rows(x1, fg_ref, wg_ref, wu_ref, wd_ref, acc_ref, hc)
    o_ref[...] = _rms(out, fin_ref[...])


def _ret_ffn_call(x, y, w, fg, wg, wu, wd, fin, row, hc):
    t, dm = x.shape
    hidden = wg.shape[1]
    kdim = y.shape[1]
    tile, weights = _ffn_specs(row, dm, hidden)
    return pl.pallas_call(
        functools.partial(_ret_ffn_kernel, hc=hc),
        grid=(t // row,),
        in_specs=[tile, pl.BlockSpec((row, kdim), lambda i: (i, 0)), _resident((kdim, dm))]
        + weights + [_resident((1, dm))],
        out_specs=tile,
        out_shape=jax.ShapeDtypeStruct((t, dm), F32),
        scratch_shapes=[pltpu.VMEM((row, dm), F32)],
        compiler_params=_cparams(("parallel",)),
        name="ret_ffn",
    )(x, y, w, fg.reshape(1, dm), wg, wu, wd, fin.reshape(1, dm))


def _ret_layer(x, batch, seq, norm_w, w_qkvg, gn_w, w_o, ffn_g, ffn_wg, ffn_wu, ffn_wd, fin,
               row, row_out, blk, hc):
    dm = x.shape[1]
    heads = dm // RET_QK_DIM
    rot = _rotary_tables(seq, row)
    dmask, qdec, kdec, cdec = _ret_decays(heads, blk)
    y = _ret_call(x, norm_w, w_qkvg.astype(BF16), rot, gn_w, dmask, qdec, kdec, cdec,
                  batch, seq, heads, row, blk)
    return _ret_ffn_call(x, y, w_o.astype(BF16), ffn_g, ffn_wg.astype(BF16), ffn_wu.astype(BF16),
                         ffn_wd.astype(BF16), fin, row_out, hc)


def kernel(x, s5_norm, s5_lambda_re, s5_lambda_im, s5_log_step, s5_b_re, s5_b_im, s5_c_re, s5_c_im,
           s5_d, s5_glu_w, s5_glu_b, ret_norm, ret_w_qkvg, ret_gn_w, ret_w_o, ffn_norm, ffn_w_gate,
           ffn_w_up, ffn_w_down, final_norm):
    batch, seq, dm = x.shape
    row, row_out, blk, hc = _tiles(seq)
    assert ffn_norm.shape[0] == 2 and s5_norm.shape[0] == 1 and ret_norm.shape[0] == 1
    h = x.reshape(batch * seq, dm)
    h = _s5_layer(h, batch, seq, s5_norm[0], s5_lambda_re[0], s5_lambda_im[0], s5_log_step[0],
                  s5_b_re[0], s5_b_im[0], s5_c_re[0], s5_c_im[0], s5_d[0], s5_glu_w[0],
                  s5_glu_b[0], ffn_norm[0], ffn_w_gate[0], ffn_w_up[0], ffn_w_down[0], row, hc)
    h = _ret_layer(h, batch, seq, ret_norm[0], ret_w_qkvg[0], ret_gn_w[0], ret_w_o[0],
                   ffn_norm[1], ffn_w_gate[1], ffn_w_up[1], ffn_w_down[1], final_norm,
                   row, row_out, blk, hc)
    return h.reshape(batch, seq, dm)
```

```python
import functools

import jax
import jax.numpy as jnp
from jax import lax
from jax.experimental import pallas as pl
from jax.experimental.pallas import tpu as pltpu

F32 = jnp.float32
BF16 = jnp.bfloat16

NORM_EPS = 1e-6
CHUNK = 64
S5_GROUP = 16
S5_STATE = 64
S5_SUB = 16
S5_GB = 16
RET_QK_DIM = 256
RET_V_DIM = 512
ROPE_BASE = 10000.0

V7X_VMEM_BYTES = 64 * 1024 * 1024
V7X_LANES = 128
VMEM_LIMIT = 56 * 1024 * 1024


def _tiles(seq):
    row = min(512, seq)
    row_out = min(1024, seq)
    ret = min(256, seq)
    hc = 256
    assert seq % row == 0 and seq % row_out == 0 and row % ret == 0 and ret % CHUNK == 0
    return row, row_out, ret, hc


def _cparams(sem):
    return pltpu.CompilerParams(dimension_semantics=sem, vmem_limit_bytes=VMEM_LIMIT)


def _resident(shape):
    nd = len(shape)
    return pl.BlockSpec(shape, lambda *_: (0,) * nd, pipeline_mode=pl.Buffered(1))


def _rms(xf, g):
    ms = jnp.mean(jnp.square(xf), axis=-1, keepdims=True)
    return xf * lax.rsqrt(ms + NORM_EPS) * g


def _dot(a, b):
    return jnp.dot(a, b, preferred_element_type=F32)


def _bdot(a, b):
    return lax.dot_general(a, b, (((2,), (1,)), ((0,), (0,))), preferred_element_type=F32)


def _s5_pre_kernel(x_hbm, g_ref, nn_ref, are_ref, aim_ref, zt_ref, er_ref, ei_ref, xs, sem):
    groups = zt_ref.shape[1]
    nseg = zt_ref.shape[3]
    i = pl.program_id(0)
    steps = pl.num_programs(0)

    def gathers(step, slot):
        return [pltpu.make_async_copy(x_hbm.at[:, step * S5_SUB + t, :], xs.at[slot, t],
                                      sem.at[slot]) for t in range(S5_SUB)]

    @pl.when(i == 0)
    def _():
        for c in gathers(0, 0):
            c.start()

    @pl.when(i + 1 < steps)
    def _():
        for c in gathers(i + 1, (i + 1) % 2):
            c.start()

    slot = i % 2
    for c in gathers(i, slot):
        c.wait()
    g = g_ref[...]
    for t in range(S5_SUB):
        ut = _rms(xs[slot, t], g)
        zt_ref[0, :, t * S5_GROUP:(t + 1) * S5_GROUP, :] = (
            ut.T.reshape(groups, S5_GROUP, nseg).astype(zt_ref.dtype))
    s = _bdot(nn_ref[...], zt_ref[0])
    sr = s[:, :S5_STATE, :].reshape(groups * S5_STATE, nseg)
    si = s[:, S5_STATE:, :].reshape(groups * S5_STATE, nseg)

    @pl.when(i == 0)
    def _():
        er_ref[...] = sr
        ei_ref[...] = si

    @pl.when(i > 0)
    def _():
        er, ei = er_ref[...], ei_ref[...]
        ar, ai = are_ref[...], aim_ref[...]
        er_ref[...] = ar * er - ai * ei + sr
        ei_ref[...] = ar * ei + ai * er + si


def _s5_pre_call(x3, g, nn, a_re, a_im):
    nseg, sl, dm = x3.shape
    groups = dm // S5_GROUP
    steps = sl // S5_SUB
    width = S5_SUB * S5_GROUP
    rows = groups * S5_STATE
    return pl.pallas_call(
        _s5_pre_kernel,
        grid=(steps,),
        in_specs=[pl.BlockSpec(memory_space=pl.ANY), _resident((1, dm)),
                  _resident(nn.shape), _resident((rows, nseg)), _resident((rows, nseg))],
        out_specs=[pl.BlockSpec((1, groups, width, nseg), lambda i: (i, 0, 0, 0)),
                   pl.BlockSpec((rows, nseg), lambda i: (0, 0)),
                   pl.BlockSpec((rows, nseg), lambda i: (0, 0))],
        out_shape=[jax.ShapeDtypeStruct((steps, groups, width, nseg), BF16),
                   jax.ShapeDtypeStruct((rows, nseg), F32),
                   jax.ShapeDtypeStruct((rows, nseg), F32)],
        scratch_shapes=[pltpu.VMEM((2, S5_SUB, nseg, dm), F32), pltpu.SemaphoreType.DMA((2,))],
        compiler_params=_cparams(("arbitrary",)),
        name="s5_pre",
    )(x3, g.reshape(1, dm), nn, a_re, a_im)


def _s5_stitch(er_ref, ei_ref, pr_ref, pi_ref, xr_ref, xi_ref, nsb):
    er, ei = er_ref[...], ei_ref[...]
    lane = lax.broadcasted_iota(jnp.int32, er.shape, 1) % nsb
    d, k = 1, 0
    while d < nsb:
        sr, si = pltpu.roll(er, d, axis=1), pltpu.roll(ei, d, axis=1)
        pr, pi = pr_ref[:, k:k + 1], pi_ref[:, k:k + 1]
        ok = lane >= d
        er, ei = (er + jnp.where(ok, pr * sr - pi * si, 0.0),
                  ei + jnp.where(ok, pr * si + pi * sr, 0.0))
        d, k = 2 * d, k + 1
    ok = lane >= 1
    xr_ref[...] = jnp.where(ok, pltpu.roll(er, 1, axis=1), 0.0)
    xi_ref[...] = jnp.where(ok, pltpu.roll(ei, 1, axis=1), 0.0)


def _s5_core_kernel(zt_ref, tt_ref, nn_ref, mm_ref, are_ref, aim_ref, er_ref, ei_ref, pr_ref,
                    pi_ref, y_hbm, xr_scr, xi_scr, ybuf, sem, *, nsb):
    pair, gb, _, nseg = zt_ref.shape
    cols = gb * S5_GROUP
    j, k = pl.program_id(0), pl.program_id(1)
    n = j * pl.num_programs(1) + k
    last = pl.num_programs(0) * pl.num_programs(1) - 1
    slot = n % 2

    def scatters(sl):
        return [pltpu.make_async_copy(
            ybuf.at[sl, s, t],
            y_hbm.at[:, (k * pair + s) * S5_SUB + t, pl.ds(j * cols, cols)],
            sem.at[sl]) for s in range(pair) for t in range(S5_SUB)]

    @pl.when(k == 0)
    def _():
        _s5_stitch(er_ref, ei_ref, pr_ref, pi_ref, xr_scr, xi_scr, nsb)

    @pl.when(n >= 2)
    def _():
        for c in scatters(slot):
            c.wait()

    zt = jnp.concatenate([zt_ref[s] for s in range(pair)], axis=-1)
    s_all = _bdot(nn_ref[...], zt)
    ar, ai = are_ref[...], aim_ref[...]
    xr, xi = xr_scr[...], xi_scr[...]
    starts = []
    for s in range(pair):
        starts.append(jnp.concatenate([xr.reshape(gb, S5_STATE, nseg),
                                       xi.reshape(gb, S5_STATE, nseg)], axis=1))
        inc = s_all[:, :, s * nseg:(s + 1) * nseg]
        sr = inc[:, :S5_STATE, :].reshape(gb * S5_STATE, nseg)
        si = inc[:, S5_STATE:, :].reshape(gb * S5_STATE, nseg)
        xr, xi = ar * xr - ai * xi + sr, ar * xi + ai * xr + si
    xr_scr[...] = xr
    xi_scr[...] = xi
    xprev = jnp.concatenate(starts, axis=-1).astype(zt.dtype)
    y = _bdot(tt_ref[...], zt) + _bdot(mm_ref[...], xprev)
    for t in range(S5_SUB):
        blk = y[:, t * S5_GROUP:(t + 1) * S5_GROUP, :].reshape(cols, pair * nseg)
        bt = blk.T
        for s in range(pair):
            ybuf[slot, s, t] = bt[s * nseg:(s + 1) * nseg]
    for c in scatters(slot):
        c.start()

    @pl.when(n == last)
    def _():
        for c in scatters(slot):
            c.wait()

    @pl.when(jnp.logical_and(n == last, n >= 1))
    def _():
        for c in scatters(1 - slot):
            c.wait()


def _s5_core_call(zt, tt, nn, mm, a_re, a_im, er, ei, p_re, p_im, nsb, dm):
    steps, groups, width, nseg = zt.shape
    gb = min(S5_GB, groups)
    pair = 2 if steps % 2 == 0 else 1
    rows = gb * S5_STATE
    per_gb = lambda shape: pl.BlockSpec(shape, lambda j, k: (j,) + (0,) * (len(shape) - 1))
    return pl.pallas_call(
        functools.partial(_s5_core_kernel, nsb=nsb),
        grid=(groups // gb, steps // pair),
        in_specs=[pl.BlockSpec((pair, gb, width, nseg), lambda j, k: (k, j, 0, 0)),
                  per_gb((gb, width, width)), per_gb((gb, 2 * S5_STATE, width)),
                  per_gb((gb, width, 2 * S5_STATE)),
                  per_gb((rows, nseg)), per_gb((rows, nseg)), per_gb((rows, nseg)),
                  per_gb((rows, nseg)), per_gb((rows, p_re.shape[1])),
                  per_gb((rows, p_im.shape[1]))],
        out_specs=pl.BlockSpec(memory_space=pl.ANY),
        out_shape=jax.ShapeDtypeStruct((nseg, steps * S5_SUB, dm), F32),
        scratch_shapes=[pltpu.VMEM((rows, nseg), F32), pltpu.VMEM((rows, nseg), F32),
                        pltpu.VMEM((2, pair, S5_SUB, nseg, gb * S5_GROUP), F32),
                        pltpu.SemaphoreType.DMA((2,))],
        compiler_params=_cparams(("arbitrary", "arbitrary")),
        name="s5_core",
    )(zt, tt, nn, mm, a_re, a_im, er, ei, p_re, p_im)


def _ffn_rows(x, g_ref, wg_ref, wu_ref, wd_ref, acc_ref, hc):
    u = _rms(x, g_ref[...]).astype(BF16)
    hidden = wg_ref.shape[1]
    for j in range(hidden // hc):
        cols = slice(j * hc, (j + 1) * hc)
        gt = _dot(u, wg_ref[:, cols])
        up = _dot(u, wu_ref[:, cols])
        h = (jax.nn.silu(gt) * up).astype(BF16)
        part = _dot(h, wd_ref[cols, :])
        if j == 0:
            acc_ref[...] = part
        else:
            acc_ref[...] += part
    return x + acc_ref[...]


def _layer_resident(shape, layer):
    nd = len(shape)
    return pl.BlockSpec((None,) + shape, lambda *_: (layer,) + (0,) * nd,
                        pipeline_mode=pl.Buffered(1))


def _ffn_specs(row, dm, hidden, layer):
    tile = pl.BlockSpec((row, dm), lambda i: (i, 0))
    weights = [_resident((1, dm)), _layer_resident((dm, hidden), layer),
               _layer_resident((dm, hidden), layer), _layer_resident((hidden, dm), layer)]
    return tile, weights


def _s5_ffn_kernel(x_ref, y_ref, g_ref, d_ref, w_ref, b_ref, fg_ref, wg_ref, wu_ref, wd_ref,
                   o_ref, acc_ref, *, hc):
    x = x_ref[...]
    u = _rms(x, g_ref[...])
    y = jax.nn.gelu(y_ref[...] + d_ref[...] * u)
    gate = jax.nn.sigmoid(_dot(y.astype(BF16), w_ref[...]) + b_ref[...])
    o_ref[...] = _ffn_rows(x + y * gate, fg_ref, wg_ref, wu_ref, wd_ref, acc_ref, hc)


def _s5_ffn_call(x, y, g, d, w, b, fg, wg, wu, wd, layer, row, hc):
    t, dm = x.shape
    hidden = wg.shape[-1]
    tile, weights = _ffn_specs(row, dm, hidden, layer)
    return pl.pallas_call(
        functools.partial(_s5_ffn_kernel, hc=hc),
        grid=(t // row,),
        in_specs=[tile, tile, _resident((1, dm)), _resident((1, dm)), _resident((dm, dm)),
                  _resident((1, dm))] + weights,
        out_specs=tile,
        out_shape=jax.ShapeDtypeStruct((t, dm), F32),
        scratch_shapes=[pltpu.VMEM((row, dm), F32)],
        compiler_params=_cparams(("parallel",)),
        name="s5_ffn",
    )(x, y, g.reshape(1, dm), d.reshape(1, dm), w, b.reshape(1, dm), fg.reshape(1, dm), wg, wu, wd)


def _s5_tab_kernel(p_ref, q_ref, c1_ref, c2_ref, b1_ref, b2_ref, bs_ref, tt_ref, nn_ref, mm_ref):
    sub = S5_SUB
    width = sub * S5_GROUP
    lane = lax.broadcasted_iota(jnp.int32, (S5_GROUP, width), 1)
    sgn = jnp.where(lax.broadcasted_iota(jnp.int32, (1, 2 * S5_STATE), 1) < S5_STATE, 1.0, -1.0)
    for gi in range(p_ref.shape[0]):
        p, q = p_ref[gi], q_ref[gi]
        c1, c2, b1, b2 = c1_ref[gi], c2_ref[gi], b1_ref[gi], b2_ref[gi]
        cl = [c1 * p[k:k + 1] + c2 * q[k:k + 1] for k in range(sub + 1)]
        mm_ref[gi] = (jnp.concatenate(cl[1:], axis=0) * sgn).astype(mm_ref.dtype)
        r = jnp.dot(bs_ref[gi], jnp.concatenate(cl[:sub], axis=0).T,
                    precision=lax.Precision.HIGHEST, preferred_element_type=F32)
        rows = [r] + [jnp.where(lane >= S5_GROUP * s, pltpu.roll(r, S5_GROUP * s, axis=1), 0.0)
                      for s in range(1, sub)]
        tt_ref[gi] = jnp.concatenate(rows, axis=0).T.astype(tt_ref.dtype)
        nt = [b1 * p[sub - 1 - s:sub - s] + b2 * q[sub - 1 - s:sub - s] for s in range(sub)]
        nn_ref[gi] = jnp.concatenate(nt, axis=0).T.astype(nn_ref.dtype)


def _s5_tab_call(p, q, c1, c2, b1, b2, bs):
    groups = p.shape[0]
    gt = 4 if groups % 4 == 0 else 1
    width = S5_SUB * S5_GROUP
    st2 = 2 * S5_STATE
    blk = lambda a: pl.BlockSpec((gt,) + a.shape[1:], lambda i: (i, 0, 0))
    out = lambda r, c: pl.BlockSpec((gt, r, c), lambda i: (i, 0, 0))
    return pl.pallas_call(
        _s5_tab_kernel,
        grid=(groups // gt,),
        in_specs=[blk(a) for a in (p, q, c1, c2, b1, b2, bs)],
        out_specs=[out(width, width), out(st2, width), out(width, st2)],
        out_shape=[jax.ShapeDtypeStruct((groups, width, width), BF16),
                   jax.ShapeDtypeStruct((groups, st2, width), BF16),
                   jax.ShapeDtypeStruct((groups, width, st2), BF16)],
        compiler_params=_cparams(("parallel",)),
        name="s5_tab",
    )(p, q, c1, c2, b1, b2, bs)


def _s5_tables(lam_re, lam_im, log_step, b_re, b_im, c_re, c_im, seg_len, nsb, nseg):
    sub = S5_SUB
    step = jnp.exp(log_step)[:, None]
    dre, dim = lam_re * step, lam_im * step

    def power(k):
        kk = jnp.asarray(k, F32)[None, :, None]
        mag = jnp.exp(kk * dre[:, None, :])
        return mag * jnp.cos(kk * dim[:, None, :]), mag * jnp.sin(kk * dim[:, None, :])

    pw_re, pw_im = power(jnp.arange(sub + 1))
    den = lam_re * lam_re + lam_im * lam_im
    nr, ni = pw_re[:, 1] - 1.0, pw_im[:, 1]
    cf_re = ((nr * lam_re + ni * lam_im) / den)[:, None, :]
    cf_im = ((ni * lam_re - nr * lam_im) / den)[:, None, :]
    bt_re, bt_im = b_re.transpose(0, 2, 1), b_im.transpose(0, 2, 1)
    bb_re = cf_re * bt_re - cf_im * bt_im
    bb_im = cf_re * bt_im + cf_im * bt_re
    cat = lambda a, b: jnp.concatenate([a, b], axis=-1)
    tt, nn, mm = _s5_tab_call(cat(pw_re, pw_im), cat(pw_im, pw_re), cat(c_re, c_re),
                              cat(-c_im, c_im), cat(bb_re, bb_re), cat(-bb_im, bb_im),
                              cat(bb_re, -bb_im))
    a_re = jnp.broadcast_to(pw_re[:, sub].reshape(-1, 1), (pw_re[:, sub].size, nseg))
    a_im = jnp.broadcast_to(pw_im[:, sub].reshape(-1, 1), (pw_im[:, sub].size, nseg))
    nd = max(1, (nsb - 1).bit_length())
    p_re, p_im = power(seg_len * (2 ** jnp.arange(nd)))
    p_re = p_re.transpose(0, 2, 1).reshape(-1, nd)
    p_im = p_im.transpose(0, 2, 1).reshape(-1, nd)
    return tt, nn, mm, a_re, a_im, p_re, p_im


def _s5_layer(x, batch, seq, norm_w, lam_re, lam_im, log_step, b_re, b_im, c_re, c_im, d, glu_w,
              glu_b, ffn_g, ffn_wg, ffn_wu, ffn_wd, ffn_layer, row, hc):
    t, dm = x.shape
    nseg = V7X_LANES
    seg_len = t // nseg
    assert t % nseg == 0 and seg_len % S5_SUB == 0 and seq % seg_len == 0
    nsb = seq // seg_len
    tt, nn, mm, a_re, a_im, p_re, p_im = _s5_tables(
        lam_re, lam_im, log_step, b_re, b_im, c_re, c_im, seg_len, nsb, nseg)
    zt, er, ei = _s5_pre_call(x.reshape(nseg, seg_len, dm), norm_w, nn, a_re, a_im)
    y = _s5_core_call(zt, tt, nn, mm, a_re, a_im, er, ei, p_re, p_im, nsb, dm).reshape(t, dm)
    return _s5_ffn_call(x, y, norm_w, d.reshape(-1), glu_w.astype(BF16), glu_b, ffn_g,
                        ffn_wg, ffn_wu, ffn_wd, ffn_layer, row, hc)


def _ret_kernel(x_ref, g_ref, w_ref, cr_ref, sr_ref, cb_ref, sb_ref, gn_ref, dm_ref, qd_ref, kd_ref, cd_ref,
                y_ref, state_ref, *, heads, blk):
    @pl.when(pl.program_id(1) == 0)
    def _():
        state_ref[...] = jnp.zeros_like(state_ref)

    u = _rms(x_ref[...], g_ref[...]).astype(BF16)
    cb = cb_ref[pl.ds(pl.program_id(1), 1), :]
    sb = sb_ref[pl.ds(pl.program_id(1), 1), :]
    cos = cr_ref[...] * cb - sr_ref[...] * sb
    sin = sr_ref[...] * cb + cr_ref[...] * sb
    half = RET_QK_DIM // 2
    qk_w = heads * RET_QK_DIM
    v_w = heads * RET_V_DIM
    k_scale = RET_QK_DIM ** -0.5
    row = x_ref.shape[0]

    def rotary(tq):
        t1, t2 = tq[:, :half], tq[:, half:]
        return jnp.concatenate([t1 * cos - t2 * sin, t1 * sin + t2 * cos], axis=-1)

    for h in range(heads):
        cq = h * RET_QK_DIM
        cv = 2 * qk_w + h * RET_V_DIM
        q = rotary(_dot(u, w_ref[:, cq:cq + RET_QK_DIM])).astype(BF16)
        k = rotary(_dot(u, w_ref[:, qk_w + cq:qk_w + cq + RET_QK_DIM]) * k_scale)
        v = _dot(u, w_ref[:, cv:cv + RET_V_DIM]).astype(BF16)
        gt = _dot(u, w_ref[:, v_w + cv:v_w + cv + RET_V_DIM])
        gn = gn_ref[:, h * RET_V_DIM:(h + 1) * RET_V_DIM]
        for r0 in range(0, row, blk):
            qs, ks, vs = q[r0:r0 + blk], k[r0:r0 + blk], v[r0:r0 + blk]
            s = lax.dot_general(qs, ks.astype(BF16), (((1,), (1,)), ((), ())),
                                preferred_element_type=F32)
            o = _dot((s * dm_ref[h]).astype(BF16), vs)
            st = state_ref[h]
            o = o + _dot(qs, st.astype(BF16)) * qd_ref[h]
            kd = (ks * kd_ref[h]).astype(BF16)
            state_ref[h] = st * cd_ref[h] + lax.dot_general(
                kd, vs, (((0,), (0,)), ((), ())), preferred_element_type=F32)
            mean = jnp.mean(o, axis=-1, keepdims=True)
            cen = o - mean
            var = jnp.mean(jnp.square(cen), axis=-1, keepdims=True)
            on = cen * lax.rsqrt(var + NORM_EPS) * gn
            y_ref[r0:r0 + blk, h * RET_V_DIM:(h + 1) * RET_V_DIM] = (
                jax.nn.silu(gt[r0:r0 + blk]) * on).astype(y_ref.dtype)


def _ret_call(x, g, w, rot, gn_w, dmask, qdec, kdec, cdec, batch, seq, heads, row, blk):
    t, dm = x.shape
    v_w = heads * RET_V_DIM
    nb = seq // row
    kern = functools.partial(_ret_kernel, heads=heads, blk=blk)
    rows = lambda width: pl.BlockSpec((row, width), lambda b, i: (b * nb + i, 0))
    tabs = [_resident(a.shape) for a in rot]
    return pl.pallas_call(
        kern,
        grid=(batch, nb),
        in_specs=[rows(dm), _resident((1, dm)), _resident(w.shape)] + tabs + [
                  _resident((1, v_w)), _resident(dmask.shape), _resident(qdec.shape),
                  _resident(kdec.shape), _resident(cdec.shape)],
        out_specs=rows(v_w),
        out_shape=jax.ShapeDtypeStruct((t, v_w), BF16),
        scratch_shapes=[pltpu.VMEM((heads, RET_QK_DIM, RET_V_DIM), F32)],
        compiler_params=_cparams(("parallel", "arbitrary")),
        name="ret_mix",
    )(x, g.reshape(1, dm), w, *rot, gn_w.reshape(1, v_w), dmask, qdec, kdec, cdec)


def _ret_decays(heads, blk):
    log_gamma = jnp.log1p(-jnp.exp2(-5.0 - jnp.arange(heads, dtype=F32)))
    pos = jnp.arange(blk, dtype=F32)
    diff = pos[:, None] - pos[None, :]
    cn = (jnp.arange(blk) // CHUNK)[:, None]
    cm = (jnp.arange(blk) // CHUNK)[None, :]
    expo = jnp.where(cn == cm, jnp.abs(diff), diff)
    dmask = jnp.where((cm <= cn)[None], jnp.exp(log_gamma[:, None, None] * expo[None]), 0.0)
    qdec = jnp.exp((pos[None, :] + 1.0) * log_gamma[:, None])[..., None]
    kdec = jnp.exp((blk - 1.0 - pos)[None, :] * log_gamma[:, None])[..., None]
    cdec = jnp.exp(blk * log_gamma)[:, None, None]
    return dmask, qdec, kdec, cdec


def _rotary_tables(seq, row):
    inv_freq = 1.0 / (ROPE_BASE ** jnp.linspace(0.0, 1.0, RET_QK_DIM // 2, dtype=F32))
    ang_r = jnp.arange(row, dtype=F32)[:, None] * inv_freq[None, :]
    ang_b = (jnp.arange(seq // row, dtype=F32) * row)[:, None] * inv_freq[None, :]
    return jnp.cos(ang_r), jnp.sin(ang_r), jnp.cos(ang_b), jnp.sin(ang_b)


def _ret_ffn_kernel(x_ref, y_ref, w_ref, fg_ref, wg_ref, wu_ref, wd_ref, fin_ref, o_ref, acc_ref,
                    *, hc):
    x1 = x_ref[...] + _dot(y_ref[...], w_ref[...])
    out = _ffn_rows(x1, fg_ref, wg_ref, wu_ref, wd_ref, acc_ref, hc)
    o_ref[...] = _rms(out, fin_ref[...])


def _ret_ffn_call(x, y, w, fg, wg, wu, wd, layer, fin, row, hc):
    t, dm = x.shape
    hidden = wg.shape[-1]
    kdim = y.shape[1]
    tile, weights = _ffn_specs(row, dm, hidden, layer)
    return pl.pallas_call(
        functools.partial(_ret_ffn_kernel, hc=hc),
        grid=(t // row,),
        in_specs=[tile, pl.BlockSpec((row, kdim), lambda i: (i, 0)), _resident((kdim, dm))]
        + weights + [_resident((1, dm))],
        out_specs=tile,
        out_shape=jax.ShapeDtypeStruct((t, dm), F32),
        scratch_shapes=[pltpu.VMEM((row, dm), F32)],
        compiler_params=_cparams(("parallel",)),
        name="ret_ffn",
    )(x, y, w, fg.reshape(1, dm), wg, wu, wd, fin.reshape(1, dm))


def _ret_layer(x, batch, seq, norm_w, w_qkvg, gn_w, w_o, ffn_g, ffn_wg, ffn_wu, ffn_wd, ffn_layer,
               fin, row, row_out, blk, hc):
    dm = x.shape[1]
    heads = dm // RET_QK_DIM
    rot = _rotary_tables(seq, row)
    dmask, qdec, kdec, cdec = _ret_decays(heads, blk)
    y = _ret_call(x, norm_w, w_qkvg.astype(BF16), rot, gn_w, dmask, qdec, kdec, cdec,
                  batch, seq, heads, row, blk)
    return _ret_ffn_call(x, y, w_o.astype(BF16), ffn_g, ffn_wg, ffn_wu, ffn_wd, ffn_layer, fin,
                         row_out, hc)


def kernel(x, s5_norm, s5_lambda_re, s5_lambda_im, s5_log_step, s5_b_re, s5_b_im, s5_c_re, s5_c_im,
           s5_d, s5_glu_w, s5_glu_b, ret_norm, ret_w_qkvg, ret_gn_w, ret_w_o, ffn_norm, ffn_w_gate,
           ffn_w_up, ffn_w_down, final_norm):
    batch, seq, dm = x.shape
    row, row_out, blk, hc = _tiles(seq)
    assert ffn_norm.shape[0] == 2 and s5_norm.shape[0] == 1 and ret_norm.shape[0] == 1
    h = x.reshape(batch * seq, dm)
    wg, wu, wd = ffn_w_gate.astype(BF16), ffn_w_up.astype(BF16), ffn_w_down.astype(BF16)
    h = _s5_layer(h, batch, seq, s5_norm[0], s5_lambda_re[0], s5_lambda_im[0], s5_log_step[0],
                  s5_b_re[0], s5_b_im[0], s5_c_re[0], s5_c_im[0], s5_d[0], s5_glu_w[0],
                  s5_glu_b[0], ffn_norm[0], wg, wu, wd, 0, row, hc)
    h = _ret_layer(h, batch, seq, ret_norm[0], ret_w_qkvg[0], ret_gn_w[0], ret_w_o[0],
                   ffn_norm[1], wg, wu, wd, 1, final_norm,
                   row, row_out, blk, hc)
    return h.reshape(batch, seq, dm)
```

```python
import functools

import jax
import jax.numpy as jnp
from jax import lax
from jax.experimental import pallas as pl
from jax.experimental.pallas import tpu as pltpu

F32 = jnp.float32
BF16 = jnp.bfloat16

NORM_EPS = 1e-6
CHUNK = 64
S5_GROUP = 16
S5_STATE = 64
S5_SUB = 16
S5_GB = 16
RET_QK_DIM = 256
RET_V_DIM = 512
ROPE_BASE = 10000.0

V7X_VMEM_BYTES = 64 * 1024 * 1024
V7X_LANES = 128
VMEM_LIMIT = 56 * 1024 * 1024


def _tiles(seq):
    row = min(512, seq)
    row_out = min(1024, seq)
    ret = min(256, seq)
    hc = 256
    assert seq % row == 0 and seq % row_out == 0 and row % ret == 0 and ret % CHUNK == 0
    return row, row_out, ret, hc


def _cparams(sem):
    return pltpu.CompilerParams(dimension_semantics=sem, vmem_limit_bytes=VMEM_LIMIT)


def _resident(shape):
    nd = len(shape)
    return pl.BlockSpec(shape, lambda *_: (0,) * nd, pipeline_mode=pl.Buffered(1))


def _rms(xf, g):
    ms = jnp.mean(jnp.square(xf), axis=-1, keepdims=True)
    return xf * lax.rsqrt(ms + NORM_EPS) * g


def _dot(a, b):
    return jnp.dot(a, b, preferred_element_type=F32)


def _bdot(a, b):
    return lax.dot_general(a, b, (((2,), (1,)), ((0,), (0,))), preferred_element_type=F32)


def _s5_pre_kernel(x_hbm, g_ref, nn_ref, are_ref, aim_ref, zt_ref, er_ref, ei_ref, xs, sem):
    groups = zt_ref.shape[1]
    nseg = zt_ref.shape[3]
    i = pl.program_id(0)
    steps = pl.num_programs(0)

    def gathers(step, slot):
        return [pltpu.make_async_copy(x_hbm.at[:, step * S5_SUB + t, :], xs.at[slot, t],
                                      sem.at[slot]) for t in range(S5_SUB)]

    @pl.when(i == 0)
    def _():
        for c in gathers(0, 0):
            c.start()

    @pl.when(i + 1 < steps)
    def _():
        for c in gathers(i + 1, (i + 1) % 2):
            c.start()

    slot = i % 2
    for c in gathers(i, slot):
        c.wait()
    g = g_ref[...]
    for t in range(S5_SUB):
        ut = _rms(xs[slot, t], g)
        zt_ref[0, :, t * S5_GROUP:(t + 1) * S5_GROUP, :] = (
            ut.T.reshape(groups, S5_GROUP, nseg).astype(zt_ref.dtype))
    s = _bdot(nn_ref[...], zt_ref[0])
    sr = s[:, :S5_STATE, :].reshape(groups * S5_STATE, nseg)
    si = s[:, S5_STATE:, :].reshape(groups * S5_STATE, nseg)

    @pl.when(i == 0)
    def _():
        er_ref[...] = sr
        ei_ref[...] = si

    @pl.when(i > 0)
    def _():
        er, ei = er_ref[...], ei_ref[...]
        ar, ai = are_ref[...], aim_ref[...]
        er_ref[...] = ar * er - ai * ei + sr
        ei_ref[...] = ar * ei + ai * er + si


def _s5_pre_call(x3, g, nn, a_re, a_im):
    nseg, sl, dm = x3.shape
    groups = dm // S5_GROUP
    steps = sl // S5_SUB
    width = S5_SUB * S5_GROUP
    rows = groups * S5_STATE
    return pl.pallas_call(
        _s5_pre_kernel,
        grid=(steps,),
        in_specs=[pl.BlockSpec(memory_space=pl.ANY), _resident((1, dm)),
                  _resident(nn.shape), _resident((rows, nseg)), _resident((rows, nseg))],
        out_specs=[pl.BlockSpec((1, groups, width, nseg), lambda i: (i, 0, 0, 0)),
                   pl.BlockSpec((rows, nseg), lambda i: (0, 0)),
                   pl.BlockSpec((rows, nseg), lambda i: (0, 0))],
        out_shape=[jax.ShapeDtypeStruct((steps, groups, width, nseg), BF16),
                   jax.ShapeDtypeStruct((rows, nseg), F32),
                   jax.ShapeDtypeStruct((rows, nseg), F32)],
        scratch_shapes=[pltpu.VMEM((2, S5_SUB, nseg, dm), F32), pltpu.SemaphoreType.DMA((2,))],
        compiler_params=_cparams(("arbitrary",)),
        name="s5_pre",
    )(x3, g.reshape(1, dm), nn, a_re, a_im)


def _s5_stitch(er_ref, ei_ref, pr_ref, pi_ref, xr_ref, xi_ref, nsb):
    er, ei = er_ref[...], ei_ref[...]
    lane = lax.broadcasted_iota(jnp.int32, er.shape, 1) % nsb
    d, k = 1, 0
    while d < nsb:
        sr, si = pltpu.roll(er, d, axis=1), pltpu.roll(ei, d, axis=1)
        pr, pi = pr_ref[:, k:k + 1], pi_ref[:, k:k + 1]
        ok = lane >= d
        er, ei = (er + jnp.where(ok, pr * sr - pi * si, 0.0),
                  ei + jnp.where(ok, pr * si + pi * sr, 0.0))
        d, k = 2 * d, k + 1
    ok = lane >= 1
    xr_ref[...] = jnp.where(ok, pltpu.roll(er, 1, axis=1), 0.0)
    xi_ref[...] = jnp.where(ok, pltpu.roll(ei, 1, axis=1), 0.0)


def _s5_core_kernel(zt_ref, tt_ref, nn_ref, mm_ref, are_ref, aim_ref, er_ref, ei_ref, pr_ref,
                    pi_ref, y_hbm, xr_scr, xi_scr, ybuf, sem, *, nsb):
    pair, gb, _, nseg = zt_ref.shape
    cols = gb * S5_GROUP
    j, k = pl.program_id(0), pl.program_id(1)
    n = j * pl.num_programs(1) + k
    last = pl.num_programs(0) * pl.num_programs(1) - 1
    slot = n % 2

    def scatters(sl):
        return [pltpu.make_async_copy(
            ybuf.at[sl, s, t],
            y_hbm.at[:, (k * pair + s) * S5_SUB + t, pl.ds(j * cols, cols)],
            sem.at[sl]) for s in range(pair) for t in range(S5_SUB)]

    @pl.when(k == 0)
    def _():
        _s5_stitch(er_ref, ei_ref, pr_ref, pi_ref, xr_scr, xi_scr, nsb)

    @pl.when(n >= 2)
    def _():
        for c in scatters(slot):
            c.wait()

    zt = jnp.concatenate([zt_ref[s] for s in range(pair)], axis=-1)
    s_all = _bdot(nn_ref[...], zt)
    ar, ai = are_ref[...], aim_ref[...]
    xr, xi = xr_scr[...], xi_scr[...]
    starts = []
    for s in range(pair):
        starts.append(jnp.concatenate([xr.reshape(gb, S5_STATE, nseg),
                                       xi.reshape(gb, S5_STATE, nseg)], axis=1))
        inc = s_all[:, :, s * nseg:(s + 1) * nseg]
        sr = inc[:, :S5_STATE, :].reshape(gb * S5_STATE, nseg)
        si = inc[:, S5_STATE:, :].reshape(gb * S5_STATE, nseg)
        xr, xi = ar * xr - ai * xi + sr, ar * xi + ai * xr + si
    xr_scr[...] = xr
    xi_scr[...] = xi
    xprev = jnp.concatenate(starts, axis=-1).astype(zt.dtype)
    y = _bdot(tt_ref[...], zt) + _bdot(mm_ref[...], xprev)
    for t in range(S5_SUB):
        blk = y[:, t * S5_GROUP:(t + 1) * S5_GROUP, :].reshape(cols, pair * nseg)
        bt = blk.T
        for s in range(pair):
            ybuf[slot, s, t] = bt[s * nseg:(s + 1) * nseg]
    for c in scatters(slot):
        c.start()

    @pl.when(n == last)
    def _():
        for c in scatters(slot):
            c.wait()

    @pl.when(jnp.logical_and(n == last, n >= 1))
    def _():
        for c in scatters(1 - slot):
            c.wait()


def _s5_core_call(zt, tt, nn, mm, a_re, a_im, er, ei, p_re, p_im, nsb, dm):
    steps, groups, width, nseg = zt.shape
    gb = min(S5_GB, groups)
    pair = 2 if steps % 2 == 0 else 1
    rows = gb * S5_STATE
    per_gb = lambda shape: pl.BlockSpec(shape, lambda j, k: (j,) + (0,) * (len(shape) - 1))
    return pl.pallas_call(
        functools.partial(_s5_core_kernel, nsb=nsb),
        grid=(groups // gb, steps // pair),
        in_specs=[pl.BlockSpec((pair, gb, width, nseg), lambda j, k: (k, j, 0, 0)),
                  per_gb((gb, width, width)), per_gb((gb, 2 * S5_STATE, width)),
                  per_gb((gb, width, 2 * S5_STATE)),
                  per_gb((rows, nseg)), per_gb((rows, nseg)), per_gb((rows, nseg)),
                  per_gb((rows, nseg)), per_gb((rows, p_re.shape[1])),
                  per_gb((rows, p_im.shape[1]))],
        out_specs=pl.BlockSpec(memory_space=pl.ANY),
        out_shape=jax.ShapeDtypeStruct((nseg, steps * S5_SUB, dm), F32),
        scratch_shapes=[pltpu.VMEM((rows, nseg), F32), pltpu.VMEM((rows, nseg), F32),
                        pltpu.VMEM((2, pair, S5_SUB, nseg, gb * S5_GROUP), F32),
                        pltpu.SemaphoreType.DMA((2,))],
        compiler_params=_cparams(("arbitrary", "arbitrary")),
        name="s5_core",
    )(zt, tt, nn, mm, a_re, a_im, er, ei, p_re, p_im)


def _ffn_rows(x, g_ref, wg_ref, wu_ref, wd_ref, acc_ref, hc):
    u = _rms(x, g_ref[...]).astype(BF16)
    hidden = wg_ref.shape[1]
    for j in range(hidden // hc):
        cols = slice(j * hc, (j + 1) * hc)
        gt = _dot(u, wg_ref[:, cols])
        up = _dot(u, wu_ref[:, cols])
        h = (jax.nn.silu(gt) * up).astype(BF16)
        part = _dot(h, wd_ref[cols, :])
        if j == 0:
            acc_ref[...] = part
        else:
            acc_ref[...] += part
    return x + acc_ref[...]


def _layer_resident(shape, layer):
    nd = len(shape)
    return pl.BlockSpec((None,) + shape, lambda *_: (layer,) + (0,) * nd,
                        pipeline_mode=pl.Buffered(1))


def _ffn_specs(row, dm, hidden, layer):
    tile = pl.BlockSpec((row, dm), lambda i: (i, 0))
    weights = [_resident((1, dm)), _layer_resident((dm, hidden), layer),
               _layer_resident((dm, hidden), layer), _layer_resident((hidden, dm), layer)]
    return tile, weights


def _s5_ffn_kernel(x_ref, y_ref, g_ref, d_ref, w_ref, b_ref, fg_ref, wg_ref, wu_ref, wd_ref,
                   o_ref, acc_ref, *, hc):
    x = x_ref[...]
    u = _rms(x, g_ref[...])
    y = jax.nn.gelu(y_ref[...] + d_ref[...] * u)
    gate = jax.nn.sigmoid(_dot(y.astype(BF16), w_ref[...]) + b_ref[...])
    o_ref[...] = _ffn_rows(x + y * gate, fg_ref, wg_ref, wu_ref, wd_ref, acc_ref, hc)


def _s5_ffn_call(x, y, g, d, w, b, fg, wg, wu, wd, layer, row, hc):
    t, dm = x.shape
    hidden = wg.shape[-1]
    tile, weights = _ffn_specs(row, dm, hidden, layer)
    return pl.pallas_call(
        functools.partial(_s5_ffn_kernel, hc=hc),
        grid=(t // row,),
        in_specs=[tile, tile, _resident((1, dm)), _resident((1, dm)), _resident((dm, dm)),
                  _resident((1, dm))] + weights,
        out_specs=tile,
        out_shape=jax.ShapeDtypeStruct((t, dm), F32),
        scratch_shapes=[pltpu.VMEM((row, dm), F32)],
        compiler_params=_cparams(("parallel",)),
        name="s5_ffn",
    )(x, y, g.reshape(1, dm), d.reshape(1, dm), w, b.reshape(1, dm), fg.reshape(1, dm), wg, wu, wd)


def _s5_tab_kernel(p_ref, q_ref, c1_ref, c2_ref, b1_ref, b2_ref, bs_ref, tt_ref, nn_ref, mm_ref):
    sub = S5_SUB
    width = sub * S5_GROUP
    lane = lax.broadcasted_iota(jnp.int32, (S5_GROUP, width), 1)
    sgn = jnp.where(lax.broadcasted_iota(jnp.int32, (1, 2 * S5_STATE), 1) < S5_STATE, 1.0, -1.0)
    for gi in range(p_ref.shape[0]):
        p, q = p_ref[gi], q_ref[gi]
        c1, c2, b1, b2 = c1_ref[gi], c2_ref[gi], b1_ref[gi], b2_ref[gi]
        cl = [c1 * p[k:k + 1] + c2 * q[k:k + 1] for k in range(sub + 1)]
        mm_ref[gi] = (jnp.concatenate(cl[1:], axis=0) * sgn).astype(mm_ref.dtype)
        r = jnp.dot(bs_ref[gi], jnp.concatenate(cl[:sub], axis=0).T,
                    precision=lax.Precision.HIGHEST, preferred_element_type=F32)
        rows = [r] + [jnp.where(lane >= S5_GROUP * s, pltpu.roll(r, S5_GROUP * s, axis=1), 0.0)
                      for s in range(1, sub)]
        tt_ref[gi] = jnp.concatenate(rows, axis=0).T.astype(tt_ref.dtype)
        nt = [b1 * p[sub - 1 - s:sub - s] + b2 * q[sub - 1 - s:sub - s] for s in range(sub)]
        nn_ref[gi] = jnp.concatenate(nt, axis=0).T.astype(nn_ref.dtype)


def _s5_tab_call(p, q, c1, c2, b1, b2, bs):
    groups = p.shape[0]
    gt = 4 if groups % 4 == 0 else 1
    width = S5_SUB * S5_GROUP
    st2 = 2 * S5_STATE
    blk = lambda a: pl.BlockSpec((gt,) + a.shape[1:], lambda i: (i, 0, 0))
    out = lambda r, c: pl.BlockSpec((gt, r, c), lambda i: (i, 0, 0))
    return pl.pallas_call(
        _s5_tab_kernel,
        grid=(groups // gt,),
        in_specs=[blk(a) for a in (p, q, c1, c2, b1, b2, bs)],
        out_specs=[out(width, width), out(st2, width), out(width, st2)],
        out_shape=[jax.ShapeDtypeStruct((groups, width, width), BF16),
                   jax.ShapeDtypeStruct((groups, st2, width), BF16),
                   jax.ShapeDtypeStruct((groups, width, st2), BF16)],
        compiler_params=_cparams(("parallel",)),
        name="s5_tab",
    )(p, q, c1, c2, b1, b2, bs)


def _s5_tables(lam_re, lam_im, log_step, b_re, b_im, c_re, c_im, seg_len, nsb, nseg):
    sub = S5_SUB
    step = jnp.exp(log_step)[:, None]
    dre, dim = lam_re * step, lam_im * step

    def power(k):
        kk = jnp.asarray(k, F32)[None, :, None]
        mag = jnp.exp(kk * dre[:, None, :])
        return mag * jnp.cos(kk * dim[:, None, :]), mag * jnp.sin(kk * dim[:, None, :])

    pw_re, pw_im = power(jnp.arange(sub + 1))
    den = lam_re * lam_re + lam_im * lam_im
    nr, ni = pw_re[:, 1] - 1.0, pw_im[:, 1]
    cf_re = ((nr * lam_re + ni * lam_im) / den)[:, None, :]
    cf_im = ((ni * lam_re - nr * lam_im) / den)[:, None, :]
    bt_re, bt_im = b_re.transpose(0, 2, 1), b_im.transpose(0, 2, 1)
    bb_re = cf_re * bt_re - cf_im * bt_im
    bb_im = cf_re * bt_im + cf_im * bt_re
    cat = lambda a, b: jnp.concatenate([a, b], axis=-1)
    tt, nn, mm = _s5_tab_call(cat(pw_re, pw_im), cat(pw_im, pw_re), cat(c_re, c_re),
                              cat(-c_im, c_im), cat(bb_re, bb_re), cat(-bb_im, bb_im),
                              cat(bb_re, -bb_im))
    a_re = jnp.broadcast_to(pw_re[:, sub].reshape(-1, 1), (pw_re[:, sub].size, nseg))
    a_im = jnp.broadcast_to(pw_im[:, sub].reshape(-1, 1), (pw_im[:, sub].size, nseg))
    nd = max(1, (nsb - 1).bit_length())
    p_re, p_im = power(seg_len * (2 ** jnp.arange(nd)))
    p_re = p_re.transpose(0, 2, 1).reshape(-1, nd)
    p_im = p_im.transpose(0, 2, 1).reshape(-1, nd)
    return tt, nn, mm, a_re, a_im, p_re, p_im


def _s5_layer(x, batch, seq, norm_w, lam_re, lam_im, log_step, b_re, b_im, c_re, c_im, d, glu_w,
              glu_b, ffn_g, ffn_wg, ffn_wu, ffn_wd, ffn_layer, row, hc):
    t, dm = x.shape
    nseg = V7X_LANES
    seg_len = t // nseg
    assert t % nseg == 0 and seg_len % S5_SUB == 0 and seq % seg_len == 0
    nsb = seq // seg_len
    tt, nn, mm, a_re, a_im, p_re, p_im = _s5_tables(
        lam_re, lam_im, log_step, b_re, b_im, c_re, c_im, seg_len, nsb, nseg)
    zt, er, ei = _s5_pre_call(x.reshape(nseg, seg_len, dm), norm_w, nn, a_re, a_im)
    y = _s5_core_call(zt, tt, nn, mm, a_re, a_im, er, ei, p_re, p_im, nsb, dm).reshape(t, dm)
    return _s5_ffn_call(x, y, norm_w, d.reshape(-1), glu_w.astype(BF16), glu_b, ffn_g,
                        ffn_wg, ffn_wu, ffn_wd, ffn_layer, row, hc)


def _ret_kernel(x_ref, g_ref, w_ref, cr_ref, sr_ref, cb_ref, sb_ref, gn_ref, dm_ref, qd_ref, kd_ref, cd_ref,
                y_ref, state_ref, *, heads, blk):
    @pl.when(pl.program_id(1) == 0)
    def _():
        state_ref[...] = jnp.zeros_like(state_ref)

    u = _rms(x_ref[...], g_ref[...]).astype(BF16)
    cb = cb_ref[pl.ds(pl.program_id(1), 1), :]
    sb = sb_ref[pl.ds(pl.program_id(1), 1), :]
    cos = cr_ref[...] * cb - sr_ref[...] * sb
    sin = sr_ref[...] * cb + cr_ref[...] * sb
    half = RET_QK_DIM // 2
    qk_w = heads * RET_QK_DIM
    v_w = heads * RET_V_DIM
    k_scale = RET_QK_DIM ** -0.5
    row = x_ref.shape[0]

    def rotary(tq):
        t1, t2 = tq[:, :half], tq[:, half:]
        return jnp.concatenate([t1 * cos - t2 * sin, t1 * sin + t2 * cos], axis=-1)

    for h in range(heads):
        cq = h * RET_QK_DIM
        cv = 2 * qk_w + h * RET_V_DIM
        qf = rotary(_dot(u, w_ref[:, cq:cq + RET_QK_DIM]))
        q = qf.astype(BF16)
        k = rotary(_dot(u, w_ref[:, qk_w + cq:qk_w + cq + RET_QK_DIM]) * k_scale)
        v = _dot(u, w_ref[:, cv:cv + RET_V_DIM]).astype(BF16)
        gt = _dot(u, w_ref[:, v_w + cv:v_w + cv + RET_V_DIM])
        gn = gn_ref[:, h * RET_V_DIM:(h + 1) * RET_V_DIM]
        for r0 in range(0, row, blk):
            qs, ks, vs = q[r0:r0 + blk], k[r0:r0 + blk], v[r0:r0 + blk]
            s = lax.dot_general(qs, ks.astype(BF16), (((1,), (1,)), ((), ())),
                                preferred_element_type=F32)
            st = state_ref[h]
            qd = (qf[r0:r0 + blk] * qd_ref[h]).astype(BF16)
            o = _dot((s * dm_ref[h]).astype(BF16), vs) + _dot(qd, st.astype(BF16))
            kd = (ks * kd_ref[h]).astype(BF16)
            state_ref[h] = st * cd_ref[h] + lax.dot_general(
                kd, vs, (((0,), (0,)), ((), ())), preferred_element_type=F32)
            mean = jnp.mean(o, axis=-1, keepdims=True)
            cen = o - mean
            var = jnp.mean(jnp.square(cen), axis=-1, keepdims=True)
            on = cen * lax.rsqrt(var + NORM_EPS) * gn
            y_ref[r0:r0 + blk, h * RET_V_DIM:(h + 1) * RET_V_DIM] = (
                jax.nn.silu(gt[r0:r0 + blk]) * on).astype(y_ref.dtype)


def _ret_call(x, g, w, rot, gn_w, dmask, qdec, kdec, cdec, batch, seq, heads, row, blk):
    t, dm = x.shape
    v_w = heads * RET_V_DIM
    nb = seq // row
    kern = functools.partial(_ret_kernel, heads=heads, blk=blk)
    rows = lambda width: pl.BlockSpec((row, width), lambda b, i: (b * nb + i, 0))
    tabs = [_resident(a.shape) for a in rot]
    return pl.pallas_call(
        kern,
        grid=(batch, nb),
        in_specs=[rows(dm), _resident((1, dm)), _resident(w.shape)] + tabs + [
                  _resident((1, v_w)), _resident(dmask.shape), _resident(qdec.shape),
                  _resident(kdec.shape), _resident(cdec.shape)],
        out_specs=rows(v_w),
        out_shape=jax.ShapeDtypeStruct((t, v_w), BF16),
        scratch_shapes=[pltpu.VMEM((heads, RET_QK_DIM, RET_V_DIM), F32)],
        compiler_params=_cparams(("parallel", "arbitrary")),
        name="ret_mix",
    )(x, g.reshape(1, dm), w, *rot, gn_w.reshape(1, v_w), dmask, qdec, kdec, cdec)


def _ret_decays(heads, blk):
    log_gamma = jnp.log1p(-jnp.exp2(-5.0 - jnp.arange(heads, dtype=F32)))
    pos = jnp.arange(blk, dtype=F32)
    diff = pos[:, None] - pos[None, :]
    cn = (jnp.arange(blk) // CHUNK)[:, None]
    cm = (jnp.arange(blk) // CHUNK)[None, :]
    expo = jnp.where(cn == cm, jnp.abs(diff), diff)
    dmask = jnp.where((cm <= cn)[None], jnp.exp(log_gamma[:, None, None] * expo[None]), 0.0)
    qdec = jnp.exp((pos[None, :] + 1.0) * log_gamma[:, None])[..., None]
    kdec = jnp.exp((blk - 1.0 - pos)[None, :] * log_gamma[:, None])[..., None]
    cdec = jnp.exp(blk * log_gamma)[:, None, None]
    return dmask, qdec, kdec, cdec


def _rotary_tables(seq, row):
    inv_freq = 1.0 / (ROPE_BASE ** jnp.linspace(0.0, 1.0, RET_QK_DIM // 2, dtype=F32))
    ang_r = jnp.arange(row, dtype=F32)[:, None] * inv_freq[None, :]
    ang_b = (jnp.arange(seq // row, dtype=F32) * row)[:, None] * inv_freq[None, :]
    return jnp.cos(ang_r), jnp.sin(ang_r), jnp.cos(ang_b), jnp.sin(ang_b)


def _ret_ffn_kernel(x_ref, y_ref, w_ref, fg_ref, wg_ref, wu_ref, wd_ref, fin_ref, o_ref, acc_ref,
                    *, hc):
    x1 = x_ref[...] + _dot(y_ref[...], w_ref[...])
    out = _ffn_rows(x1, fg_ref, wg_ref, wu_ref, wd_ref, acc_ref, hc)
    o_ref[...] = _rms(out, fin_ref[...])


def _ret_ffn_call(x, y, w, fg, wg, wu, wd, layer, fin, row, hc):
    t, dm = x.shape
    hidden = wg.shape[-1]
    kdim = y.shape[1]
    tile, weights = _ffn_specs(row, dm, hidden, layer)
    return pl.pallas_call(
        functools.partial(_ret_ffn_kernel, hc=hc),
        grid=(t // row,),
        in_specs=[tile, pl.BlockSpec((row, kdim), lambda i: (i, 0)), _resident((kdim, dm))]
        + weights + [_resident((1, dm))],
        out_specs=tile,
        out_shape=jax.ShapeDtypeStruct((t, dm), F32),
        scratch_shapes=[pltpu.VMEM((row, dm), F32)],
        compiler_params=_cparams(("parallel",)),
        name="ret_ffn",
    )(x, y, w, fg.reshape(1, dm), wg, wu, wd, fin.reshape(1, dm))


def _ret_layer(x, batch, seq, norm_w, w_qkvg, gn_w, w_o, ffn_g, ffn_wg, ffn_wu, ffn_wd, ffn_layer,
               fin, row, row_out, blk, hc):
    dm = x.shape[1]
    heads = dm // RET_QK_DIM
    rot = _rotary_tables(seq, row)
    dmask, qdec, kdec, cdec = _ret_decays(heads, blk)
    y = _ret_call(x, norm_w, w_qkvg.astype(BF16), rot, gn_w, dmask, qdec, kdec, cdec,
                  batch, seq, heads, row, blk)
    return _ret_ffn_call(x, y, w_o.astype(BF16), ffn_g, ffn_wg, ffn_wu, ffn_wd, ffn_layer, fin,
                         row_out, hc)


def kernel(x, s5_norm, s5_lambda_re, s5_lambda_im, s5_log_step, s5_b_re, s5_b_im, s5_c_re, s5_c_im,
           s5_d, s5_glu_w, s5_glu_b, ret_norm, ret_w_qkvg, ret_gn_w, ret_w_o, ffn_norm, ffn_w_gate,
           ffn_w_up, ffn_w_down, final_norm):
    batch, seq, dm = x.shape
    row, row_out, blk, hc = _tiles(seq)
    assert ffn_norm.shape[0] == 2 and s5_norm.shape[0] == 1 and ret_norm.shape[0] == 1
    h = x.reshape(batch * seq, dm)
    wg, wu, wd = ffn_w_gate.astype(BF16), ffn_w_up.astype(BF16), ffn_w_down.astype(BF16)
    h = _s5_layer(h, batch, seq, s5_norm[0], s5_lambda_re[0], s5_lambda_im[0], s5_log_step[0],
                  s5_b_re[0], s5_b_im[0], s5_c_re[0], s5_c_im[0], s5_d[0], s5_glu_w[0],
                  s5_glu_b[0], ffn_norm[0], wg, wu, wd, 0, row, hc)
    h = _ret_layer(h, batch, seq, ret_norm[0], ret_w_qkvg[0], ret_gn_w[0], ret_w_o[0],
                   ffn_norm[1], wg, wu, wd, 1, final_norm,
                   row, row_out, blk, hc)
    return h.reshape(batch, seq, dm)
```

```python
import functools

import jax
import jax.numpy as jnp
from jax import lax
from jax.experimental import pallas as pl
from jax.experimental.pallas import tpu as pltpu

F32 = jnp.float32
BF16 = jnp.bfloat16

NORM_EPS = 1e-6
CHUNK = 64
S5_GROUP = 16
S5_STATE = 64
S5_SUB = 16
S5_GB = 16
S5_TAB_GROUPS = 4
RET_QK_DIM = 256
RET_V_DIM = 512
ROPE_BASE = 10000.0

V7X_VMEM_BYTES = 64 * 1024 * 1024
V7X_LANES = 128
VMEM_LIMIT = V7X_VMEM_BYTES * 7 // 8


def _tiles(seq):
    row = min(512, seq)
    row_out = min(1024, seq)
    ret = min(256, seq)
    hc = 256
    assert seq % row == 0 and seq % row_out == 0 and row % ret == 0 and ret % CHUNK == 0
    return row, row_out, ret, hc


def _cparams(sem):
    return pltpu.CompilerParams(dimension_semantics=sem, vmem_limit_bytes=VMEM_LIMIT)


def _resident(shape):
    nd = len(shape)
    return pl.BlockSpec(shape, lambda *_: (0,) * nd, pipeline_mode=pl.Buffered(1))


def _rms(xf, g):
    ms = jnp.mean(jnp.square(xf), axis=-1, keepdims=True)
    return xf * lax.rsqrt(ms + NORM_EPS) * g


def _dot(a, b):
    return jnp.dot(a, b, preferred_element_type=F32)


def _bdot(a, b):
    return lax.dot_general(a, b, (((2,), (1,)), ((0,), (0,))), preferred_element_type=F32)


def _s5_pre_kernel(x_hbm, g_ref, nn_ref, are_ref, aim_ref, zt_ref, er_ref, ei_ref, xs, sem):
    groups = zt_ref.shape[1]
    nseg = zt_ref.shape[3]
    i = pl.program_id(0)
    steps = pl.num_programs(0)

    def gathers(step, slot):
        return [pltpu.make_async_copy(x_hbm.at[:, step * S5_SUB + t, :], xs.at[slot, t],
                                      sem.at[slot]) for t in range(S5_SUB)]

    @pl.when(i == 0)
    def _():
        for c in gathers(0, 0):
            c.start()

    @pl.when(i + 1 < steps)
    def _():
        for c in gathers(i + 1, (i + 1) % 2):
            c.start()

    slot = i % 2
    for c in gathers(i, slot):
        c.wait()
    g = g_ref[...]
    for t in range(S5_SUB):
        ut = _rms(xs[slot, t], g)
        zt_ref[0, :, t * S5_GROUP:(t + 1) * S5_GROUP, :] = (
            ut.T.reshape(groups, S5_GROUP, nseg).astype(zt_ref.dtype))
    s = _bdot(nn_ref[...], zt_ref[0])
    sr = s[:, :S5_STATE, :].reshape(groups * S5_STATE, nseg)
    si = s[:, S5_STATE:, :].reshape(groups * S5_STATE, nseg)

    @pl.when(i == 0)
    def _():
        er_ref[...] = sr
        ei_ref[...] = si

    @pl.when(i > 0)
    def _():
        er, ei = er_ref[...], ei_ref[...]
        ar, ai = are_ref[...], aim_ref[...]
        er_ref[...] = ar * er - ai * ei + sr
        ei_ref[...] = ar * ei + ai * er + si


def _s5_pre_call(x3, g, nn, a_re, a_im):
    nseg, sl, dm = x3.shape
    groups = dm // S5_GROUP
    steps = sl // S5_SUB
    width = S5_SUB * S5_GROUP
    rows = groups * S5_STATE
    return pl.pallas_call(
        _s5_pre_kernel,
        grid=(steps,),
        in_specs=[pl.BlockSpec(memory_space=pl.ANY), _resident((1, dm)),
                  _resident(nn.shape), _resident((rows, nseg)), _resident((rows, nseg))],
        out_specs=[pl.BlockSpec((1, groups, width, nseg), lambda i: (i, 0, 0, 0)),
                   pl.BlockSpec((rows, nseg), lambda i: (0, 0)),
                   pl.BlockSpec((rows, nseg), lambda i: (0, 0))],
        out_shape=[jax.ShapeDtypeStruct((steps, groups, width, nseg), BF16),
                   jax.ShapeDtypeStruct((rows, nseg), F32),
                   jax.ShapeDtypeStruct((rows, nseg), F32)],
        scratch_shapes=[pltpu.VMEM((2, S5_SUB, nseg, dm), F32), pltpu.SemaphoreType.DMA((2,))],
        compiler_params=_cparams(("arbitrary",)),
        name="s5_pre",
    )(x3, g.reshape(1, dm), nn, a_re, a_im)


def _s5_stitch(er_ref, ei_ref, pr_ref, pi_ref, xr_ref, xi_ref, nsb):
    er, ei = er_ref[...], ei_ref[...]
    lane = lax.broadcasted_iota(jnp.int32, er.shape, 1) % nsb
    d, k = 1, 0
    while d < nsb:
        sr, si = pltpu.roll(er, d, axis=1), pltpu.roll(ei, d, axis=1)
        pr, pi = pr_ref[:, k:k + 1], pi_ref[:, k:k + 1]
        ok = lane >= d
        er, ei = (er + jnp.where(ok, pr * sr - pi * si, 0.0),
                  ei + jnp.where(ok, pr * si + pi * sr, 0.0))
        d, k = 2 * d, k + 1
    ok = lane >= 1
    xr_ref[...] = jnp.where(ok, pltpu.roll(er, 1, axis=1), 0.0)
    xi_ref[...] = jnp.where(ok, pltpu.roll(ei, 1, axis=1), 0.0)


def _s5_core_kernel(zt_ref, tt_ref, nn_ref, mm_ref, are_ref, aim_ref, er_ref, ei_ref, pr_ref,
                    pi_ref, y_hbm, xr_scr, xi_scr, ybuf, sem, *, nsb):
    pair, gb, _, nseg = zt_ref.shape
    cols = gb * S5_GROUP
    j, k = pl.program_id(0), pl.program_id(1)
    n = j * pl.num_programs(1) + k
    last = pl.num_programs(0) * pl.num_programs(1) - 1
    slot = n % 2

    def scatters(sl):
        return [pltpu.make_async_copy(
            ybuf.at[sl, s, t],
            y_hbm.at[:, (k * pair + s) * S5_SUB + t, pl.ds(j * cols, cols)],
            sem.at[sl]) for s in range(pair) for t in range(S5_SUB)]

    @pl.when(k == 0)
    def _():
        _s5_stitch(er_ref, ei_ref, pr_ref, pi_ref, xr_scr, xi_scr, nsb)

    @pl.when(n >= 2)
    def _():
        for c in scatters(slot):
            c.wait()

    zt = jnp.concatenate([zt_ref[s] for s in range(pair)], axis=-1)
    s_all = _bdot(nn_ref[...], zt)
    ar, ai = are_ref[...], aim_ref[...]
    xr, xi = xr_scr[...], xi_scr[...]
    starts = []
    for s in range(pair):
        starts.append(jnp.concatenate([xr.reshape(gb, S5_STATE, nseg),
                                       xi.reshape(gb, S5_STATE, nseg)], axis=1))
        inc = s_all[:, :, s * nseg:(s + 1) * nseg]
        sr = inc[:, :S5_STATE, :].reshape(gb * S5_STATE, nseg)
        si = inc[:, S5_STATE:, :].reshape(gb * S5_STATE, nseg)
        xr, xi = ar * xr - ai * xi + sr, ar * xi + ai * xr + si
    xr_scr[...] = xr
    xi_scr[...] = xi
    xprev = jnp.concatenate(starts, axis=-1).astype(zt.dtype)
    y = _bdot(tt_ref[...], zt) + _bdot(mm_ref[...], xprev)
    for t in range(S5_SUB):
        blk = y[:, t * S5_GROUP:(t + 1) * S5_GROUP, :].reshape(cols, pair * nseg)
        bt = blk.T
        for s in range(pair):
            ybuf[slot, s, t] = bt[s * nseg:(s + 1) * nseg]
    for c in scatters(slot):
        c.start()

    @pl.when(n == last)
    def _():
        for c in scatters(slot):
            c.wait()

    @pl.when(jnp.logical_and(n == last, n >= 1))
    def _():
        for c in scatters(1 - slot):
            c.wait()


def _s5_core_call(zt, tt, nn, mm, a_re, a_im, er, ei, p_re, p_im, nsb, dm):
    steps, groups, width, nseg = zt.shape
    gb = min(S5_GB, groups)
    pair = 2 if steps % 2 == 0 else 1
    rows = gb * S5_STATE
    per_gb = lambda shape: pl.BlockSpec(shape, lambda j, k: (j,) + (0,) * (len(shape) - 1))
    return pl.pallas_call(
        functools.partial(_s5_core_kernel, nsb=nsb),
        grid=(groups // gb, steps // pair),
        in_specs=[pl.BlockSpec((pair, gb, width, nseg), lambda j, k: (k, j, 0, 0)),
                  per_gb((gb, width, width)), per_gb((gb, 2 * S5_STATE, width)),
                  per_gb((gb, width, 2 * S5_STATE)),
                  per_gb((rows, nseg)), per_gb((rows, nseg)), per_gb((rows, nseg)),
                  per_gb((rows, nseg)), per_gb((rows, p_re.shape[1])),
                  per_gb((rows, p_im.shape[1]))],
        out_specs=pl.BlockSpec(memory_space=pl.ANY),
        out_shape=jax.ShapeDtypeStruct((nseg, steps * S5_SUB, dm), F32),
        scratch_shapes=[pltpu.VMEM((rows, nseg), F32), pltpu.VMEM((rows, nseg), F32),
                        pltpu.VMEM((2, pair, S5_SUB, nseg, gb * S5_GROUP), F32),
                        pltpu.SemaphoreType.DMA((2,))],
        compiler_params=_cparams(("arbitrary", "arbitrary")),
        name="s5_core",
    )(zt, tt, nn, mm, a_re, a_im, er, ei, p_re, p_im)


def _ffn_rows(x, g_ref, wg_ref, wu_ref, wd_ref, acc_ref, hc):
    u = _rms(x, g_ref[...]).astype(BF16)
    hidden = wg_ref.shape[1]
    for j in range(hidden // hc):
        cols = slice(j * hc, (j + 1) * hc)
        gt = _dot(u, wg_ref[:, cols])
        up = _dot(u, wu_ref[:, cols])
        h = (jax.nn.silu(gt) * up).astype(BF16)
        part = _dot(h, wd_ref[cols, :])
        if j == 0:
            acc_ref[...] = part
        else:
            acc_ref[...] += part
    return x + acc_ref[...]


def _layer_resident(shape, layer):
    nd = len(shape)
    return pl.BlockSpec((None,) + shape, lambda *_: (layer,) + (0,) * nd,
                        pipeline_mode=pl.Buffered(1))


def _ffn_specs(row, dm, hidden, layer, hc):
    assert hidden % hc == 0
    tile = pl.BlockSpec((row, dm), lambda i: (i, 0))
    weights = [_resident((1, dm)), _layer_resident((dm, hidden), layer),
               _layer_resident((dm, hidden), layer), _layer_resident((hidden, dm), layer)]
    return tile, weights


def _s5_ffn_kernel(x_ref, y_ref, g_ref, d_ref, w_ref, b_ref, fg_ref, wg_ref, wu_ref, wd_ref,
                   o_ref, acc_ref, *, hc):
    x = x_ref[...]
    u = _rms(x, g_ref[...])
    y = jax.nn.gelu(y_ref[...] + d_ref[...] * u)
    gate = jax.nn.sigmoid(_dot(y.astype(BF16), w_ref[...]) + b_ref[...])
    o_ref[...] = _ffn_rows(x + y * gate, fg_ref, wg_ref, wu_ref, wd_ref, acc_ref, hc)


def _s5_ffn_call(x, y, g, d, w, b, fg, wg, wu, wd, layer, row, hc):
    t, dm = x.shape
    hidden = wg.shape[-1]
    tile, weights = _ffn_specs(row, dm, hidden, layer, hc)
    return pl.pallas_call(
        functools.partial(_s5_ffn_kernel, hc=hc),
        grid=(t // row,),
        in_specs=[tile, tile, _resident((1, dm)), _resident((1, dm)), _resident((dm, dm)),
                  _resident((1, dm))] + weights,
        out_specs=tile,
        out_shape=jax.ShapeDtypeStruct((t, dm), F32),
        scratch_shapes=[pltpu.VMEM((row, dm), F32)],
        compiler_params=_cparams(("parallel",)),
        name="s5_ffn",
    )(x, y, g.reshape(1, dm), d.reshape(1, dm), w, b.reshape(1, dm), fg.reshape(1, dm), wg, wu, wd)


def _s5_tab_kernel(p_ref, q_ref, c1_ref, c2_ref, b1_ref, b2_ref, bs_ref, tt_ref, nn_ref, mm_ref):
    sub = S5_SUB
    width = sub * S5_GROUP
    lane = lax.broadcasted_iota(jnp.int32, (S5_GROUP, width), 1)
    sgn = jnp.where(lax.broadcasted_iota(jnp.int32, (1, 2 * S5_STATE), 1) < S5_STATE, 1.0, -1.0)
    for gi in range(p_ref.shape[0]):
        p, q = p_ref[gi], q_ref[gi]
        c1, c2, b1, b2 = c1_ref[gi], c2_ref[gi], b1_ref[gi], b2_ref[gi]
        cl = [c1 * p[k:k + 1] + c2 * q[k:k + 1] for k in range(sub + 1)]
        mm_ref[gi] = (jnp.concatenate(cl[1:], axis=0) * sgn).astype(mm_ref.dtype)
        r = jnp.dot(bs_ref[gi], jnp.concatenate(cl[:sub], axis=0).T,
                    precision=lax.Precision.HIGHEST, preferred_element_type=F32)
        rows = [r] + [jnp.where(lane >= S5_GROUP * s, pltpu.roll(r, S5_GROUP * s, axis=1), 0.0)
                      for s in range(1, sub)]
        tt_ref[gi] = jnp.concatenate(rows, axis=0).T.astype(tt_ref.dtype)
        nt = [b1 * p[sub - 1 - s:sub - s] + b2 * q[sub - 1 - s:sub - s] for s in range(sub)]
        nn_ref[gi] = jnp.concatenate(nt, axis=0).T.astype(nn_ref.dtype)


def _s5_tab_call(p, q, c1, c2, b1, b2, bs):
    groups = p.shape[0]
    gt = S5_TAB_GROUPS if groups % S5_TAB_GROUPS == 0 else 1
    width = S5_SUB * S5_GROUP
    st2 = 2 * S5_STATE
    blk = lambda a: pl.BlockSpec((gt,) + a.shape[1:], lambda i: (i, 0, 0))
    out = lambda r, c: pl.BlockSpec((gt, r, c), lambda i: (i, 0, 0))
    return pl.pallas_call(
        _s5_tab_kernel,
        grid=(groups // gt,),
        in_specs=[blk(a) for a in (p, q, c1, c2, b1, b2, bs)],
        out_specs=[out(width, width), out(st2, width), out(width, st2)],
        out_shape=[jax.ShapeDtypeStruct((groups, width, width), BF16),
                   jax.ShapeDtypeStruct((groups, st2, width), BF16),
                   jax.ShapeDtypeStruct((groups, width, st2), BF16)],
        compiler_params=_cparams(("parallel",)),
        name="s5_tab",
    )(p, q, c1, c2, b1, b2, bs)


def _s5_tables(lam_re, lam_im, log_step, b_re, b_im, c_re, c_im, seg_len, nsb, nseg):
    sub = S5_SUB
    step = jnp.exp(log_step)[:, None]
    dre, dim = lam_re * step, lam_im * step

    def power(k):
        kk = jnp.asarray(k, F32)[None, :, None]
        mag = jnp.exp(kk * dre[:, None, :])
        return mag * jnp.cos(kk * dim[:, None, :]), mag * jnp.sin(kk * dim[:, None, :])

    pw_re, pw_im = power(jnp.arange(sub + 1))
    den = lam_re * lam_re + lam_im * lam_im
    nr, ni = pw_re[:, 1] - 1.0, pw_im[:, 1]
    cf_re = ((nr * lam_re + ni * lam_im) / den)[:, None, :]
    cf_im = ((ni * lam_re - nr * lam_im) / den)[:, None, :]
    bt_re, bt_im = b_re.transpose(0, 2, 1), b_im.transpose(0, 2, 1)
    bb_re = cf_re * bt_re - cf_im * bt_im
    bb_im = cf_re * bt_im + cf_im * bt_re
    cat = lambda a, b: jnp.concatenate([a, b], axis=-1)
    tt, nn, mm = _s5_tab_call(cat(pw_re, pw_im), cat(pw_im, pw_re), cat(c_re, c_re),
                              cat(-c_im, c_im), cat(bb_re, bb_re), cat(-bb_im, bb_im),
                              cat(bb_re, -bb_im))
    a_re = jnp.broadcast_to(pw_re[:, sub].reshape(-1, 1), (pw_re[:, sub].size, nseg))
    a_im = jnp.broadcast_to(pw_im[:, sub].reshape(-1, 1), (pw_im[:, sub].size, nseg))
    nd = max(1, (nsb - 1).bit_length())
    p_re, p_im = power(seg_len * (2 ** jnp.arange(nd)))
    p_re = p_re.transpose(0, 2, 1).reshape(-1, nd)
    p_im = p_im.transpose(0, 2, 1).reshape(-1, nd)
    return tt, nn, mm, a_re, a_im, p_re, p_im


def _s5_layer(x, batch, seq, norm_w, lam_re, lam_im, log_step, b_re, b_im, c_re, c_im, d, glu_w,
              glu_b, ffn_g, ffn_wg, ffn_wu, ffn_wd, ffn_layer, row, hc):
    t, dm = x.shape
    nseg = V7X_LANES
    seg_len = t // nseg
    assert t % nseg == 0 and seg_len % S5_SUB == 0 and seq % seg_len == 0
    nsb = seq // seg_len
    tt, nn, mm, a_re, a_im, p_re, p_im = _s5_tables(
        lam_re, lam_im, log_step, b_re, b_im, c_re, c_im, seg_len, nsb, nseg)
    zt, er, ei = _s5_pre_call(x.reshape(nseg, seg_len, dm), norm_w, nn, a_re, a_im)
    y = _s5_core_call(zt, tt, nn, mm, a_re, a_im, er, ei, p_re, p_im, nsb, dm).reshape(t, dm)
    return _s5_ffn_call(x, y, norm_w, d.reshape(-1), glu_w.astype(BF16), glu_b, ffn_g,
                        ffn_wg, ffn_wu, ffn_wd, ffn_layer, row, hc)


def _ret_kernel(x_ref, g_ref, w_ref, cr_ref, sr_ref, cb_ref, sb_ref, gn_ref, dm_ref, qd_ref, kd_ref, cd_ref,
                y_ref, state_ref, *, heads, blk):
    @pl.when(pl.program_id(1) == 0)
    def _():
        state_ref[...] = jnp.zeros_like(state_ref)

    u = _rms(x_ref[...], g_ref[...]).astype(BF16)
    cb = cb_ref[pl.ds(pl.program_id(1), 1), :]
    sb = sb_ref[pl.ds(pl.program_id(1), 1), :]
    cos = cr_ref[...] * cb - sr_ref[...] * sb
    sin = sr_ref[...] * cb + cr_ref[...] * sb
    half = RET_QK_DIM // 2
    qk_w = heads * RET_QK_DIM
    v_w = heads * RET_V_DIM
    k_scale = RET_QK_DIM ** -0.5
    row = x_ref.shape[0]

    def rotary(tq):
        t1, t2 = tq[:, :half], tq[:, half:]
        return jnp.concatenate([t1 * cos - t2 * sin, t1 * sin + t2 * cos], axis=-1)

    for h in range(heads):
        cq = h * RET_QK_DIM
        cv = 2 * qk_w + h * RET_V_DIM
        qf = rotary(_dot(u, w_ref[:, cq:cq + RET_QK_DIM]))
        q = qf.astype(BF16)
        k = rotary(_dot(u, w_ref[:, qk_w + cq:qk_w + cq + RET_QK_DIM]) * k_scale)
        v = _dot(u, w_ref[:, cv:cv + RET_V_DIM]).astype(BF16)
        gt = _dot(u, w_ref[:, v_w + cv:v_w + cv + RET_V_DIM])
        gn = gn_ref[:, h * RET_V_DIM:(h + 1) * RET_V_DIM]
        for r0 in range(0, row, blk):
            qs, ks, vs = q[r0:r0 + blk], k[r0:r0 + blk], v[r0:r0 + blk]
            s = lax.dot_general(qs, ks.astype(BF16), (((1,), (1,)), ((), ())),
                                preferred_element_type=F32)
            st = state_ref[h]
            qd = (qf[r0:r0 + blk] * qd_ref[h]).astype(BF16)
            o = _dot((s * dm_ref[h]).astype(BF16), vs) + _dot(qd, st.astype(BF16))
            kd = (ks * kd_ref[h]).astype(BF16)
            state_ref[h] = st * cd_ref[h] + lax.dot_general(
                kd, vs, (((0,), (0,)), ((), ())), preferred_element_type=F32)
            mean = jnp.mean(o, axis=-1, keepdims=True)
            cen = o - mean
            var = jnp.mean(jnp.square(cen), axis=-1, keepdims=True)
            on = cen * lax.rsqrt(var + NORM_EPS) * gn
            y_ref[r0:r0 + blk, h * RET_V_DIM:(h + 1) * RET_V_DIM] = (
                jax.nn.silu(gt[r0:r0 + blk]) * on).astype(y_ref.dtype)


def _ret_call(x, g, w, rot, gn_w, dmask, qdec, kdec, cdec, batch, seq, heads, row, blk):
    t, dm = x.shape
    v_w = heads * RET_V_DIM
    nb = seq // row
    kern = functools.partial(_ret_kernel, heads=heads, blk=blk)
    rows = lambda width: pl.BlockSpec((row, width), lambda b, i: (b * nb + i, 0))
    tabs = [_resident(a.shape) for a in rot]
    return pl.pallas_call(
        kern,
        grid=(batch, nb),
        in_specs=[rows(dm), _resident((1, dm)), _resident(w.shape)] + tabs + [
                  _resident((1, v_w)), _resident(dmask.shape), _resident(qdec.shape),
                  _resident(kdec.shape), _resident(cdec.shape)],
        out_specs=rows(v_w),
        out_shape=jax.ShapeDtypeStruct((t, v_w), BF16),
        scratch_shapes=[pltpu.VMEM((heads, RET_QK_DIM, RET_V_DIM), F32)],
        compiler_params=_cparams(("parallel", "arbitrary")),
        name="ret_mix",
    )(x, g.reshape(1, dm), w, *rot, gn_w.reshape(1, v_w), dmask, qdec, kdec, cdec)


def _ret_decays(heads, blk):
    log_gamma = jnp.log1p(-jnp.exp2(-5.0 - jnp.arange(heads, dtype=F32)))
    pos = jnp.arange(blk, dtype=F32)
    diff = pos[:, None] - pos[None, :]
    cn = (jnp.arange(blk) // CHUNK)[:, None]
    cm = (jnp.arange(blk) // CHUNK)[None, :]
    expo = jnp.where(cn == cm, jnp.abs(diff), diff)
    dmask = jnp.where((cm <= cn)[None], jnp.exp(log_gamma[:, None, None] * expo[None]), 0.0)
    qdec = jnp.exp((pos[None, :] + 1.0) * log_gamma[:, None])[..., None]
    kdec = jnp.exp((blk - 1.0 - pos)[None, :] * log_gamma[:, None])[..., None]
    cdec = jnp.exp(blk * log_gamma)[:, None, None]
    return dmask, qdec, kdec, cdec


def _rotary_tables(seq, row):
    inv_freq = 1.0 / (ROPE_BASE ** jnp.linspace(0.0, 1.0, RET_QK_DIM // 2, dtype=F32))
    ang_r = jnp.arange(row, dtype=F32)[:, None] * inv_freq[None, :]
    ang_b = (jnp.arange(seq // row, dtype=F32) * row)[:, None] * inv_freq[None, :]
    return jnp.cos(ang_r), jnp.sin(ang_r), jnp.cos(ang_b), jnp.sin(ang_b)


def _ret_ffn_kernel(x_ref, y_ref, w_ref, fg_ref, wg_ref, wu_ref, wd_ref, fin_ref, o_ref, acc_ref,
                    *, hc):
    x1 = x_ref[...] + _dot(y_ref[...], w_ref[...])
    out = _ffn_rows(x1, fg_ref, wg_ref, wu_ref, wd_ref, acc_ref, hc)
    o_ref[...] = _rms(out, fin_ref[...])


def _ret_ffn_call(x, y, w, fg, wg, wu, wd, layer, fin, row, hc):
    t, dm = x.shape
    hidden = wg.shape[-1]
    kdim = y.shape[1]
    tile, weights = _ffn_specs(row, dm, hidden, layer, hc)
    return pl.pallas_call(
        functools.partial(_ret_ffn_kernel, hc=hc),
        grid=(t // row,),
        in_specs=[tile, pl.BlockSpec((row, kdim), lambda i: (i, 0)), _resident((kdim, dm))]
        + weights + [_resident((1, dm))],
        out_specs=tile,
        out_shape=jax.ShapeDtypeStruct((t, dm), F32),
        scratch_shapes=[pltpu.VMEM((row, dm), F32)],
        compiler_params=_cparams(("parallel",)),
        name="ret_ffn",
    )(x, y, w, fg.reshape(1, dm), wg, wu, wd, fin.reshape(1, dm))


def _ret_layer(x, batch, seq, norm_w, w_qkvg, gn_w, w_o, ffn_g, ffn_wg, ffn_wu, ffn_wd, ffn_layer,
               fin, row, row_out, blk, hc):
    dm = x.shape[1]
    heads = dm // RET_QK_DIM
    rot = _rotary_tables(seq, row)
    dmask, qdec, kdec, cdec = _ret_decays(heads, blk)
    y = _ret_call(x, norm_w, w_qkvg.astype(BF16), rot, gn_w, dmask, qdec, kdec, cdec,
                  batch, seq, heads, row, blk)
    return _ret_ffn_call(x, y, w_o.astype(BF16), ffn_g, ffn_wg, ffn_wu, ffn_wd, ffn_layer, fin,
                         row_out, hc)


def kernel(x, s5_norm, s5_lambda_re, s5_lambda_im, s5_log_step, s5_b_re, s5_b_im, s5_c_re, s5_c_im,
           s5_d, s5_glu_w, s5_glu_b, ret_norm, ret_w_qkvg, ret_gn_w, ret_w_o, ffn_norm, ffn_w_gate,
           ffn_w_up, ffn_w_down, final_norm):
    batch, seq, dm = x.shape
    row, row_out, blk, hc = _tiles(seq)
    assert ffn_norm.shape[0] == 2 and s5_norm.shape[0] == 1 and ret_norm.shape[0] == 1
    h = x.reshape(batch * seq, dm)
    wg, wu, wd = ffn_w_gate.astype(BF16), ffn_w_up.astype(BF16), ffn_w_down.astype(BF16)
    h = _s5_layer(h, batch, seq, s5_norm[0], s5_lambda_re[0], s5_lambda_im[0], s5_log_step[0],
                  s5_b_re[0], s5_b_im[0], s5_c_re[0], s5_c_im[0], s5_d[0], s5_glu_w[0],
                  s5_glu_b[0], ffn_norm[0], wg, wu, wd, 0, row, hc)
    h = _ret_layer(h, batch, seq, ret_norm[0], ret_w_qkvg[0], ret_gn_w[0], ret_w_o[0],
                   ffn_norm[1], wg, wu, wd, 1, final_norm,
                   row, row_out, blk, hc)
    return h.reshape(batch, seq, dm)
```

```python
import functools

import jax
import jax.numpy as jnp
from jax import lax
from jax.experimental import pallas as pl
from jax.experimental.pallas import tpu as pltpu

F32 = jnp.float32
BF16 = jnp.bfloat16

NORM_EPS = 1e-6
CHUNK = 64
S5_GROUP = 16
S5_STATE = 64
S5_SUB = 16
S5_GB = 16
S5_TAB_GROUPS = 4
RET_QK_DIM = 256
RET_V_DIM = 512
ROPE_BASE = 10000.0

V7X_VMEM_BYTES = 64 * 1024 * 1024
V7X_LANES = 128
VMEM_LIMIT = V7X_VMEM_BYTES * 15 // 16
W_STAGE_BYTES = 3 * 256 * 1024


def _tiles(seq):
    row = min(512, seq)
    row_out = min(1024, seq)
    ret = min(256, seq)
    hc = 256
    assert seq % row == 0 and seq % row_out == 0 and row % ret == 0 and ret % CHUNK == 0
    return row, row_out, ret, hc


def _cparams(sem):
    return pltpu.CompilerParams(dimension_semantics=sem, vmem_limit_bytes=VMEM_LIMIT)


def _resident(shape):
    nd = len(shape)
    return pl.BlockSpec(shape, lambda *_: (0,) * nd, pipeline_mode=pl.Buffered(1))


def _rms(xf, g):
    ms = jnp.mean(jnp.square(xf), axis=-1, keepdims=True)
    return xf * lax.rsqrt(ms + NORM_EPS) * g


def _dot(a, b):
    return jnp.dot(a, b, preferred_element_type=F32)


def _bdot(a, b):
    return lax.dot_general(a, b, (((2,), (1,)), ((0,), (0,))), preferred_element_type=F32)


def _s5_pre_kernel(x_hbm, g_ref, nn_ref, are_ref, aim_ref, zt_ref, er_ref, ei_ref, xs, sem):
    groups = zt_ref.shape[1]
    nseg = zt_ref.shape[3]
    i = pl.program_id(0)
    steps = pl.num_programs(0)

    def gathers(step, slot):
        return [pltpu.make_async_copy(x_hbm.at[:, step * S5_SUB + t, :], xs.at[slot, t],
                                      sem.at[slot]) for t in range(S5_SUB)]

    @pl.when(i == 0)
    def _():
        for c in gathers(0, 0):
            c.start()

    @pl.when(i + 1 < steps)
    def _():
        for c in gathers(i + 1, (i + 1) % 2):
            c.start()

    slot = i % 2
    for c in gathers(i, slot):
        c.wait()
    g = g_ref[...]
    for t in range(S5_SUB):
        ut = _rms(xs[slot, t], g)
        zt_ref[0, :, t * S5_GROUP:(t + 1) * S5_GROUP, :] = (
            ut.T.reshape(groups, S5_GROUP, nseg).astype(zt_ref.dtype))
    s = _bdot(nn_ref[...], zt_ref[0])
    sr = s[:, :S5_STATE, :].reshape(groups * S5_STATE, nseg)
    si = s[:, S5_STATE:, :].reshape(groups * S5_STATE, nseg)

    @pl.when(i == 0)
    def _():
        er_ref[...] = sr
        ei_ref[...] = si

    @pl.when(i > 0)
    def _():
        er, ei = er_ref[...], ei_ref[...]
        ar, ai = are_ref[...], aim_ref[...]
        er_ref[...] = ar * er - ai * ei + sr
        ei_ref[...] = ar * ei + ai * er + si


def _s5_pre_call(x3, g, nn, a_re, a_im):
    nseg, sl, dm = x3.shape
    groups = dm // S5_GROUP
    steps = sl // S5_SUB
    width = S5_SUB * S5_GROUP
    rows = groups * S5_STATE
    return pl.pallas_call(
        _s5_pre_kernel,
        grid=(steps,),
        in_specs=[pl.BlockSpec(memory_space=pl.ANY), _resident((1, dm)),
                  _resident(nn.shape), _resident((rows, nseg)), _resident((rows, nseg))],
        out_specs=[pl.BlockSpec((1, groups, width, nseg), lambda i: (i, 0, 0, 0)),
                   pl.BlockSpec((rows, nseg), lambda i: (0, 0)),
                   pl.BlockSpec((rows, nseg), lambda i: (0, 0))],
        out_shape=[jax.ShapeDtypeStruct((steps, groups, width, nseg), BF16),
                   jax.ShapeDtypeStruct((rows, nseg), F32),
                   jax.ShapeDtypeStruct((rows, nseg), F32)],
        scratch_shapes=[pltpu.VMEM((2, S5_SUB, nseg, dm), F32), pltpu.SemaphoreType.DMA((2,))],
        compiler_params=_cparams(("arbitrary",)),
        name="s5_pre",
    )(x3, g.reshape(1, dm), nn, a_re, a_im)


def _s5_stitch(er_ref, ei_ref, pr_ref, pi_ref, xr_ref, xi_ref, nsb):
    er, ei = er_ref[...], ei_ref[...]
    lane = lax.broadcasted_iota(jnp.int32, er.shape, 1) % nsb
    d, k = 1, 0
    while d < nsb:
        sr, si = pltpu.roll(er, d, axis=1), pltpu.roll(ei, d, axis=1)
        pr, pi = pr_ref[:, k:k + 1], pi_ref[:, k:k + 1]
        ok = lane >= d
        er, ei = (er + jnp.where(ok, pr * sr - pi * si, 0.0),
                  ei + jnp.where(ok, pr * si + pi * sr, 0.0))
        d, k = 2 * d, k + 1
    ok = lane >= 1
    xr_ref[...] = jnp.where(ok, pltpu.roll(er, 1, axis=1), 0.0)
    xi_ref[...] = jnp.where(ok, pltpu.roll(ei, 1, axis=1), 0.0)


def _s5_core_kernel(zt_ref, tt_ref, nn_ref, mm_ref, are_ref, aim_ref, er_ref, ei_ref, pr_ref,
                    pi_ref, y_hbm, xr_scr, xi_scr, ybuf, sem, *, nsb):
    pair, gb, _, nseg = zt_ref.shape
    cols = gb * S5_GROUP
    j, k = pl.program_id(0), pl.program_id(1)
    n = j * pl.num_programs(1) + k
    last = pl.num_programs(0) * pl.num_programs(1) - 1
    slot = n % 2

    def scatters(sl):
        return [pltpu.make_async_copy(
            ybuf.at[sl, s, t],
            y_hbm.at[:, (k * pair + s) * S5_SUB + t, pl.ds(j * cols, cols)],
            sem.at[sl]) for s in range(pair) for t in range(S5_SUB)]

    @pl.when(k == 0)
    def _():
        _s5_stitch(er_ref, ei_ref, pr_ref, pi_ref, xr_scr, xi_scr, nsb)

    @pl.when(n >= 2)
    def _():
        for c in scatters(slot):
            c.wait()

    zt = jnp.concatenate([zt_ref[s] for s in range(pair)], axis=-1)
    s_all = _bdot(nn_ref[...], zt)
    ar, ai = are_ref[...], aim_ref[...]
    xr, xi = xr_scr[...], xi_scr[...]
    starts = []
    for s in range(pair):
        starts.append(jnp.concatenate([xr.reshape(gb, S5_STATE, nseg),
                                       xi.reshape(gb, S5_STATE, nseg)], axis=1))
        inc = s_all[:, :, s * nseg:(s + 1) * nseg]
        sr = inc[:, :S5_STATE, :].reshape(gb * S5_STATE, nseg)
        si = inc[:, S5_STATE:, :].reshape(gb * S5_STATE, nseg)
        xr, xi = ar * xr - ai * xi + sr, ar * xi + ai * xr + si
    xr_scr[...] = xr
    xi_scr[...] = xi
    xprev = jnp.concatenate(starts, axis=-1).astype(zt.dtype)
    y = _bdot(tt_ref[...], zt) + _bdot(mm_ref[...], xprev)
    for t in range(S5_SUB):
        blk = y[:, t * S5_GROUP:(t + 1) * S5_GROUP, :].reshape(cols, pair * nseg)
        bt = blk.T
        for s in range(pair):
            ybuf[slot, s, t] = bt[s * nseg:(s + 1) * nseg]
    for c in scatters(slot):
        c.start()

    @pl.when(n == last)
    def _():
        for c in scatters(slot):
            c.wait()

    @pl.when(jnp.logical_and(n == last, n >= 1))
    def _():
        for c in scatters(1 - slot):
            c.wait()


def _s5_core_call(zt, tt, nn, mm, a_re, a_im, er, ei, p_re, p_im, nsb, dm):
    steps, groups, width, nseg = zt.shape
    gb = min(S5_GB, groups)
    pair = 2 if steps % 2 == 0 else 1
    rows = gb * S5_STATE
    per_gb = lambda shape: pl.BlockSpec(shape, lambda j, k: (j,) + (0,) * (len(shape) - 1))
    return pl.pallas_call(
        functools.partial(_s5_core_kernel, nsb=nsb),
        grid=(groups // gb, steps // pair),
        in_specs=[pl.BlockSpec((pair, gb, width, nseg), lambda j, k: (k, j, 0, 0)),
                  per_gb((gb, width, width)), per_gb((gb, 2 * S5_STATE, width)),
                  per_gb((gb, width, 2 * S5_STATE)),
                  per_gb((rows, nseg)), per_gb((rows, nseg)), per_gb((rows, nseg)),
                  per_gb((rows, nseg)), per_gb((rows, p_re.shape[1])),
                  per_gb((rows, p_im.shape[1]))],
        out_specs=pl.BlockSpec(memory_space=pl.ANY),
        out_shape=jax.ShapeDtypeStruct((nseg, steps * S5_SUB, dm), F32),
        scratch_shapes=[pltpu.VMEM((rows, nseg), F32), pltpu.VMEM((rows, nseg), F32),
                        pltpu.VMEM((2, pair, S5_SUB, nseg, gb * S5_GROUP), F32),
                        pltpu.SemaphoreType.DMA((2,))],
        compiler_params=_cparams(("arbitrary", "arbitrary")),
        name="s5_core",
    )(zt, tt, nn, mm, a_re, a_im, er, ei, p_re, p_im)


def _ffn_rows(x, g_ref, wg_ref, wu_ref, wd_ref, acc_ref, hc):
    u = _rms(x, g_ref[...]).astype(BF16)
    hidden = wg_ref.shape[1]
    for j in range(hidden // hc):
        cols = slice(j * hc, (j + 1) * hc)
        gt = _dot(u, wg_ref[:, cols])
        up = _dot(u, wu_ref[:, cols])
        h = (jax.nn.silu(gt) * up).astype(BF16)
        part = _dot(h, wd_ref[cols, :])
        if j == 0:
            acc_ref[...] = part
        else:
            acc_ref[...] += part
    return x + acc_ref[...]


def _stage_rows(k, n):
    fits = [r for r in range(16, k + 1, 16) if k % r == 0 and r * n * 4 <= W_STAGE_BYTES]
    return fits[-1]


def _stage_scratch(shapes):
    rows = {}
    for k, n in shapes:
        rows[n] = max(rows.get(n, 0), _stage_rows(k, n))
    return {n: pltpu.VMEM((2, r, n), F32) for n, r in sorted(rows.items())}


def _load_bf16(w_hbm, layer, w_vmem, stage, sem):
    k, n = w_vmem.shape
    rows = _stage_rows(k, n)

    def chunk(c, slot):
        return pltpu.make_async_copy(w_hbm.at[layer, pl.ds(c * rows, rows), :],
                                     stage.at[slot, pl.ds(0, rows), :], sem.at[slot])

    chunk(0, 0).start()
    for c in range(k // rows):
        if c + 1 < k // rows:
            chunk(c + 1, (c + 1) % 2).start()
        chunk(c, c % 2).wait()
        w_vmem[c * rows:(c + 1) * rows, :] = stage[c % 2, 0:rows, :].astype(w_vmem.dtype)


def _ffn_scratch(row, dm, hidden, extra):
    mats = list(extra) + [(dm, hidden), (dm, hidden), (hidden, dm)]
    stages = _stage_scratch(mats)
    scratch = ([pltpu.VMEM((row, dm), F32)] + [pltpu.VMEM(m, BF16) for m in mats]
               + list(stages.values()) + [pltpu.SemaphoreType.DMA((2,))])
    return scratch, list(stages.keys())


def _s5_ffn_kernel(x_ref, y_ref, g_ref, d_ref, b_ref, fg_ref, w_hbm, wg_hbm, wu_hbm, wd_hbm,
                   o_ref, acc_ref, w_ref, wg_ref, wu_ref, wd_ref, *stage_and_sem, hc, layer, widths):
    stage = dict(zip(widths, stage_and_sem[:-1]))
    sem = stage_and_sem[-1]

    @pl.when(pl.program_id(0) == 0)
    def _():
        for hbm, lyr, ref in ((w_hbm, 0, w_ref), (wg_hbm, layer, wg_ref), (wu_hbm, layer, wu_ref),
                              (wd_hbm, layer, wd_ref)):
            _load_bf16(hbm, lyr, ref, stage[ref.shape[1]], sem)

    x = x_ref[...]
    u = _rms(x, g_ref[...])
    y = jax.nn.gelu(y_ref[...] + d_ref[...] * u)
    gate = jax.nn.sigmoid(_dot(y.astype(BF16), w_ref[...]) + b_ref[...])
    o_ref[...] = _ffn_rows(x + y * gate, fg_ref, wg_ref, wu_ref, wd_ref, acc_ref, hc)


def _s5_ffn_call(x, y, g, d, w, b, fg, wg, wu, wd, layer, row, hc):
    t, dm = x.shape
    hidden = wg.shape[-1]
    assert hidden % hc == 0
    tile = pl.BlockSpec((row, dm), lambda i: (i, 0))
    vec = _resident((1, dm))
    hbm = pl.BlockSpec(memory_space=pl.ANY)
    scratch, widths = _ffn_scratch(row, dm, hidden, [(dm, dm)])
    return pl.pallas_call(
        functools.partial(_s5_ffn_kernel, hc=hc, layer=layer, widths=widths),
        grid=(t // row,),
        in_specs=[tile, tile, vec, vec, vec, vec, hbm, hbm, hbm, hbm],
        out_specs=tile,
        out_shape=jax.ShapeDtypeStruct((t, dm), F32),
        scratch_shapes=scratch,
        compiler_params=_cparams(("arbitrary",)),
        name="s5_ffn",
    )(x, y, g.reshape(1, dm), d.reshape(1, dm), b.reshape(1, dm), fg.reshape(1, dm), w, wg, wu, wd)


def _s5_tab_kernel(p_ref, q_ref, c1_ref, c2_ref, b1_ref, b2_ref, bs_ref, tt_ref, nn_ref, mm_ref):
    sub = S5_SUB
    width = sub * S5_GROUP
    lane = lax.broadcasted_iota(jnp.int32, (S5_GROUP, width), 1)
    sgn = jnp.where(lax.broadcasted_iota(jnp.int32, (1, 2 * S5_STATE), 1) < S5_STATE, 1.0, -1.0)
    for gi in range(p_ref.shape[0]):
        p, q = p_ref[gi], q_ref[gi]
        c1, c2, b1, b2 = c1_ref[gi], c2_ref[gi], b1_ref[gi], b2_ref[gi]
        cl = [c1 * p[k:k + 1] + c2 * q[k:k + 1] for k in range(sub + 1)]
        mm_ref[gi] = (jnp.concatenate(cl[1:], axis=0) * sgn).astype(mm_ref.dtype)
        r = jnp.dot(bs_ref[gi], jnp.concatenate(cl[:sub], axis=0).T,
                    precision=lax.Precision.HIGHEST, preferred_element_type=F32)
        rows = [r] + [jnp.where(lane >= S5_GROUP * s, pltpu.roll(r, S5_GROUP * s, axis=1), 0.0)
                      for s in range(1, sub)]
        tt_ref[gi] = jnp.concatenate(rows, axis=0).T.astype(tt_ref.dtype)
        nt = [b1 * p[sub - 1 - s:sub - s] + b2 * q[sub - 1 - s:sub - s] for s in range(sub)]
        nn_ref[gi] = jnp.concatenate(nt, axis=0).T.astype(nn_ref.dtype)


def _s5_tab_call(p, q, c1, c2, b1, b2, bs):
    groups = p.shape[0]
    gt = S5_TAB_GROUPS if groups % S5_TAB_GROUPS == 0 else 1
    width = S5_SUB * S5_GROUP
    st2 = 2 * S5_STATE
    blk = lambda a: pl.BlockSpec((gt,) + a.shape[1:], lambda i: (i, 0, 0))
    out = lambda r, c: pl.BlockSpec((gt, r, c), lambda i: (i, 0, 0))
    return pl.pallas_call(
        _s5_tab_kernel,
        grid=(groups // gt,),
        in_specs=[blk(a) for a in (p, q, c1, c2, b1, b2, bs)],
        out_specs=[out(width, width), out(st2, width), out(width, st2)],
        out_shape=[jax.ShapeDtypeStruct((groups, width, width), BF16),
                   jax.ShapeDtypeStruct((groups, st2, width), BF16),
                   jax.ShapeDtypeStruct((groups, width, st2), BF16)],
        compiler_params=_cparams(("parallel",)),
        name="s5_tab",
    )(p, q, c1, c2, b1, b2, bs)


def _s5_tables(lam_re, lam_im, log_step, b_re, b_im, c_re, c_im, seg_len, nsb, nseg):
    sub = S5_SUB
    step = jnp.exp(log_step)[:, None]
    dre, dim = lam_re * step, lam_im * step

    def power(k):
        kk = jnp.asarray(k, F32)[None, :, None]
        mag = jnp.exp(kk * dre[:, None, :])
        return mag * jnp.cos(kk * dim[:, None, :]), mag * jnp.sin(kk * dim[:, None, :])

    pw_re, pw_im = power(jnp.arange(sub + 1))
    den = lam_re * lam_re + lam_im * lam_im
    nr, ni = pw_re[:, 1] - 1.0, pw_im[:, 1]
    cf_re = ((nr * lam_re + ni * lam_im) / den)[:, None, :]
    cf_im = ((ni * lam_re - nr * lam_im) / den)[:, None, :]
    bt_re, bt_im = b_re.transpose(0, 2, 1), b_im.transpose(0, 2, 1)
    bb_re = cf_re * bt_re - cf_im * bt_im
    bb_im = cf_re * bt_im + cf_im * bt_re
    cat = lambda a, b: jnp.concatenate([a, b], axis=-1)
    tt, nn, mm = _s5_tab_call(cat(pw_re, pw_im), cat(pw_im, pw_re), cat(c_re, c_re),
                              cat(-c_im, c_im), cat(bb_re, bb_re), cat(-bb_im, bb_im),
                              cat(bb_re, -bb_im))
    a_re = jnp.broadcast_to(pw_re[:, sub].reshape(-1, 1), (pw_re[:, sub].size, nseg))
    a_im = jnp.broadcast_to(pw_im[:, sub].reshape(-1, 1), (pw_im[:, sub].size, nseg))
    nd = max(1, (nsb - 1).bit_length())
    p_re, p_im = power(seg_len * (2 ** jnp.arange(nd)))
    p_re = p_re.transpose(0, 2, 1).reshape(-1, nd)
    p_im = p_im.transpose(0, 2, 1).reshape(-1, nd)
    return tt, nn, mm, a_re, a_im, p_re, p_im


def _s5_layer(x, batch, seq, norm_w, lam_re, lam_im, log_step, b_re, b_im, c_re, c_im, d, glu_w,
              glu_b, ffn_g, ffn_wg, ffn_wu, ffn_wd, ffn_layer, row, hc):
    t, dm = x.shape
    nseg = V7X_LANES
    seg_len = t // nseg
    assert t % nseg == 0 and seg_len % S5_SUB == 0 and seq % seg_len == 0
    nsb = seq // seg_len
    tt, nn, mm, a_re, a_im, p_re, p_im = _s5_tables(
        lam_re, lam_im, log_step, b_re, b_im, c_re, c_im, seg_len, nsb, nseg)
    zt, er, ei = _s5_pre_call(x.reshape(nseg, seg_len, dm), norm_w, nn, a_re, a_im)
    y = _s5_core_call(zt, tt, nn, mm, a_re, a_im, er, ei, p_re, p_im, nsb, dm).reshape(t, dm)
    return _s5_ffn_call(x, y, norm_w, d.reshape(-1), glu_w, glu_b, ffn_g,
                        ffn_wg, ffn_wu, ffn_wd, ffn_layer, row, hc)


def _ret_kernel(x_ref, g_ref, w_hbm, cr_ref, sr_ref, cb_ref, sb_ref, gn_ref, dm_ref, qd_ref, kd_ref,
                cd_ref, y_ref, state_ref, w_ref, stage, sem, *, heads, blk):
    @pl.when(jnp.logical_and(pl.program_id(0) == 0, pl.program_id(1) == 0))
    def _():
        _load_bf16(w_hbm, 0, w_ref, stage, sem)

    @pl.when(pl.program_id(1) == 0)
    def _():
        state_ref[...] = jnp.zeros_like(state_ref)

    u = _rms(x_ref[...], g_ref[...]).astype(BF16)
    cb = cb_ref[pl.ds(pl.program_id(1), 1), :]
    sb = sb_ref[pl.ds(pl.program_id(1), 1), :]
    cos = cr_ref[...] * cb - sr_ref[...] * sb
    sin = sr_ref[...] * cb + cr_ref[...] * sb
    half = RET_QK_DIM // 2
    qk_w = heads * RET_QK_DIM
    v_w = heads * RET_V_DIM
    k_scale = RET_QK_DIM ** -0.5
    row = x_ref.shape[0]

    def rotary(tq):
        t1, t2 = tq[:, :half], tq[:, half:]
        return jnp.concatenate([t1 * cos - t2 * sin, t1 * sin + t2 * cos], axis=-1)

    for h in range(heads):
        cq = h * RET_QK_DIM
        cv = 2 * qk_w + h * RET_V_DIM
        qf = rotary(_dot(u, w_ref[:, cq:cq + RET_QK_DIM]))
        q = qf.astype(BF16)
        k = rotary(_dot(u, w_ref[:, qk_w + cq:qk_w + cq + RET_QK_DIM]) * k_scale)
        v = _dot(u, w_ref[:, cv:cv + RET_V_DIM]).astype(BF16)
        gt = _dot(u, w_ref[:, v_w + cv:v_w + cv + RET_V_DIM])
        gn = gn_ref[:, h * RET_V_DIM:(h + 1) * RET_V_DIM]
        for r0 in range(0, row, blk):
            qs, ks, vs = q[r0:r0 + blk], k[r0:r0 + blk], v[r0:r0 + blk]
            s = lax.dot_general(qs, ks.astype(BF16), (((1,), (1,)), ((), ())),
                                preferred_element_type=F32)
            st = state_ref[h]
            qd = (qf[r0:r0 + blk] * qd_ref[h]).astype(BF16)
            o = _dot((s * dm_ref[h]).astype(BF16), vs) + _dot(qd, st.astype(BF16))
            kd = (ks * kd_ref[h]).astype(BF16)
            state_ref[h] = st * cd_ref[h] + lax.dot_general(
                kd, vs, (((0,), (0,)), ((), ())), preferred_element_type=F32)
            mean = jnp.mean(o, axis=-1, keepdims=True)
            cen = o - mean
            var = jnp.mean(jnp.square(cen), axis=-1, keepdims=True)
            on = cen * lax.rsqrt(var + NORM_EPS) * gn
            y_ref[r0:r0 + blk, h * RET_V_DIM:(h + 1) * RET_V_DIM] = (
                jax.nn.silu(gt[r0:r0 + blk]) * on).astype(y_ref.dtype)


def _ret_call(x, g, w, rot, gn_w, dmask, qdec, kdec, cdec, batch, seq, heads, row, blk):
    t, dm = x.shape
    v_w = heads * RET_V_DIM
    nb = seq // row
    kern = functools.partial(_ret_kernel, heads=heads, blk=blk)
    rows = lambda width: pl.BlockSpec((row, width), lambda b, i: (b * nb + i, 0))
    tabs = [_resident(a.shape) for a in rot]
    return pl.pallas_call(
        kern,
        grid=(batch, nb),
        in_specs=[rows(dm), _resident((1, dm)), pl.BlockSpec(memory_space=pl.ANY)] + tabs + [
                  _resident((1, v_w)), _resident(dmask.shape), _resident(qdec.shape),
                  _resident(kdec.shape), _resident(cdec.shape)],
        out_specs=rows(v_w),
        out_shape=jax.ShapeDtypeStruct((t, v_w), BF16),
        scratch_shapes=[pltpu.VMEM((heads, RET_QK_DIM, RET_V_DIM), F32), pltpu.VMEM(w.shape[1:], BF16)]
        + list(_stage_scratch([w.shape[1:]]).values()) + [pltpu.SemaphoreType.DMA((2,))],
        compiler_params=_cparams(("arbitrary", "arbitrary")),
        name="ret_mix",
    )(x, g.reshape(1, dm), w, *rot, gn_w.reshape(1, v_w), dmask, qdec, kdec, cdec)


def _ret_decays(heads, blk):
    log_gamma = jnp.log1p(-jnp.exp2(-5.0 - jnp.arange(heads, dtype=F32)))
    pos = jnp.arange(blk, dtype=F32)
    diff = pos[:, None] - pos[None, :]
    cn = (jnp.arange(blk) // CHUNK)[:, None]
    cm = (jnp.arange(blk) // CHUNK)[None, :]
    expo = jnp.where(cn == cm, jnp.abs(diff), diff)
    dmask = jnp.where((cm <= cn)[None], jnp.exp(log_gamma[:, None, None] * expo[None]), 0.0)
    qdec = jnp.exp((pos[None, :] + 1.0) * log_gamma[:, None])[..., None]
    kdec = jnp.exp((blk - 1.0 - pos)[None, :] * log_gamma[:, None])[..., None]
    cdec = jnp.exp(blk * log_gamma)[:, None, None]
    return dmask, qdec, kdec, cdec


def _rotary_tables(seq, row):
    inv_freq = 1.0 / (ROPE_BASE ** jnp.linspace(0.0, 1.0, RET_QK_DIM // 2, dtype=F32))
    ang_r = jnp.arange(row, dtype=F32)[:, None] * inv_freq[None, :]
    ang_b = (jnp.arange(seq // row, dtype=F32) * row)[:, None] * inv_freq[None, :]
    return jnp.cos(ang_r), jnp.sin(ang_r), jnp.cos(ang_b), jnp.sin(ang_b)


def _ret_ffn_kernel(x_ref, y_ref, fg_ref, fin_ref, w_hbm, wg_hbm, wu_hbm, wd_hbm, o_ref, acc_ref,
                    w_ref, wg_ref, wu_ref, wd_ref, *stage_and_sem, hc, layer, widths):
    stage = dict(zip(widths, stage_and_sem[:-1]))
    sem = stage_and_sem[-1]

    @pl.when(pl.program_id(0) == 0)
    def _():
        for hbm, lyr, ref in ((w_hbm, 0, w_ref), (wg_hbm, layer, wg_ref), (wu_hbm, layer, wu_ref),
                              (wd_hbm, layer, wd_ref)):
            _load_bf16(hbm, lyr, ref, stage[ref.shape[1]], sem)

    x1 = x_ref[...] + _dot(y_ref[...], w_ref[...])
    out = _ffn_rows(x1, fg_ref, wg_ref, wu_ref, wd_ref, acc_ref, hc)
    o_ref[...] = _rms(out, fin_ref[...])


def _ret_ffn_call(x, y, w, fg, wg, wu, wd, layer, fin, row, hc):
    t, dm = x.shape
    hidden = wg.shape[-1]
    kdim = y.shape[1]
    assert hidden % hc == 0
    tile = pl.BlockSpec((row, dm), lambda i: (i, 0))
    vec = _resident((1, dm))
    hbm = pl.BlockSpec(memory_space=pl.ANY)
    scratch, widths = _ffn_scratch(row, dm, hidden, [(kdim, dm)])
    return pl.pallas_call(
        functools.partial(_ret_ffn_kernel, hc=hc, layer=layer, widths=widths),
        grid=(t // row,),
        in_specs=[tile, pl.BlockSpec((row, kdim), lambda i: (i, 0)), vec, vec, hbm, hbm, hbm, hbm],
        out_specs=tile,
        out_shape=jax.ShapeDtypeStruct((t, dm), F32),
        scratch_shapes=scratch,
        compiler_params=_cparams(("arbitrary",)),
        name="ret_ffn",
    )(x, y, fg.reshape(1, dm), fin.reshape(1, dm), w, wg, wu, wd)


def _ret_layer(x, batch, seq, norm_w, w_qkvg, gn_w, w_o, ffn_g, ffn_wg, ffn_wu, ffn_wd, ffn_layer,
               fin, row, row_out, blk, hc):
    dm = x.shape[1]
    heads = dm // RET_QK_DIM
    rot = _rotary_tables(seq, row)
    dmask, qdec, kdec, cdec = _ret_decays(heads, blk)
    y = _ret_call(x, norm_w, w_qkvg, rot, gn_w, dmask, qdec, kdec, cdec,
                  batch, seq, heads, row, blk)
    return _ret_ffn_call(x, y, w_o, ffn_g, ffn_wg, ffn_wu, ffn_wd, ffn_layer, fin,
                         row_out, hc)


def kernel(x, s5_norm, s5_lambda_re, s5_lambda_im, s5_log_step, s5_b_re, s5_b_im, s5_c_re, s5_c_im,
           s5_d, s5_glu_w, s5_glu_b, ret_norm, ret_w_qkvg, ret_gn_w, ret_w_o, ffn_norm, ffn_w_gate,
           ffn_w_up, ffn_w_down, final_norm):
    batch, seq, dm = x.shape
    row, row_out, blk, hc = _tiles(seq)
    assert ffn_norm.shape[0] == 2 and s5_norm.shape[0] == 1 and ret_norm.shape[0] == 1
    h = x.reshape(batch * seq, dm)
    h = _s5_layer(h, batch, seq, s5_norm[0], s5_lambda_re[0], s5_lambda_im[0], s5_log_step[0],
                  s5_b_re[0], s5_b_im[0], s5_c_re[0], s5_c_im[0], s5_d[0], s5_glu_w,
                  s5_glu_b[0], ffn_norm[0], ffn_w_gate, ffn_w_up, ffn_w_down, 0, row, hc)
    h = _ret_layer(h, batch, seq, ret_norm[0], ret_w_qkvg, ret_gn_w[0], ret_w_o,
                   ffn_norm[1], ffn_w_gate, ffn_w_up, ffn_w_down, 1, final_norm,
                   row, row_out, blk, hc)
    return h.reshape(batch, seq, dm)
```

```python
import functools

import jax
import jax.numpy as jnp
from jax import lax
from jax.experimental import pallas as pl
from jax.experimental.pallas import tpu as pltpu

F32 = jnp.float32
BF16 = jnp.bfloat16

NORM_EPS = 1e-6
CHUNK = 64
S5_GROUP = 16
S5_STATE = 64
S5_SUB = 16
S5_GB = 16
S5_TAB_GROUPS = 16
RET_QK_DIM = 256
RET_V_DIM = 512
ROPE_BASE = 10000.0

V7X_VMEM_BYTES = 64 * 1024 * 1024
V7X_LANES = 128
VMEM_LIMIT = V7X_VMEM_BYTES * 7 // 8


def _tiles(seq):
    row = min(512, seq)
    row_out = min(1024, seq)
    ret = min(256, seq)
    hc = 256
    assert seq % row == 0 and seq % row_out == 0 and row % ret == 0 and ret % CHUNK == 0
    return row, row_out, ret, hc


def _cparams(sem):
    return pltpu.CompilerParams(dimension_semantics=sem, vmem_limit_bytes=VMEM_LIMIT)


def _resident(shape):
    nd = len(shape)
    return pl.BlockSpec(shape, lambda *_: (0,) * nd, pipeline_mode=pl.Buffered(1))


def _rms(xf, g):
    ms = jnp.mean(jnp.square(xf), axis=-1, keepdims=True)
    return xf * lax.rsqrt(ms + NORM_EPS) * g


def _dot(a, b):
    return jnp.dot(a, b, preferred_element_type=F32)


def _bdot(a, b):
    return lax.dot_general(a, b, (((2,), (1,)), ((0,), (0,))), preferred_element_type=F32)


def _s5_pre_kernel(x_hbm, g_ref, nn_ref, are_ref, aim_ref, zt_ref, er_ref, ei_ref, xs, sem):
    groups = zt_ref.shape[1]
    nseg = zt_ref.shape[3]
    i = pl.program_id(0)
    steps = pl.num_programs(0)

    def gathers(step, slot):
        return [pltpu.make_async_copy(x_hbm.at[:, step * S5_SUB + t, :], xs.at[slot, t],
                                      sem.at[slot]) for t in range(S5_SUB)]

    @pl.when(i == 0)
    def _():
        for c in gathers(0, 0):
            c.start()

    @pl.when(i + 1 < steps)
    def _():
        for c in gathers(i + 1, (i + 1) % 2):
            c.start()

    slot = i % 2
    for c in gathers(i, slot):
        c.wait()
    g = g_ref[...]
    for t in range(S5_SUB):
        ut = _rms(xs[slot, t], g)
        zt_ref[0, :, t * S5_GROUP:(t + 1) * S5_GROUP, :] = (
            ut.T.reshape(groups, S5_GROUP, nseg).astype(zt_ref.dtype))
    s = _bdot(nn_ref[...], zt_ref[0])
    sr = s[:, :S5_STATE, :].reshape(groups * S5_STATE, nseg)
    si = s[:, S5_STATE:, :].reshape(groups * S5_STATE, nseg)

    @pl.when(i == 0)
    def _():
        er_ref[...] = sr
        ei_ref[...] = si

    @pl.when(i > 0)
    def _():
        er, ei = er_ref[...], ei_ref[...]
        ar, ai = are_ref[...], aim_ref[...]
        er_ref[...] = ar * er - ai * ei + sr
        ei_ref[...] = ar * ei + ai * er + si


def _s5_pre_call(x3, g, nn, a_re, a_im):
    nseg, sl, dm = x3.shape
    groups = dm // S5_GROUP
    steps = sl // S5_SUB
    width = S5_SUB * S5_GROUP
    rows = groups * S5_STATE
    return pl.pallas_call(
        _s5_pre_kernel,
        grid=(steps,),
        in_specs=[pl.BlockSpec(memory_space=pl.ANY), _resident((1, dm)),
                  _resident(nn.shape), _resident((rows, nseg)), _resident((rows, nseg))],
        out_specs=[pl.BlockSpec((1, groups, width, nseg), lambda i: (i, 0, 0, 0)),
                   pl.BlockSpec((rows, nseg), lambda i: (0, 0)),
                   pl.BlockSpec((rows, nseg), lambda i: (0, 0))],
        out_shape=[jax.ShapeDtypeStruct((steps, groups, width, nseg), BF16),
                   jax.ShapeDtypeStruct((rows, nseg), F32),
                   jax.ShapeDtypeStruct((rows, nseg), F32)],
        scratch_shapes=[pltpu.VMEM((2, S5_SUB, nseg, dm), F32), pltpu.SemaphoreType.DMA((2,))],
        compiler_params=_cparams(("arbitrary",)),
        name="s5_pre",
    )(x3, g.reshape(1, dm), nn, a_re, a_im)


def _s5_stitch(er_ref, ei_ref, pr_ref, pi_ref, xr_ref, xi_ref, nsb):
    er, ei = er_ref[...], ei_ref[...]
    lane = lax.broadcasted_iota(jnp.int32, er.shape, 1) % nsb
    d, k = 1, 0
    while d < nsb:
        sr, si = pltpu.roll(er, d, axis=1), pltpu.roll(ei, d, axis=1)
        pr, pi = pr_ref[:, k:k + 1], pi_ref[:, k:k + 1]
        ok = lane >= d
        er, ei = (er + jnp.where(ok, pr * sr - pi * si, 0.0),
                  ei + jnp.where(ok, pr * si + pi * sr, 0.0))
        d, k = 2 * d, k + 1
    ok = lane >= 1
    xr_ref[...] = jnp.where(ok, pltpu.roll(er, 1, axis=1), 0.0)
    xi_ref[...] = jnp.where(ok, pltpu.roll(ei, 1, axis=1), 0.0)


def _s5_core_kernel(zt_ref, tt_ref, nn_ref, mm_ref, are_ref, aim_ref, er_ref, ei_ref, pr_ref,
                    pi_ref, y_hbm, xr_scr, xi_scr, ybuf, sem, *, nsb):
    pair, gb, _, nseg = zt_ref.shape
    cols = gb * S5_GROUP
    j, k = pl.program_id(0), pl.program_id(1)
    n = j * pl.num_programs(1) + k
    last = pl.num_programs(0) * pl.num_programs(1) - 1
    slot = n % 2

    def scatters(sl):
        return [pltpu.make_async_copy(
            ybuf.at[sl, s, t],
            y_hbm.at[:, (k * pair + s) * S5_SUB + t, pl.ds(j * cols, cols)],
            sem.at[sl]) for s in range(pair) for t in range(S5_SUB)]

    @pl.when(k == 0)
    def _():
        _s5_stitch(er_ref, ei_ref, pr_ref, pi_ref, xr_scr, xi_scr, nsb)

    @pl.when(n >= 2)
    def _():
        for c in scatters(slot):
            c.wait()

    zt = jnp.concatenate([zt_ref[s] for s in range(pair)], axis=-1)
    s_all = _bdot(nn_ref[...], zt)
    ar, ai = are_ref[...], aim_ref[...]
    xr, xi = xr_scr[...], xi_scr[...]
    starts = []
    for s in range(pair):
        starts.append(jnp.concatenate([xr.reshape(gb, S5_STATE, nseg),
                                       xi.reshape(gb, S5_STATE, nseg)], axis=1))
        inc = s_all[:, :, s * nseg:(s + 1) * nseg]
        sr = inc[:, :S5_STATE, :].reshape(gb * S5_STATE, nseg)
        si = inc[:, S5_STATE:, :].reshape(gb * S5_STATE, nseg)
        xr, xi = ar * xr - ai * xi + sr, ar * xi + ai * xr + si
    xr_scr[...] = xr
    xi_scr[...] = xi
    xprev = jnp.concatenate(starts, axis=-1).astype(zt.dtype)
    y = _bdot(tt_ref[...], zt) + _bdot(mm_ref[...], xprev)
    for t in range(S5_SUB):
        blk = y[:, t * S5_GROUP:(t + 1) * S5_GROUP, :].reshape(cols, pair * nseg)
        bt = blk.T
        for s in range(pair):
            ybuf[slot, s, t] = bt[s * nseg:(s + 1) * nseg]
    for c in scatters(slot):
        c.start()

    @pl.when(n == last)
    def _():
        for c in scatters(slot):
            c.wait()

    @pl.when(jnp.logical_and(n == last, n >= 1))
    def _():
        for c in scatters(1 - slot):
            c.wait()


def _s5_core_call(zt, tt, nn, mm, a_re, a_im, er, ei, p_re, p_im, nsb, dm):
    steps, groups, width, nseg = zt.shape
    gb = min(S5_GB, groups)
    pair = max(p for p in (1, 2, 4) if steps % p == 0)
    rows = gb * S5_STATE
    per_gb = lambda shape: pl.BlockSpec(shape, lambda j, k: (j,) + (0,) * (len(shape) - 1))
    return pl.pallas_call(
        functools.partial(_s5_core_kernel, nsb=nsb),
        grid=(groups // gb, steps // pair),
        in_specs=[pl.BlockSpec((pair, gb, width, nseg), lambda j, k: (k, j, 0, 0)),
                  per_gb((gb, width, width)), per_gb((gb, 2 * S5_STATE, width)),
                  per_gb((gb, width, 2 * S5_STATE)),
                  per_gb((rows, nseg)), per_gb((rows, nseg)), per_gb((rows, nseg)),
                  per_gb((rows, nseg)), per_gb((rows, p_re.shape[1])),
                  per_gb((rows, p_im.shape[1]))],
        out_specs=pl.BlockSpec(memory_space=pl.ANY),
        out_shape=jax.ShapeDtypeStruct((nseg, steps * S5_SUB, dm), F32),
        scratch_shapes=[pltpu.VMEM((rows, nseg), F32), pltpu.VMEM((rows, nseg), F32),
                        pltpu.VMEM((2, pair, S5_SUB, nseg, gb * S5_GROUP), F32),
                        pltpu.SemaphoreType.DMA((2,))],
        compiler_params=_cparams(("arbitrary", "arbitrary")),
        name="s5_core",
    )(zt, tt, nn, mm, a_re, a_im, er, ei, p_re, p_im)


def _ffn_rows(x, g_ref, wg_ref, wu_ref, wd_ref, acc_ref, hc):
    u = _rms(x, g_ref[...]).astype(BF16)
    hidden = wg_ref.shape[1]
    for j in range(hidden // hc):
        cols = slice(j * hc, (j + 1) * hc)
        gt = _dot(u, wg_ref[:, cols])
        up = _dot(u, wu_ref[:, cols])
        h = (jax.nn.silu(gt) * up).astype(BF16)
        part = _dot(h, wd_ref[cols, :])
        if j == 0:
            acc_ref[...] = part
        else:
            acc_ref[...] += part
    return x + acc_ref[...]


def _layer_resident(shape, layer):
    nd = len(shape)
    return pl.BlockSpec((None,) + shape, lambda *_: (layer,) + (0,) * nd,
                        pipeline_mode=pl.Buffered(1))


def _ffn_specs(row, dm, hidden, layer, hc):
    assert hidden % hc == 0
    tile = pl.BlockSpec((row, dm), lambda i: (i, 0))
    weights = [_resident((1, dm)), _layer_resident((dm, hidden), layer),
               _layer_resident((dm, hidden), layer), _layer_resident((hidden, dm), layer)]
    return tile, weights


def _s5_ffn_kernel(x_ref, y_ref, g_ref, d_ref, w_ref, b_ref, fg_ref, wg_ref, wu_ref, wd_ref,
                   o_ref, acc_ref, *, hc):
    x = x_ref[...]
    u = _rms(x, g_ref[...])
    y = jax.nn.gelu(y_ref[...] + d_ref[...] * u)
    gate = jax.nn.sigmoid(_dot(y.astype(BF16), w_ref[...]) + b_ref[...])
    o_ref[...] = _ffn_rows(x + y * gate, fg_ref, wg_ref, wu_ref, wd_ref, acc_ref, hc)


def _s5_ffn_call(x, y, g, d, w, b, fg, wg, wu, wd, layer, row, hc):
    t, dm = x.shape
    hidden = wg.shape[-1]
    tile, weights = _ffn_specs(row, dm, hidden, layer, hc)
    return pl.pallas_call(
        functools.partial(_s5_ffn_kernel, hc=hc),
        grid=(t // row,),
        in_specs=[tile, tile, _resident((1, dm)), _resident((1, dm)), _resident((dm, dm)),
                  _resident((1, dm))] + weights,
        out_specs=tile,
        out_shape=jax.ShapeDtypeStruct((t, dm), F32),
        scratch_shapes=[pltpu.VMEM((row, dm), F32)],
        compiler_params=_cparams(("parallel",)),
        name="s5_ffn",
    )(x, y, g.reshape(1, dm), d.reshape(1, dm), w, b.reshape(1, dm), fg.reshape(1, dm), wg, wu, wd)


def _s5_tab_kernel(p_ref, q_ref, c1_ref, c2_ref, b1_ref, b2_ref, bs_ref, tt_ref, nn_ref, mm_ref):
    sub = S5_SUB
    width = sub * S5_GROUP
    lane = lax.broadcasted_iota(jnp.int32, (S5_GROUP, width), 1)
    sgn = jnp.where(lax.broadcasted_iota(jnp.int32, (1, 2 * S5_STATE), 1) < S5_STATE, 1.0, -1.0)
    for gi in range(p_ref.shape[0]):
        p, q = p_ref[gi], q_ref[gi]
        c1, c2, b1, b2 = c1_ref[gi], c2_ref[gi], b1_ref[gi], b2_ref[gi]
        cl = [c1 * p[k:k + 1] + c2 * q[k:k + 1] for k in range(sub + 1)]
        mm_ref[gi] = (jnp.concatenate(cl[1:], axis=0) * sgn).astype(mm_ref.dtype)
        r = jnp.dot(bs_ref[gi], jnp.concatenate(cl[:sub], axis=0).T,
                    precision=lax.Precision.HIGHEST, preferred_element_type=F32)
        rows = [r] + [jnp.where(lane >= S5_GROUP * s, pltpu.roll(r, S5_GROUP * s, axis=1), 0.0)
                      for s in range(1, sub)]
        tt_ref[gi] = jnp.concatenate(rows, axis=0).T.astype(tt_ref.dtype)
        nt = [b1 * p[sub - 1 - s:sub - s] + b2 * q[sub - 1 - s:sub - s] for s in range(sub)]
        nn_ref[gi] = jnp.concatenate(nt, axis=0).T.astype(nn_ref.dtype)


def _s5_tab_call(p, q, c1, c2, b1, b2, bs):
    groups = p.shape[0]
    gt = S5_TAB_GROUPS if groups % S5_TAB_GROUPS == 0 else 1
    width = S5_SUB * S5_GROUP
    st2 = 2 * S5_STATE
    blk = lambda a: pl.BlockSpec((gt,) + a.shape[1:], lambda i: (i, 0, 0))
    out = lambda r, c: pl.BlockSpec((gt, r, c), lambda i: (i, 0, 0))
    return pl.pallas_call(
        _s5_tab_kernel,
        grid=(groups // gt,),
        in_specs=[blk(a) for a in (p, q, c1, c2, b1, b2, bs)],
        out_specs=[out(width, width), out(st2, width), out(width, st2)],
        out_shape=[jax.ShapeDtypeStruct((groups, width, width), BF16),
                   jax.ShapeDtypeStruct((groups, st2, width), BF16),
                   jax.ShapeDtypeStruct((groups, width, st2), BF16)],
        compiler_params=_cparams(("parallel",)),
        name="s5_tab",
    )(p, q, c1, c2, b1, b2, bs)


def _s5_tables(lam_re, lam_im, log_step, b_re, b_im, c_re, c_im, seg_len, nsb, nseg):
    sub = S5_SUB
    step = jnp.exp(log_step)[:, None]
    dre, dim = lam_re * step, lam_im * step

    def power(k):
        kk = jnp.asarray(k, F32)[None, :, None]
        mag = jnp.exp(kk * dre[:, None, :])
        return mag * jnp.cos(kk * dim[:, None, :]), mag * jnp.sin(kk * dim[:, None, :])

    pw_re, pw_im = power(jnp.arange(sub + 1))
    den = lam_re * lam_re + lam_im * lam_im
    nr, ni = pw_re[:, 1] - 1.0, pw_im[:, 1]
    cf_re = ((nr * lam_re + ni * lam_im) / den)[:, None, :]
    cf_im = ((ni * lam_re - nr * lam_im) / den)[:, None, :]
    bt_re, bt_im = b_re.transpose(0, 2, 1), b_im.transpose(0, 2, 1)
    bb_re = cf_re * bt_re - cf_im * bt_im
    bb_im = cf_re * bt_im + cf_im * bt_re
    cat = lambda a, b: jnp.concatenate([a, b], axis=-1)
    tt, nn, mm = _s5_tab_call(cat(pw_re, pw_im), cat(pw_im, pw_re), cat(c_re, c_re),
                              cat(-c_im, c_im), cat(bb_re, bb_re), cat(-bb_im, bb_im),
                              cat(bb_re, -bb_im))
    a_re = jnp.broadcast_to(pw_re[:, sub].reshape(-1, 1), (pw_re[:, sub].size, nseg))
    a_im = jnp.broadcast_to(pw_im[:, sub].reshape(-1, 1), (pw_im[:, sub].size, nseg))
    nd = max(1, (nsb - 1).bit_length())
    p_re, p_im = power(seg_len * (2 ** jnp.arange(nd)))
    p_re = p_re.transpose(0, 2, 1).reshape(-1, nd)
    p_im = p_im.transpose(0, 2, 1).reshape(-1, nd)
    return tt, nn, mm, a_re, a_im, p_re, p_im


def _s5_layer(x, batch, seq, norm_w, lam_re, lam_im, log_step, b_re, b_im, c_re, c_im, d, glu_w,
              glu_b, ffn_g, ffn_wg, ffn_wu, ffn_wd, ffn_layer, row, hc):
    t, dm = x.shape
    nseg = V7X_LANES
    seg_len = t // nseg
    assert t % nseg == 0 and seg_len % S5_SUB == 0 and seq % seg_len == 0
    nsb = seq // seg_len
    tt, nn, mm, a_re, a_im, p_re, p_im = _s5_tables(
        lam_re, lam_im, log_step, b_re, b_im, c_re, c_im, seg_len, nsb, nseg)
    zt, er, ei = _s5_pre_call(x.reshape(nseg, seg_len, dm), norm_w, nn, a_re, a_im)
    y = _s5_core_call(zt, tt, nn, mm, a_re, a_im, er, ei, p_re, p_im, nsb, dm).reshape(t, dm)
    return _s5_ffn_call(x, y, norm_w, d.reshape(-1), glu_w.astype(BF16), glu_b, ffn_g,
                        ffn_wg, ffn_wu, ffn_wd, ffn_layer, row, hc)


def _ret_kernel(x_ref, g_ref, w_ref, cr_ref, sr_ref, cb_ref, sb_ref, gn_ref, dm_ref, qd_ref, kd_ref, cd_ref,
                y_ref, state_ref, *, heads, blk):
    @pl.when(pl.program_id(1) == 0)
    def _():
        state_ref[...] = jnp.zeros_like(state_ref)

    u = _rms(x_ref[...], g_ref[...]).astype(BF16)
    cb = cb_ref[pl.ds(pl.program_id(1), 1), :]
    sb = sb_ref[pl.ds(pl.program_id(1), 1), :]
    cos = cr_ref[...] * cb - sr_ref[...] * sb
    sin = sr_ref[...] * cb + cr_ref[...] * sb
    half = RET_QK_DIM // 2
    qk_w = heads * RET_QK_DIM
    v_w = heads * RET_V_DIM
    k_scale = RET_QK_DIM ** -0.5
    row = x_ref.shape[0]

    def rotary(tq):
        t1, t2 = tq[:, :half], tq[:, half:]
        return jnp.concatenate([t1 * cos - t2 * sin, t1 * sin + t2 * cos], axis=-1)

    for h in range(heads):
        cq = h * RET_QK_DIM
        cv = 2 * qk_w + h * RET_V_DIM
        qf = rotary(_dot(u, w_ref[:, cq:cq + RET_QK_DIM]))
        q = qf.astype(BF16)
        k = rotary(_dot(u, w_ref[:, qk_w + cq:qk_w + cq + RET_QK_DIM]) * k_scale)
        v = _dot(u, w_ref[:, cv:cv + RET_V_DIM]).astype(BF16)
        gt = _dot(u, w_ref[:, v_w + cv:v_w + cv + RET_V_DIM])
        gn = gn_ref[:, h * RET_V_DIM:(h + 1) * RET_V_DIM]
        for r0 in range(0, row, blk):
            qs, ks, vs = q[r0:r0 + blk], k[r0:r0 + blk], v[r0:r0 + blk]
            s = lax.dot_general(qs, ks.astype(BF16), (((1,), (1,)), ((), ())),
                                preferred_element_type=F32)
            st = state_ref[h]
            qd = (qf[r0:r0 + blk] * qd_ref[h]).astype(BF16)
            o = _dot((s * dm_ref[h]).astype(BF16), vs) + _dot(qd, st.astype(BF16))
            kd = (ks * kd_ref[h]).astype(BF16)
            state_ref[h] = st * cd_ref[h] + lax.dot_general(
                kd, vs, (((0,), (0,)), ((), ())), preferred_element_type=F32)
            mean = jnp.mean(o, axis=-1, keepdims=True)
            cen = o - mean
            var = jnp.mean(jnp.square(cen), axis=-1, keepdims=True)
            on = cen * lax.rsqrt(var + NORM_EPS) * gn
            y_ref[r0:r0 + blk, h * RET_V_DIM:(h + 1) * RET_V_DIM] = (
                jax.nn.silu(gt[r0:r0 + blk]) * on).astype(y_ref.dtype)


def _ret_call(x, g, w, rot, gn_w, dmask, qdec, kdec, cdec, batch, seq, heads, row, blk):
    t, dm = x.shape
    v_w = heads * RET_V_DIM
    nb = seq // row
    kern = functools.partial(_ret_kernel, heads=heads, blk=blk)
    rows = lambda width: pl.BlockSpec((row, width), lambda b, i: (b * nb + i, 0))
    tabs = [_resident(a.shape) for a in rot]
    return pl.pallas_call(
        kern,
        grid=(batch, nb),
        in_specs=[rows(dm), _resident((1, dm)), _resident(w.shape)] + tabs + [
                  _resident((1, v_w)), _resident(dmask.shape), _resident(qdec.shape),
                  _resident(kdec.shape), _resident(cdec.shape)],
        out_specs=rows(v_w),
        out_shape=jax.ShapeDtypeStruct((t, v_w), BF16),
        scratch_shapes=[pltpu.VMEM((heads, RET_QK_DIM, RET_V_DIM), F32)],
        compiler_params=_cparams(("parallel", "arbitrary")),
        name="ret_mix",
    )(x, g.reshape(1, dm), w, *rot, gn_w.reshape(1, v_w), dmask, qdec, kdec, cdec)


def _ret_decays(heads, blk):
    log_gamma = jnp.log1p(-jnp.exp2(-5.0 - jnp.arange(heads, dtype=F32)))
    pos = jnp.arange(blk, dtype=F32)
    diff = pos[:, None] - pos[None, :]
    cn = (jnp.arange(blk) // CHUNK)[:, None]
    cm = (jnp.arange(blk) // CHUNK)[None, :]
    expo = jnp.where(cn == cm, jnp.abs(diff), diff)
    dmask = jnp.where((cm <= cn)[None], jnp.exp(log_gamma[:, None, None] * expo[None]), 0.0)
    qdec = jnp.exp((pos[None, :] + 1.0) * log_gamma[:, None])[..., None]
    kdec = jnp.exp((blk - 1.0 - pos)[None, :] * log_gamma[:, None])[..., None]
    cdec = jnp.exp(blk * log_gamma)[:, None, None]
    return dmask, qdec, kdec, cdec


def _rotary_tables(seq, row):
    inv_freq = 1.0 / (ROPE_BASE ** jnp.linspace(0.0, 1.0, RET_QK_DIM // 2, dtype=F32))
    ang_r = jnp.arange(row, dtype=F32)[:, None] * inv_freq[None, :]
    ang_b = (jnp.arange(seq // row, dtype=F32) * row)[:, None] * inv_freq[None, :]
    return jnp.cos(ang_r), jnp.sin(ang_r), jnp.cos(ang_b), jnp.sin(ang_b)


def _ret_ffn_kernel(x_ref, y_ref, w_ref, fg_ref, wg_ref, wu_ref, wd_ref, fin_ref, o_ref, acc_ref,
                    *, hc):
    x1 = x_ref[...] + _dot(y_ref[...], w_ref[...])
    out = _ffn_rows(x1, fg_ref, wg_ref, wu_ref, wd_ref, acc_ref, hc)
    o_ref[...] = _rms(out, fin_ref[...])


def _ret_ffn_call(x, y, w, fg, wg, wu, wd, layer, fin, row, hc):
    t, dm = x.shape
    hidden = wg.shape[-1]
    kdim = y.shape[1]
    tile, weights = _ffn_specs(row, dm, hidden, layer, hc)
    return pl.pallas_call(
        functools.partial(_ret_ffn_kernel, hc=hc),
        grid=(t // row,),
        in_specs=[tile, pl.BlockSpec((row, kdim), lambda i: (i, 0)), _resident((kdim, dm))]
        + weights + [_resident((1, dm))],
        out_specs=tile,
        out_shape=jax.ShapeDtypeStruct((t, dm), F32),
        scratch_shapes=[pltpu.VMEM((row, dm), F32)],
        compiler_params=_cparams(("parallel",)),
        name="ret_ffn",
    )(x, y, w, fg.reshape(1, dm), wg, wu, wd, fin.reshape(1, dm))


def _ret_layer(x, batch, seq, norm_w, w_qkvg, gn_w, w_o, ffn_g, ffn_wg, ffn_wu, ffn_wd, ffn_layer,
               fin, row, row_out, blk, hc):
    dm = x.shape[1]
    heads = dm // RET_QK_DIM
    rot = _rotary_tables(seq, row)
    dmask, qdec, kdec, cdec = _ret_decays(heads, blk)
    y = _ret_call(x, norm_w, w_qkvg.astype(BF16), rot, gn_w, dmask, qdec, kdec, cdec,
                  batch, seq, heads, row, blk)
    return _ret_ffn_call(x, y, w_o.astype(BF16), ffn_g, ffn_wg, ffn_wu, ffn_wd, ffn_layer, fin,
                         row_out, hc)


def kernel(x, s5_norm, s5_lambda_re, s5_lambda_im, s5_log_step, s5_b_re, s5_b_im, s5_c_re, s5_c_im,
           s5_d, s5_glu_w, s5_glu_b, ret_norm, ret_w_qkvg, ret_gn_w, ret_w_o, ffn_norm, ffn_w_gate,
           ffn_w_up, ffn_w_down, final_norm):
    batch, seq, dm = x.shape
    row, row_out, blk, hc = _tiles(seq)
    assert ffn_norm.shape[0] == 2 and s5_norm.shape[0] == 1 and ret_norm.shape[0] == 1
    h = x.reshape(batch * seq, dm)
    wg, wu, wd = ffn_w_gate.astype(BF16), ffn_w_up.astype(BF16), ffn_w_down.astype(BF16)
    h = _s5_layer(h, batch, seq, s5_norm[0], s5_lambda_re[0], s5_lambda_im[0], s5_log_step[0],
                  s5_b_re[0], s5_b_im[0], s5_c_re[0], s5_c_im[0], s5_d[0], s5_glu_w[0],
                  s5_glu_b[0], ffn_norm[0], wg, wu, wd, 0, row, hc)
    h = _ret_layer(h, batch, seq, ret_norm[0], ret_w_qkvg[0], ret_gn_w[0], ret_w_o[0],
                   ffn_norm[1], wg, wu, wd, 1, final_norm,
                   row, row_out, blk, hc)
    return h.reshape(batch, seq, dm)
```

```python
import functools

import jax
import jax.numpy as jnp
from jax import lax
from jax.experimental import pallas as pl
from jax.experimental.pallas import tpu as pltpu

F32 = jnp.float32
BF16 = jnp.bfloat16

NORM_EPS = 1e-6
CHUNK = 64
S5_GROUP = 16
S5_STATE = 64
S5_SUB = 16
S5_GB = 16
S5_TAB_GROUPS = 16
RET_QK_DIM = 256
RET_V_DIM = 512
ROPE_BASE = 10000.0

V7X_VMEM_BYTES = 64 * 1024 * 1024
V7X_LANES = 128
VMEM_LIMIT = V7X_VMEM_BYTES * 7 // 8


def _tiles(seq):
    row = min(512, seq)
    row_out = min(1024, seq)
    ret = min(256, seq)
    hc = 256
    assert seq % row == 0 and seq % row_out == 0 and row % ret == 0 and ret % CHUNK == 0
    return row, row_out, ret, hc


def _cparams(sem):
    return pltpu.CompilerParams(dimension_semantics=sem, vmem_limit_bytes=VMEM_LIMIT)


def _resident(shape):
    nd = len(shape)
    return pl.BlockSpec(shape, lambda *_: (0,) * nd, pipeline_mode=pl.Buffered(1))


def _rms(xf, g):
    ms = jnp.mean(jnp.square(xf), axis=-1, keepdims=True)
    return xf * lax.rsqrt(ms + NORM_EPS) * g


def _dot(a, b):
    return jnp.dot(a, b, preferred_element_type=F32)


def _bdot(a, b):
    return lax.dot_general(a, b, (((2,), (1,)), ((0,), (0,))), preferred_element_type=F32)


def _s5_pre_kernel(x_hbm, g_ref, nn_ref, are_ref, aim_ref, zt_ref, er_ref, ei_ref, xs, sem):
    groups = zt_ref.shape[1]
    nseg = zt_ref.shape[3]
    i = pl.program_id(0)
    steps = pl.num_programs(0)

    def gathers(step, slot):
        return [pltpu.make_async_copy(x_hbm.at[:, step * S5_SUB + t, :], xs.at[slot, t],
                                      sem.at[slot]) for t in range(S5_SUB)]

    @pl.when(i == 0)
    def _():
        for c in gathers(0, 0):
            c.start()

    @pl.when(i + 1 < steps)
    def _():
        for c in gathers(i + 1, (i + 1) % 2):
            c.start()

    slot = i % 2
    for c in gathers(i, slot):
        c.wait()
    g = g_ref[...]
    for t in range(S5_SUB):
        ut = _rms(xs[slot, t], g)
        zt_ref[0, :, t * S5_GROUP:(t + 1) * S5_GROUP, :] = (
            ut.T.reshape(groups, S5_GROUP, nseg).astype(zt_ref.dtype))
    s = _bdot(nn_ref[...], zt_ref[0])
    sr = s[:, :S5_STATE, :].reshape(groups * S5_STATE, nseg)
    si = s[:, S5_STATE:, :].reshape(groups * S5_STATE, nseg)

    @pl.when(i == 0)
    def _():
        er_ref[...] = sr
        ei_ref[...] = si

    @pl.when(i > 0)
    def _():
        er, ei = er_ref[...], ei_ref[...]
        ar, ai = are_ref[...], aim_ref[...]
        er_ref[...] = ar * er - ai * ei + sr
        ei_ref[...] = ar * ei + ai * er + si


def _s5_pre_call(x3, g, nn, a_re, a_im):
    nseg, sl, dm = x3.shape
    groups = dm // S5_GROUP
    steps = sl // S5_SUB
    width = S5_SUB * S5_GROUP
    rows = groups * S5_STATE
    return pl.pallas_call(
        _s5_pre_kernel,
        grid=(steps,),
        in_specs=[pl.BlockSpec(memory_space=pl.ANY), _resident((1, dm)),
                  _resident(nn.shape), _resident((rows, nseg)), _resident((rows, nseg))],
        out_specs=[pl.BlockSpec((1, groups, width, nseg), lambda i: (i, 0, 0, 0)),
                   pl.BlockSpec((rows, nseg), lambda i: (0, 0)),
                   pl.BlockSpec((rows, nseg), lambda i: (0, 0))],
        out_shape=[jax.ShapeDtypeStruct((steps, groups, width, nseg), BF16),
                   jax.ShapeDtypeStruct((rows, nseg), F32),
                   jax.ShapeDtypeStruct((rows, nseg), F32)],
        scratch_shapes=[pltpu.VMEM((2, S5_SUB, nseg, dm), F32), pltpu.SemaphoreType.DMA((2,))],
        compiler_params=_cparams(("arbitrary",)),
        name="s5_pre",
    )(x3, g.reshape(1, dm), nn, a_re, a_im)


def _s5_stitch(er_ref, ei_ref, are_ref, aim_ref, xr_ref, xi_ref, nsb, steps):
    er, ei = er_ref[...], ei_ref[...]
    pr, pi = are_ref[...], aim_ref[...]
    for _ in range(steps.bit_length() - 1):
        pr, pi = pr * pr - pi * pi, 2.0 * pr * pi
    lane = lax.broadcasted_iota(jnp.int32, er.shape, 1) % nsb
    d = 1
    while d < nsb:
        sr, si = pltpu.roll(er, d, axis=1), pltpu.roll(ei, d, axis=1)
        ok = lane >= d
        er, ei = (er + jnp.where(ok, pr * sr - pi * si, 0.0),
                  ei + jnp.where(ok, pr * si + pi * sr, 0.0))
        pr, pi = pr * pr - pi * pi, 2.0 * pr * pi
        d = 2 * d
    ok = lane >= 1
    xr_ref[...] = jnp.where(ok, pltpu.roll(er, 1, axis=1), 0.0)
    xi_ref[...] = jnp.where(ok, pltpu.roll(ei, 1, axis=1), 0.0)


def _s5_core_kernel(zt_ref, tt_ref, nn_ref, mm_ref, are_ref, aim_ref, er_ref, ei_ref, y_hbm,
                    xr_scr, xi_scr, ybuf, sem, *, nsb, steps):
    pair, gb, _, nseg = zt_ref.shape
    cols = gb * S5_GROUP
    j, k = pl.program_id(0), pl.program_id(1)
    n = j * pl.num_programs(1) + k
    last = pl.num_programs(0) * pl.num_programs(1) - 1
    slot = n % 2

    def scatters(sl):
        return [pltpu.make_async_copy(
            ybuf.at[sl, s, t],
            y_hbm.at[:, (k * pair + s) * S5_SUB + t, pl.ds(j * cols, cols)],
            sem.at[sl]) for s in range(pair) for t in range(S5_SUB)]

    @pl.when(k == 0)
    def _():
        _s5_stitch(er_ref, ei_ref, are_ref, aim_ref, xr_scr, xi_scr, nsb, steps)

    @pl.when(n >= 2)
    def _():
        for c in scatters(slot):
            c.wait()

    zt = jnp.concatenate([zt_ref[s] for s in range(pair)], axis=-1)
    s_all = _bdot(nn_ref[...], zt)
    ar, ai = are_ref[...], aim_ref[...]
    xr, xi = xr_scr[...], xi_scr[...]
    starts = []
    for s in range(pair):
        starts.append(jnp.concatenate([xr.reshape(gb, S5_STATE, nseg),
                                       xi.reshape(gb, S5_STATE, nseg)], axis=1))
        inc = s_all[:, :, s * nseg:(s + 1) * nseg]
        sr = inc[:, :S5_STATE, :].reshape(gb * S5_STATE, nseg)
        si = inc[:, S5_STATE:, :].reshape(gb * S5_STATE, nseg)
        xr, xi = ar * xr - ai * xi + sr, ar * xi + ai * xr + si
    xr_scr[...] = xr
    xi_scr[...] = xi
    xprev = jnp.concatenate(starts, axis=-1).astype(zt.dtype)
    y = _bdot(tt_ref[...], zt) + _bdot(mm_ref[...], xprev)
    for t in range(S5_SUB):
        blk = y[:, t * S5_GROUP:(t + 1) * S5_GROUP, :].reshape(cols, pair * nseg)
        bt = blk.T
        for s in range(pair):
            ybuf[slot, s, t] = bt[s * nseg:(s + 1) * nseg]
    for c in scatters(slot):
        c.start()

    @pl.when(n == last)
    def _():
        for c in scatters(slot):
            c.wait()

    @pl.when(jnp.logical_and(n == last, n >= 1))
    def _():
        for c in scatters(1 - slot):
            c.wait()


def _s5_core_call(zt, tt, nn, mm, a_re, a_im, er, ei, nsb, dm):
    steps, groups, width, nseg = zt.shape
    assert steps & (steps - 1) == 0
    gb = min(S5_GB, groups)
    pair = max(p for p in (1, 2, 4) if steps % p == 0)
    rows = gb * S5_STATE
    per_gb = lambda shape: pl.BlockSpec(shape, lambda j, k: (j,) + (0,) * (len(shape) - 1))
    return pl.pallas_call(
        functools.partial(_s5_core_kernel, nsb=nsb, steps=steps),
        grid=(groups // gb, steps // pair),
        in_specs=[pl.BlockSpec((pair, gb, width, nseg), lambda j, k: (k, j, 0, 0)),
                  per_gb((gb, width, width)), per_gb((gb, 2 * S5_STATE, width)),
                  per_gb((gb, width, 2 * S5_STATE)),
                  per_gb((rows, nseg)), per_gb((rows, nseg)), per_gb((rows, nseg)),
                  per_gb((rows, nseg))],
        out_specs=pl.BlockSpec(memory_space=pl.ANY),
        out_shape=jax.ShapeDtypeStruct((nseg, steps * S5_SUB, dm), F32),
        scratch_shapes=[pltpu.VMEM((rows, nseg), F32), pltpu.VMEM((rows, nseg), F32),
                        pltpu.VMEM((2, pair, S5_SUB, nseg, gb * S5_GROUP), F32),
                        pltpu.SemaphoreType.DMA((2,))],
        compiler_params=_cparams(("arbitrary", "arbitrary")),
        name="s5_core",
    )(zt, tt, nn, mm, a_re, a_im, er, ei)


def _ffn_rows(x, g_ref, wg_ref, wu_ref, wd_ref, acc_ref, hc):
    u = _rms(x, g_ref[...]).astype(BF16)
    hidden = wg_ref.shape[1]
    for j in range(hidden // hc):
        cols = slice(j * hc, (j + 1) * hc)
        gt = _dot(u, wg_ref[:, cols])
        up = _dot(u, wu_ref[:, cols])
        h = (jax.nn.silu(gt) * up).astype(BF16)
        part = _dot(h, wd_ref[cols, :])
        if j == 0:
            acc_ref[...] = part
        else:
            acc_ref[...] += part
    return x + acc_ref[...]


def _layer_resident(shape, layer):
    nd = len(shape)
    return pl.BlockSpec((None,) + shape, lambda *_: (layer,) + (0,) * nd,
                        pipeline_mode=pl.Buffered(1))


def _ffn_specs(row, dm, hidden, layer, hc):
    assert hidden % hc == 0
    tile = pl.BlockSpec((row, dm), lambda i: (i, 0))
    weights = [_resident((1, dm)), _layer_resident((dm, hidden), layer),
               _layer_resident((dm, hidden), layer), _layer_resident((hidden, dm), layer)]
    return tile, weights


def _s5_ffn_kernel(x_ref, y_ref, g_ref, d_ref, w_ref, b_ref, fg_ref, wg_ref, wu_ref, wd_ref,
                   o_ref, acc_ref, *, hc):
    x = x_ref[...]
    u = _rms(x, g_ref[...])
    y = jax.nn.gelu(y_ref[...] + d_ref[...] * u)
    gate = jax.nn.sigmoid(_dot(y.astype(BF16), w_ref[...]) + b_ref[...])
    o_ref[...] = _ffn_rows(x + y * gate, fg_ref, wg_ref, wu_ref, wd_ref, acc_ref, hc)


def _s5_ffn_call(x, y, g, d, w, b, fg, wg, wu, wd, layer, row, hc):
    t, dm = x.shape
    hidden = wg.shape[-1]
    tile, weights = _ffn_specs(row, dm, hidden, layer, hc)
    return pl.pallas_call(
        functools.partial(_s5_ffn_kernel, hc=hc),
        grid=(t // row,),
        in_specs=[tile, tile, _resident((1, dm)), _resident((1, dm)), _resident((dm, dm)),
                  _resident((1, dm))] + weights,
        out_specs=tile,
        out_shape=jax.ShapeDtypeStruct((t, dm), F32),
        scratch_shapes=[pltpu.VMEM((row, dm), F32)],
        compiler_params=_cparams(("parallel",)),
        name="s5_ffn",
    )(x, y, g.reshape(1, dm), d.reshape(1, dm), w, b.reshape(1, dm), fg.reshape(1, dm), wg, wu, wd)


def _s5_tab_kernel(p_ref, q_ref, c1_ref, c2_ref, b1_ref, b2_ref, bs_ref, tt_ref, nn_ref, mm_ref):
    sub = S5_SUB
    width = sub * S5_GROUP
    lane = lax.broadcasted_iota(jnp.int32, (S5_GROUP, width), 1)
    sgn = jnp.where(lax.broadcasted_iota(jnp.int32, (1, 2 * S5_STATE), 1) < S5_STATE, 1.0, -1.0)
    for gi in range(p_ref.shape[0]):
        p, q = p_ref[gi], q_ref[gi]
        c1, c2, b1, b2 = c1_ref[gi], c2_ref[gi], b1_ref[gi], b2_ref[gi]
        cl = [c1 * p[k:k + 1] + c2 * q[k:k + 1] for k in range(sub + 1)]
        mm_ref[gi] = (jnp.concatenate(cl[1:], axis=0) * sgn).astype(mm_ref.dtype)
        r = jnp.dot(bs_ref[gi], jnp.concatenate(cl[:sub], axis=0).T,
                    precision=lax.Precision.HIGHEST, preferred_element_type=F32)
        rows = [r] + [jnp.where(lane >= S5_GROUP * s, pltpu.roll(r, S5_GROUP * s, axis=1), 0.0)
                      for s in range(1, sub)]
        tt_ref[gi] = jnp.concatenate(rows, axis=0).T.astype(tt_ref.dtype)
        nt = [b1 * p[sub - 1 - s:sub - s] + b2 * q[sub - 1 - s:sub - s] for s in range(sub)]
        nn_ref[gi] = jnp.concatenate(nt, axis=0).T.astype(nn_ref.dtype)


def _s5_tab_call(p, q, c1, c2, b1, b2, bs):
    groups = p.shape[0]
    gt = S5_TAB_GROUPS if groups % S5_TAB_GROUPS == 0 else 1
    width = S5_SUB * S5_GROUP
    st2 = 2 * S5_STATE
    blk = lambda a: pl.BlockSpec((gt,) + a.shape[1:], lambda i: (i, 0, 0))
    out = lambda r, c: pl.BlockSpec((gt, r, c), lambda i: (i, 0, 0))
    return pl.pallas_call(
        _s5_tab_kernel,
        grid=(groups // gt,),
        in_specs=[blk(a) for a in (p, q, c1, c2, b1, b2, bs)],
        out_specs=[out(width, width), out(st2, width), out(width, st2)],
        out_shape=[jax.ShapeDtypeStruct((groups, width, width), BF16),
                   jax.ShapeDtypeStruct((groups, st2, width), BF16),
                   jax.ShapeDtypeStruct((groups, width, st2), BF16)],
        compiler_params=_cparams(("parallel",)),
        name="s5_tab",
    )(p, q, c1, c2, b1, b2, bs)


def _s5_tables(lam_re, lam_im, log_step, b_re, b_im, c_re, c_im, nseg):
    sub = S5_SUB
    step = jnp.exp(log_step)[:, None]
    dre, dim = lam_re * step, lam_im * step

    def power(k):
        kk = jnp.asarray(k, F32)[None, :, None]
        mag = jnp.exp(kk * dre[:, None, :])
        return mag * jnp.cos(kk * dim[:, None, :]), mag * jnp.sin(kk * dim[:, None, :])

    pw_re, pw_im = power(jnp.arange(sub + 1))
    den = lam_re * lam_re + lam_im * lam_im
    nr, ni = pw_re[:, 1] - 1.0, pw_im[:, 1]
    cf_re = ((nr * lam_re + ni * lam_im) / den)[:, None, :]
    cf_im = ((ni * lam_re - nr * lam_im) / den)[:, None, :]
    bt_re, bt_im = b_re.transpose(0, 2, 1), b_im.transpose(0, 2, 1)
    bb_re = cf_re * bt_re - cf_im * bt_im
    bb_im = cf_re * bt_im + cf_im * bt_re
    cat = lambda a, b: jnp.concatenate([a, b], axis=-1)
    tt, nn, mm = _s5_tab_call(cat(pw_re, pw_im), cat(pw_im, pw_re), cat(c_re, c_re),
                              cat(-c_im, c_im), cat(bb_re, bb_re), cat(-bb_im, bb_im),
                              cat(bb_re, -bb_im))
    a_re = jnp.broadcast_to(pw_re[:, sub].reshape(-1, 1), (pw_re[:, sub].size, nseg))
    a_im = jnp.broadcast_to(pw_im[:, sub].reshape(-1, 1), (pw_im[:, sub].size, nseg))
    return tt, nn, mm, a_re, a_im


def _s5_layer(x, batch, seq, norm_w, lam_re, lam_im, log_step, b_re, b_im, c_re, c_im, d, glu_w,
              glu_b, ffn_g, ffn_wg, ffn_wu, ffn_wd, ffn_layer, row, hc):
    t, dm = x.shape
    nseg = V7X_LANES
    seg_len = t // nseg
    assert t % nseg == 0 and seg_len % S5_SUB == 0 and seq % seg_len == 0
    nsb = seq // seg_len
    tt, nn, mm, a_re, a_im = _s5_tables(lam_re, lam_im, log_step, b_re, b_im, c_re, c_im, nseg)
    zt, er, ei = _s5_pre_call(x.reshape(nseg, seg_len, dm), norm_w, nn, a_re, a_im)
    y = _s5_core_call(zt, tt, nn, mm, a_re, a_im, er, ei, nsb, dm).reshape(t, dm)
    return _s5_ffn_call(x, y, norm_w, d.reshape(-1), glu_w.astype(BF16), glu_b, ffn_g,
                        ffn_wg, ffn_wu, ffn_wd, ffn_layer, row, hc)


def _ret_kernel(x_ref, g_ref, w_ref, cr_ref, sr_ref, cb_ref, sb_ref, gn_ref, dm_ref, qd_ref, kd_ref, cd_ref,
                y_ref, state_ref, *, heads, blk):
    @pl.when(pl.program_id(1) == 0)
    def _():
        state_ref[...] = jnp.zeros_like(state_ref)

    u = _rms(x_ref[...], g_ref[...]).astype(BF16)
    cb = cb_ref[pl.ds(pl.program_id(1), 1), :]
    sb = sb_ref[pl.ds(pl.program_id(1), 1), :]
    cos = cr_ref[...] * cb - sr_ref[...] * sb
    sin = sr_ref[...] * cb + cr_ref[...] * sb
    half = RET_QK_DIM // 2
    qk_w = heads * RET_QK_DIM
    v_w = heads * RET_V_DIM
    k_scale = RET_QK_DIM ** -0.5
    row = x_ref.shape[0]

    def rotary(tq):
        t1, t2 = tq[:, :half], tq[:, half:]
        return jnp.concatenate([t1 * cos - t2 * sin, t1 * sin + t2 * cos], axis=-1)

    for h in range(heads):
        cq = h * RET_QK_DIM
        cv = 2 * qk_w + h * RET_V_DIM
        qf = rotary(_dot(u, w_ref[:, cq:cq + RET_QK_DIM]))
        q = qf.astype(BF16)
        k = rotary(_dot(u, w_ref[:, qk_w + cq:qk_w + cq + RET_QK_DIM]) * k_scale)
        v = _dot(u, w_ref[:, cv:cv + RET_V_DIM]).astype(BF16)
        gt = _dot(u, w_ref[:, v_w + cv:v_w + cv + RET_V_DIM])
        gn = gn_ref[:, h * RET_V_DIM:(h + 1) * RET_V_DIM]
        for r0 in range(0, row, blk):
            qs, ks, vs = q[r0:r0 + blk], k[r0:r0 + blk], v[r0:r0 + blk]
            s = lax.dot_general(qs, ks.astype(BF16), (((1,), (1,)), ((), ())),
                                preferred_element_type=F32)
            st = state_ref[h]
            qd = (qf[r0:r0 + blk] * qd_ref[h]).astype(BF16)
            o = _dot((s * dm_ref[h]).astype(BF16), vs) + _dot(qd, st.astype(BF16))
            kd = (ks * kd_ref[h]).astype(BF16)
            state_ref[h] = st * cd_ref[h] + lax.dot_general(
                kd, vs, (((0,), (0,)), ((), ())), preferred_element_type=F32)
            mean = jnp.mean(o, axis=-1, keepdims=True)
            cen = o - mean
            var = jnp.mean(jnp.square(cen), axis=-1, keepdims=True)
            on = cen * lax.rsqrt(var + NORM_EPS) * gn
            y_ref[r0:r0 + blk, h * RET_V_DIM:(h + 1) * RET_V_DIM] = (
                jax.nn.silu(gt[r0:r0 + blk]) * on).astype(y_ref.dtype)


def _ret_call(x, g, w, rot, gn_w, dmask, qdec, kdec, cdec, batch, seq, heads, row, blk):
    t, dm = x.shape
    v_w = heads * RET_V_DIM
    nb = seq // row
    kern = functools.partial(_ret_kernel, heads=heads, blk=blk)
    rows = lambda width: pl.BlockSpec((row, width), lambda b, i: (b * nb + i, 0))
    tabs = [_resident(a.shape) for a in rot]
    return pl.pallas_call(
        kern,
        grid=(batch, nb),
        in_specs=[rows(dm), _resident((1, dm)), _resident(w.shape)] + tabs + [
                  _resident((1, v_w)), _resident(dmask.shape), _resident(qdec.shape),
                  _resident(kdec.shape), _resident(cdec.shape)],
        out_specs=rows(v_w),
        out_shape=jax.ShapeDtypeStruct((t, v_w), BF16),
        scratch_shapes=[pltpu.VMEM((heads, RET_QK_DIM, RET_V_DIM), F32)],
        compiler_params=_cparams(("parallel", "arbitrary")),
        name="ret_mix",
    )(x, g.reshape(1, dm), w, *rot, gn_w.reshape(1, v_w), dmask, qdec, kdec, cdec)


def _ret_decays(heads, blk):
    log_gamma = jnp.log1p(-jnp.exp2(-5.0 - jnp.arange(heads, dtype=F32)))
    pos = jnp.arange(blk, dtype=F32)
    diff = pos[:, None] - pos[None, :]
    cn = (jnp.arange(blk) // CHUNK)[:, None]
    cm = (jnp.arange(blk) // CHUNK)[None, :]
    expo = jnp.where(cn == cm, jnp.abs(diff), diff)
    dmask = jnp.where((cm <= cn)[None], jnp.exp(log_gamma[:, None, None] * expo[None]), 0.0)
    qdec = jnp.exp((pos[None, :] + 1.0) * log_gamma[:, None])[..., None]
    kdec = jnp.exp((blk - 1.0 - pos)[None, :] * log_gamma[:, None])[..., None]
    cdec = jnp.exp(blk * log_gamma)[:, None, None]
    return dmask, qdec, kdec, cdec


def _rotary_tables(seq, row):
    inv_freq = 1.0 / (ROPE_BASE ** jnp.linspace(0.0, 1.0, RET_QK_DIM // 2, dtype=F32))
    ang_r = jnp.arange(row, dtype=F32)[:, None] * inv_freq[None, :]
    ang_b = (jnp.arange(seq // row, dtype=F32) * row)[:, None] * inv_freq[None, :]
    return jnp.cos(ang_r), jnp.sin(ang_r), jnp.cos(ang_b), jnp.sin(ang_b)


def _ret_ffn_kernel(x_ref, y_ref, w_ref, fg_ref, wg_ref, wu_ref, wd_ref, fin_ref, o_ref, acc_ref,
                    *, hc):
    x1 = x_ref[...] + _dot(y_ref[...], w_ref[...])
    out = _ffn_rows(x1, fg_ref, wg_ref, wu_ref, wd_ref, acc_ref, hc)
    o_ref[...] = _rms(out, fin_ref[...])


def _ret_ffn_call(x, y, w, fg, wg, wu, wd, layer, fin, row, hc):
    t, dm = x.shape
    hidden = wg.shape[-1]
    kdim = y.shape[1]
    tile, weights = _ffn_specs(row, dm, hidden, layer, hc)
    return pl.pallas_call(
        functools.partial(_ret_ffn_kernel, hc=hc),
        grid=(t // row,),
        in_specs=[tile, pl.BlockSpec((row, kdim), lambda i: (i, 0)), _resident((kdim, dm))]
        + weights + [_resident((1, dm))],
        out_specs=tile,
        out_shape=jax.ShapeDtypeStruct((t, dm), F32),
        scratch_shapes=[pltpu.VMEM((row, dm), F32)],
        compiler_params=_cparams(("parallel",)),
        name="ret_ffn",
    )(x, y, w, fg.reshape(1, dm), wg, wu, wd, fin.reshape(1, dm))


def _ret_layer(x, batch, seq, norm_w, w_qkvg, gn_w, w_o, ffn_g, ffn_wg, ffn_wu, ffn_wd, ffn_layer,
               fin, row, row_out, blk, hc):
    dm = x.shape[1]
    heads = dm // RET_QK_DIM
    rot = _rotary_tables(seq, row)
    dmask, qdec, kdec, cdec = _ret_decays(heads, blk)
    y = _ret_call(x, norm_w, w_qkvg.astype(BF16), rot, gn_w, dmask, qdec, kdec, cdec,
                  batch, seq, heads, row, blk)
    return _ret_ffn_call(x, y, w_o.astype(BF16), ffn_g, ffn_wg, ffn_wu, ffn_wd, ffn_layer, fin,
                         row_out, hc)


def kernel(x, s5_norm, s5_lambda_re, s5_lambda_im, s5_log_step, s5_b_re, s5_b_im, s5_c_re, s5_c_im,
           s5_d, s5_glu_w, s5_glu_b, ret_norm, ret_w_qkvg, ret_gn_w, ret_w_o, ffn_norm, ffn_w_gate,
           ffn_w_up, ffn_w_down, final_norm):
    batch, seq, dm = x.shape
    row, row_out, blk, hc = _tiles(seq)
    assert ffn_norm.shape[0] == 2 and s5_norm.shape[0] == 1 and ret_norm.shape[0] == 1
    h = x.reshape(batch * seq, dm)
    wg, wu, wd = ffn_w_gate.astype(BF16), ffn_w_up.astype(BF16), ffn_w_down.astype(BF16)
    h = _s5_layer(h, batch, seq, s5_norm[0], s5_lambda_re[0], s5_lambda_im[0], s5_log_step[0],
                  s5_b_re[0], s5_b_im[0], s5_c_re[0], s5_c_im[0], s5_d[0], s5_glu_w[0],
                  s5_glu_b[0], ffn_norm[0], wg, wu, wd, 0, row, hc)
    h = _ret_layer(h, batch, seq, ret_norm[0], ret_w_qkvg[0], ret_gn_w[0], ret_w_o[0],
                   ffn_norm[1], wg, wu, wd, 1, final_norm,
                   row, row_out, blk, hc)
    return h.reshape(batch, seq, dm)
```

```python
import functools

import jax
import jax.numpy as jnp
from jax import lax
from jax.experimental import pallas as pl
from jax.experimental.pallas import tpu as pltpu

F32 = jnp.float32
BF16 = jnp.bfloat16

NORM_EPS = 1e-6
CHUNK = 64
S5_GROUP = 16
S5_STATE = 64
S5_SUB = 16
S5_GB = 16
S5_TAB_GROUPS = 16
RET_QK_DIM = 256
RET_V_DIM = 512
ROPE_BASE = 10000.0

V7X_VMEM_BYTES = 64 * 1024 * 1024
V7X_LANES = 128
VMEM_LIMIT = V7X_VMEM_BYTES * 7 // 8


def _tiles(seq):
    row = min(512, seq)
    row_out = min(512, seq)
    ret = min(256, seq)
    hc = 256
    assert seq % row == 0 and seq % row_out == 0 and row % ret == 0 and ret % CHUNK == 0
    return row, row_out, ret, hc


def _cparams(sem):
    return pltpu.CompilerParams(dimension_semantics=sem, vmem_limit_bytes=VMEM_LIMIT)


def _resident(shape):
    nd = len(shape)
    return pl.BlockSpec(shape, lambda *_: (0,) * nd, pipeline_mode=pl.Buffered(1))


def _rms(xf, g):
    ms = jnp.mean(jnp.square(xf), axis=-1, keepdims=True)
    return xf * lax.rsqrt(ms + NORM_EPS) * g


def _dot(a, b):
    return jnp.dot(a, b, preferred_element_type=F32)


def _bdot(a, b):
    return lax.dot_general(a, b, (((2,), (1,)), ((0,), (0,))), preferred_element_type=F32)


def _s5_pre_kernel(x_hbm, g_ref, nn_ref, are_ref, aim_ref, zt_ref, er_ref, ei_ref, xs, sem):
    groups = zt_ref.shape[1]
    nseg = zt_ref.shape[3]
    i = pl.program_id(0)
    steps = pl.num_programs(0)

    def gathers(step, slot):
        return [pltpu.make_async_copy(x_hbm.at[:, step * S5_SUB + t, :], xs.at[slot, t],
                                      sem.at[slot]) for t in range(S5_SUB)]

    @pl.when(i == 0)
    def _():
        for c in gathers(0, 0):
            c.start()

    @pl.when(i + 1 < steps)
    def _():
        for c in gathers(i + 1, (i + 1) % 2):
            c.start()

    slot = i % 2
    for c in gathers(i, slot):
        c.wait()
    g = g_ref[...]
    for t in range(S5_SUB):
        ut = _rms(xs[slot, t], g)
        zt_ref[0, :, t * S5_GROUP:(t + 1) * S5_GROUP, :] = (
            ut.T.reshape(groups, S5_GROUP, nseg).astype(zt_ref.dtype))
    s = _bdot(nn_ref[...], zt_ref[0])
    sr = s[:, :S5_STATE, :].reshape(groups * S5_STATE, nseg)
    si = s[:, S5_STATE:, :].reshape(groups * S5_STATE, nseg)

    @pl.when(i == 0)
    def _():
        er_ref[...] = sr
        ei_ref[...] = si

    @pl.when(i > 0)
    def _():
        er, ei = er_ref[...], ei_ref[...]
        ar, ai = are_ref[...], aim_ref[...]
        er_ref[...] = ar * er - ai * ei + sr
        ei_ref[...] = ar * ei + ai * er + si


def _s5_pre_call(x3, g, nn, a_re, a_im):
    nseg, sl, dm = x3.shape
    groups = dm // S5_GROUP
    steps = sl // S5_SUB
    width = S5_SUB * S5_GROUP
    rows = groups * S5_STATE
    return pl.pallas_call(
        _s5_pre_kernel,
        grid=(steps,),
        in_specs=[pl.BlockSpec(memory_space=pl.ANY), _resident((1, dm)),
                  _resident(nn.shape), _resident((rows, nseg)), _resident((rows, nseg))],
        out_specs=[pl.BlockSpec((1, groups, width, nseg), lambda i: (i, 0, 0, 0)),
                   pl.BlockSpec((rows, nseg), lambda i: (0, 0)),
                   pl.BlockSpec((rows, nseg), lambda i: (0, 0))],
        out_shape=[jax.ShapeDtypeStruct((steps, groups, width, nseg), BF16),
                   jax.ShapeDtypeStruct((rows, nseg), F32),
                   jax.ShapeDtypeStruct((rows, nseg), F32)],
        scratch_shapes=[pltpu.VMEM((2, S5_SUB, nseg, dm), F32), pltpu.SemaphoreType.DMA((2,))],
        compiler_params=_cparams(("arbitrary",)),
        name="s5_pre",
    )(x3, g.reshape(1, dm), nn, a_re, a_im)


def _s5_stitch(er_ref, ei_ref, are_ref, aim_ref, xr_ref, xi_ref, nsb, steps):
    er, ei = er_ref[...], ei_ref[...]
    pr, pi = are_ref[...], aim_ref[...]
    for _ in range(steps.bit_length() - 1):
        pr, pi = pr * pr - pi * pi, 2.0 * pr * pi
    lane = lax.broadcasted_iota(jnp.int32, er.shape, 1) % nsb
    d = 1
    while d < nsb:
        sr, si = pltpu.roll(er, d, axis=1), pltpu.roll(ei, d, axis=1)
        ok = lane >= d
        er, ei = (er + jnp.where(ok, pr * sr - pi * si, 0.0),
                  ei + jnp.where(ok, pr * si + pi * sr, 0.0))
        pr, pi = pr * pr - pi * pi, 2.0 * pr * pi
        d = 2 * d
    ok = lane >= 1
    xr_ref[...] = jnp.where(ok, pltpu.roll(er, 1, axis=1), 0.0)
    xi_ref[...] = jnp.where(ok, pltpu.roll(ei, 1, axis=1), 0.0)


def _s5_core_kernel(zt_ref, tt_ref, nn_ref, mm_ref, are_ref, aim_ref, er_ref, ei_ref, y_hbm,
                    xr_scr, xi_scr, ybuf, sem, *, nsb, steps):
    pair, gb, _, nseg = zt_ref.shape
    cols = gb * S5_GROUP
    j, k = pl.program_id(0), pl.program_id(1)
    n = j * pl.num_programs(1) + k
    last = pl.num_programs(0) * pl.num_programs(1) - 1
    slot = n % 2

    def scatters(sl):
        return [pltpu.make_async_copy(
            ybuf.at[sl, s, t],
            y_hbm.at[:, (k * pair + s) * S5_SUB + t, pl.ds(j * cols, cols)],
            sem.at[sl]) for s in range(pair) for t in range(S5_SUB)]

    @pl.when(k == 0)
    def _():
        _s5_stitch(er_ref, ei_ref, are_ref, aim_ref, xr_scr, xi_scr, nsb, steps)

    @pl.when(n >= 2)
    def _():
        for c in scatters(slot):
            c.wait()

    zt = jnp.concatenate([zt_ref[s] for s in range(pair)], axis=-1)
    s_all = _bdot(nn_ref[...], zt)
    ar, ai = are_ref[...], aim_ref[...]
    xr, xi = xr_scr[...], xi_scr[...]
    starts = []
    for s in range(pair):
        starts.append(jnp.concatenate([xr.reshape(gb, S5_STATE, nseg),
                                       xi.reshape(gb, S5_STATE, nseg)], axis=1))
        inc = s_all[:, :, s * nseg:(s + 1) * nseg]
        sr = inc[:, :S5_STATE, :].reshape(gb * S5_STATE, nseg)
        si = inc[:, S5_STATE:, :].reshape(gb * S5_STATE, nseg)
        xr, xi = ar * xr - ai * xi + sr, ar * xi + ai * xr + si
    xr_scr[...] = xr
    xi_scr[...] = xi
    xprev = jnp.concatenate(starts, axis=-1).astype(zt.dtype)
    y = _bdot(tt_ref[...], zt) + _bdot(mm_ref[...], xprev)
    for t in range(S5_SUB):
        blk = y[:, t * S5_GROUP:(t + 1) * S5_GROUP, :].reshape(cols, pair * nseg)
        bt = blk.T
        for s in range(pair):
            ybuf[slot, s, t] = bt[s * nseg:(s + 1) * nseg]
    for c in scatters(slot):
        c.start()

    @pl.when(n == last)
    def _():
        for c in scatters(slot):
            c.wait()

    @pl.when(jnp.logical_and(n == last, n >= 1))
    def _():
        for c in scatters(1 - slot):
            c.wait()


def _s5_core_call(zt, tt, nn, mm, a_re, a_im, er, ei, nsb, dm):
    steps, groups, width, nseg = zt.shape
    assert steps & (steps - 1) == 0
    gb = min(S5_GB, groups)
    pair = max(p for p in (1, 2, 4) if steps % p == 0)
    rows = gb * S5_STATE
    per_gb = lambda shape: pl.BlockSpec(shape, lambda j, k: (j,) + (0,) * (len(shape) - 1))
    return pl.pallas_call(
        functools.partial(_s5_core_kernel, nsb=nsb, steps=steps),
        grid=(groups // gb, steps // pair),
        in_specs=[pl.BlockSpec((pair, gb, width, nseg), lambda j, k: (k, j, 0, 0)),
                  per_gb((gb, width, width)), per_gb((gb, 2 * S5_STATE, width)),
                  per_gb((gb, width, 2 * S5_STATE)),
                  per_gb((rows, nseg)), per_gb((rows, nseg)), per_gb((rows, nseg)),
                  per_gb((rows, nseg))],
        out_specs=pl.BlockSpec(memory_space=pl.ANY),
        out_shape=jax.ShapeDtypeStruct((nseg, steps * S5_SUB, dm), F32),
        scratch_shapes=[pltpu.VMEM((rows, nseg), F32), pltpu.VMEM((rows, nseg), F32),
                        pltpu.VMEM((2, pair, S5_SUB, nseg, gb * S5_GROUP), F32),
                        pltpu.SemaphoreType.DMA((2,))],
        compiler_params=_cparams(("arbitrary", "arbitrary")),
        name="s5_core",
    )(zt, tt, nn, mm, a_re, a_im, er, ei)


def _ffn_rows(x, g_ref, wg_ref, wu_ref, wd_ref, acc_ref, hc):
    u = _rms(x, g_ref[...]).astype(BF16)
    hidden = wg_ref.shape[1]
    for j in range(hidden // hc):
        cols = slice(j * hc, (j + 1) * hc)
        gt = _dot(u, wg_ref[:, cols])
        up = _dot(u, wu_ref[:, cols])
        h = (jax.nn.silu(gt) * up).astype(BF16)
        part = _dot(h, wd_ref[cols, :])
        if j == 0:
            acc_ref[...] = part
        else:
            acc_ref[...] += part
    return x + acc_ref[...]


def _layer_resident(shape, layer):
    nd = len(shape)
    return pl.BlockSpec((None,) + shape, lambda *_: (layer,) + (0,) * nd,
                        pipeline_mode=pl.Buffered(1))


def _ffn_specs(row, dm, hidden, layer, hc):
    assert hidden % hc == 0
    tile = pl.BlockSpec((row, dm), lambda i: (i, 0))
    weights = [_resident((1, dm)), _layer_resident((dm, hidden), layer),
               _layer_resident((dm, hidden), layer), _layer_resident((hidden, dm), layer)]
    return tile, weights


def _s5_ffn_kernel(x_ref, y_ref, g_ref, d_ref, w_ref, b_ref, fg_ref, wg_ref, wu_ref, wd_ref,
                   o_ref, acc_ref, *, hc):
    x = x_ref[...]
    u = _rms(x, g_ref[...])
    y = jax.nn.gelu(y_ref[...] + d_ref[...] * u)
    gate = jax.nn.sigmoid(_dot(y.astype(BF16), w_ref[...]) + b_ref[...])
    o_ref[...] = _ffn_rows(x + y * gate, fg_ref, wg_ref, wu_ref, wd_ref, acc_ref, hc)


def _s5_ffn_call(x, y, g, d, w, b, fg, wg, wu, wd, layer, row, hc):
    t, dm = x.shape
    hidden = wg.shape[-1]
    tile, weights = _ffn_specs(row, dm, hidden, layer, hc)
    return pl.pallas_call(
        functools.partial(_s5_ffn_kernel, hc=hc),
        grid=(t // row,),
        in_specs=[tile, tile, _resident((1, dm)), _resident((1, dm)), _resident((dm, dm)),
                  _resident((1, dm))] + weights,
        out_specs=tile,
        out_shape=jax.ShapeDtypeStruct((t, dm), F32),
        scratch_shapes=[pltpu.VMEM((row, dm), F32)],
        compiler_params=_cparams(("parallel",)),
        name="s5_ffn",
    )(x, y, g.reshape(1, dm), d.reshape(1, dm), w, b.reshape(1, dm), fg.reshape(1, dm), wg, wu, wd)


def _s5_tab_kernel(p_ref, q_ref, c1_ref, c2_ref, b1_ref, b2_ref, bs_ref, tt_ref, nn_ref, mm_ref):
    sub = S5_SUB
    width = sub * S5_GROUP
    lane = lax.broadcasted_iota(jnp.int32, (S5_GROUP, width), 1)
    sgn = jnp.where(lax.broadcasted_iota(jnp.int32, (1, 2 * S5_STATE), 1) < S5_STATE, 1.0, -1.0)
    for gi in range(p_ref.shape[0]):
        p, q = p_ref[gi], q_ref[gi]
        c1, c2, b1, b2 = c1_ref[gi], c2_ref[gi], b1_ref[gi], b2_ref[gi]
        cl = [c1 * p[k:k + 1] + c2 * q[k:k + 1] for k in range(sub + 1)]
        mm_ref[gi] = (jnp.concatenate(cl[1:], axis=0) * sgn).astype(mm_ref.dtype)
        r = jnp.dot(bs_ref[gi], jnp.concatenate(cl[:sub], axis=0).T,
                    precision=lax.Precision.HIGHEST, preferred_element_type=F32)
        rows = [r] + [jnp.where(lane >= S5_GROUP * s, pltpu.roll(r, S5_GROUP * s, axis=1), 0.0)
                      for s in range(1, sub)]
        tt_ref[gi] = jnp.concatenate(rows, axis=0).T.astype(tt_ref.dtype)
        nt = [b1 * p[sub - 1 - s:sub - s] + b2 * q[sub - 1 - s:sub - s] for s in range(sub)]
        nn_ref[gi] = jnp.concatenate(nt, axis=0).T.astype(nn_ref.dtype)


def _s5_tab_call(p, q, c1, c2, b1, b2, bs):
    groups = p.shape[0]
    gt = S5_TAB_GROUPS if groups % S5_TAB_GROUPS == 0 else 1
    width = S5_SUB * S5_GROUP
    st2 = 2 * S5_STATE
    blk = lambda a: pl.BlockSpec((gt,) + a.shape[1:], lambda i: (i, 0, 0))
    out = lambda r, c: pl.BlockSpec((gt, r, c), lambda i: (i, 0, 0))
    return pl.pallas_call(
        _s5_tab_kernel,
        grid=(groups // gt,),
        in_specs=[blk(a) for a in (p, q, c1, c2, b1, b2, bs)],
        out_specs=[out(width, width), out(st2, width), out(width, st2)],
        out_shape=[jax.ShapeDtypeStruct((groups, width, width), BF16),
                   jax.ShapeDtypeStruct((groups, st2, width), BF16),
                   jax.ShapeDtypeStruct((groups, width, st2), BF16)],
        compiler_params=_cparams(("parallel",)),
        name="s5_tab",
    )(p, q, c1, c2, b1, b2, bs)


def _s5_tables(lam_re, lam_im, log_step, b_re, b_im, c_re, c_im, nseg):
    sub = S5_SUB
    step = jnp.exp(log_step)[:, None]
    dre, dim = lam_re * step, lam_im * step

    def power(k):
        kk = jnp.asarray(k, F32)[None, :, None]
        mag = jnp.exp(kk * dre[:, None, :])
        return mag * jnp.cos(kk * dim[:, None, :]), mag * jnp.sin(kk * dim[:, None, :])

    pw_re, pw_im = power(jnp.arange(sub + 1))
    den = lam_re * lam_re + lam_im * lam_im
    nr, ni = pw_re[:, 1] - 1.0, pw_im[:, 1]
    cf_re = ((nr * lam_re + ni * lam_im) / den)[:, None, :]
    cf_im = ((ni * lam_re - nr * lam_im) / den)[:, None, :]
    bt_re, bt_im = b_re.transpose(0, 2, 1), b_im.transpose(0, 2, 1)
    bb_re = cf_re * bt_re - cf_im * bt_im
    bb_im = cf_re * bt_im + cf_im * bt_re
    cat = lambda a, b: jnp.concatenate([a, b], axis=-1)
    tt, nn, mm = _s5_tab_call(cat(pw_re, pw_im), cat(pw_im, pw_re), cat(c_re, c_re),
                              cat(-c_im, c_im), cat(bb_re, bb_re), cat(-bb_im, bb_im),
                              cat(bb_re, -bb_im))
    a_re = jnp.broadcast_to(pw_re[:, sub].reshape(-1, 1), (pw_re[:, sub].size, nseg))
    a_im = jnp.broadcast_to(pw_im[:, sub].reshape(-1, 1), (pw_im[:, sub].size, nseg))
    return tt, nn, mm, a_re, a_im


def _s5_layer(x, batch, seq, norm_w, lam_re, lam_im, log_step, b_re, b_im, c_re, c_im, d, glu_w,
              glu_b, ffn_g, ffn_wg, ffn_wu, ffn_wd, ffn_layer, row, hc):
    t, dm = x.shape
    nseg = V7X_LANES
    seg_len = t // nseg
    assert t % nseg == 0 and seg_len % S5_SUB == 0 and seq % seg_len == 0
    nsb = seq // seg_len
    tt, nn, mm, a_re, a_im = _s5_tables(lam_re, lam_im, log_step, b_re, b_im, c_re, c_im, nseg)
    zt, er, ei = _s5_pre_call(x.reshape(nseg, seg_len, dm), norm_w, nn, a_re, a_im)
    y = _s5_core_call(zt, tt, nn, mm, a_re, a_im, er, ei, nsb, dm).reshape(t, dm)
    return _s5_ffn_call(x, y, norm_w, d.reshape(-1), glu_w.astype(BF16), glu_b, ffn_g,
                        ffn_wg, ffn_wu, ffn_wd, ffn_layer, row, hc)


def _ret_kernel(x_ref, g_ref, w_ref, cr_ref, sr_ref, cb_ref, sb_ref, dm_ref, qd_ref, kd_ref, cd_ref,
                o_ref, state_ref, *, heads, blk):
    @pl.when(pl.program_id(1) == 0)
    def _():
        state_ref[...] = jnp.zeros_like(state_ref)

    u = _rms(x_ref[...], g_ref[...]).astype(BF16)
    cb = cb_ref[pl.ds(pl.program_id(1), 1), :]
    sb = sb_ref[pl.ds(pl.program_id(1), 1), :]
    cos = cr_ref[...] * cb - sr_ref[...] * sb
    sin = sr_ref[...] * cb + cr_ref[...] * sb
    half = RET_QK_DIM // 2
    qk_w = heads * RET_QK_DIM
    k_scale = RET_QK_DIM ** -0.5
    row = x_ref.shape[0]

    def rotary(tq):
        t1, t2 = tq[:, :half], tq[:, half:]
        return jnp.concatenate([t1 * cos - t2 * sin, t1 * sin + t2 * cos], axis=-1)

    for h in range(heads):
        cq = h * RET_QK_DIM
        cv = 2 * qk_w + h * RET_V_DIM
        qf = rotary(_dot(u, w_ref[:, cq:cq + RET_QK_DIM]))
        q = qf.astype(BF16)
        k = rotary(_dot(u, w_ref[:, qk_w + cq:qk_w + cq + RET_QK_DIM]) * k_scale)
        v = _dot(u, w_ref[:, cv:cv + RET_V_DIM]).astype(BF16)
        for r0 in range(0, row, blk):
            qs, ks, vs = q[r0:r0 + blk], k[r0:r0 + blk], v[r0:r0 + blk]
            s = lax.dot_general(qs, ks.astype(BF16), (((1,), (1,)), ((), ())),
                                preferred_element_type=F32)
            st = state_ref[h]
            qd = (qf[r0:r0 + blk] * qd_ref[h]).astype(BF16)
            o_ref[r0:r0 + blk, h * RET_V_DIM:(h + 1) * RET_V_DIM] = (
                _dot((s * dm_ref[h]).astype(BF16), vs) + _dot(qd, st.astype(BF16)))
            kd = (ks * kd_ref[h]).astype(BF16)
            state_ref[h] = st * cd_ref[h] + lax.dot_general(
                kd, vs, (((0,), (0,)), ((), ())), preferred_element_type=F32)


def _ret_call(x, g, w, rot, dmask, qdec, kdec, cdec, batch, seq, heads, row, blk):
    t, dm = x.shape
    v_w = heads * RET_V_DIM
    qkv_w = 2 * heads * RET_QK_DIM + v_w
    nb = seq // row
    kern = functools.partial(_ret_kernel, heads=heads, blk=blk)
    rows = lambda width: pl.BlockSpec((row, width), lambda b, i: (b * nb + i, 0))
    tabs = [_resident(a.shape) for a in rot]
    return pl.pallas_call(
        kern,
        grid=(batch, nb),
        in_specs=[rows(dm), _resident((1, dm)), _resident((dm, qkv_w))] + tabs + [
                  _resident(dmask.shape), _resident(qdec.shape),
                  _resident(kdec.shape), _resident(cdec.shape)],
        out_specs=rows(v_w),
        out_shape=jax.ShapeDtypeStruct((t, v_w), F32),
        scratch_shapes=[pltpu.VMEM((heads, RET_QK_DIM, RET_V_DIM), F32)],
        compiler_params=_cparams(("parallel", "arbitrary")),
        name="ret_mix",
    )(x, g.reshape(1, dm), w, *rot, dmask, qdec, kdec, cdec)


def _ret_decays(heads, blk):
    log_gamma = jnp.log1p(-jnp.exp2(-5.0 - jnp.arange(heads, dtype=F32)))
    pos = jnp.arange(blk, dtype=F32)
    diff = pos[:, None] - pos[None, :]
    cn = (jnp.arange(blk) // CHUNK)[:, None]
    cm = (jnp.arange(blk) // CHUNK)[None, :]
    expo = jnp.where(cn == cm, jnp.abs(diff), diff)
    dmask = jnp.where((cm <= cn)[None], jnp.exp(log_gamma[:, None, None] * expo[None]), 0.0)
    qdec = jnp.exp((pos[None, :] + 1.0) * log_gamma[:, None])[..., None]
    kdec = jnp.exp((blk - 1.0 - pos)[None, :] * log_gamma[:, None])[..., None]
    cdec = jnp.exp(blk * log_gamma)[:, None, None]
    return dmask, qdec, kdec, cdec


def _rotary_tables(seq, row):
    inv_freq = 1.0 / (ROPE_BASE ** jnp.linspace(0.0, 1.0, RET_QK_DIM // 2, dtype=F32))
    ang_r = jnp.arange(row, dtype=F32)[:, None] * inv_freq[None, :]
    ang_b = (jnp.arange(seq // row, dtype=F32) * row)[:, None] * inv_freq[None, :]
    return jnp.cos(ang_r), jnp.sin(ang_r), jnp.cos(ang_b), jnp.sin(ang_b)


def _ret_ffn_kernel(x_ref, y_ref, g_ref, wgate_ref, gn_ref, w_ref, fg_ref, wg_ref, wu_ref, wd_ref,
                    fin_ref, o_ref, acc_ref, *, hc):
    x = x_ref[...]
    u = _rms(x, g_ref[...]).astype(BF16)
    x1 = x
    for h in range(y_ref.shape[1] // RET_V_DIM):
        cols = slice(h * RET_V_DIM, (h + 1) * RET_V_DIM)
        gt = _dot(u, wgate_ref[:, cols])
        o = y_ref[:, cols]
        mean = jnp.mean(o, axis=-1, keepdims=True)
        cen = o - mean
        var = jnp.mean(jnp.square(cen), axis=-1, keepdims=True)
        on = cen * lax.rsqrt(var + NORM_EPS) * gn_ref[:, cols]
        x1 = x1 + _dot((jax.nn.silu(gt) * on).astype(BF16), w_ref[cols, :])
    out = _ffn_rows(x1, fg_ref, wg_ref, wu_ref, wd_ref, acc_ref, hc)
    o_ref[...] = _rms(out, fin_ref[...])


def _ret_ffn_call(x, y, g, w_qkvg, gn_w, w, fg, wg, wu, wd, layer, fin, row, hc):
    t, dm = x.shape
    hidden = wg.shape[-1]
    kdim = y.shape[1]
    assert w_qkvg.shape[1] % kdim == 0
    gate_block = w_qkvg.shape[1] // kdim - 1
    tile, weights = _ffn_specs(row, dm, hidden, layer, hc)
    return pl.pallas_call(
        functools.partial(_ret_ffn_kernel, hc=hc),
        grid=(t // row,),
        in_specs=[tile, pl.BlockSpec((row, kdim), lambda i: (i, 0)), _resident((1, dm)),
                  pl.BlockSpec((dm, kdim), lambda i: (0, gate_block), pipeline_mode=pl.Buffered(1)),
                  _resident((1, kdim)), _resident((kdim, dm))]
        + weights + [_resident((1, dm))],
        out_specs=tile,
        out_shape=jax.ShapeDtypeStruct((t, dm), F32),
        scratch_shapes=[pltpu.VMEM((row, dm), F32)],
        compiler_params=_cparams(("parallel",)),
        name="ret_ffn",
    )(x, y, g.reshape(1, dm), w_qkvg, gn_w.reshape(1, kdim), w, fg.reshape(1, dm), wg, wu, wd,
      fin.reshape(1, dm))


def _ret_layer(x, batch, seq, norm_w, w_qkvg, gn_w, w_o, ffn_g, ffn_wg, ffn_wu, ffn_wd, ffn_layer,
               fin, row, row_out, blk, hc):
    dm = x.shape[1]
    heads = dm // RET_QK_DIM
    rot = _rotary_tables(seq, row)
    dmask, qdec, kdec, cdec = _ret_decays(heads, blk)
    w_bf = w_qkvg.astype(BF16)
    o = _ret_call(x, norm_w, w_bf, rot, dmask, qdec, kdec, cdec, batch, seq, heads, row, blk)
    return _ret_ffn_call(x, o, norm_w, w_bf, gn_w, w_o.astype(BF16), ffn_g, ffn_wg, ffn_wu, ffn_wd,
                         ffn_layer, fin, row_out, hc)


def kernel(x, s5_norm, s5_lambda_re, s5_lambda_im, s5_log_step, s5_b_re, s5_b_im, s5_c_re, s5_c_im,
           s5_d, s5_glu_w, s5_glu_b, ret_norm, ret_w_qkvg, ret_gn_w, ret_w_o, ffn_norm, ffn_w_gate,
           ffn_w_up, ffn_w_down, final_norm):
    batch, seq, dm = x.shape
    row, row_out, blk, hc = _tiles(seq)
    assert ffn_norm.shape[0] == 2 and s5_norm.shape[0] == 1 and ret_norm.shape[0] == 1
    h = x.reshape(batch * seq, dm)
    wg, wu, wd = ffn_w_gate.astype(BF16), ffn_w_up.astype(BF16), ffn_w_down.astype(BF16)
    h = _s5_layer(h, batch, seq, s5_norm[0], s5_lambda_re[0], s5_lambda_im[0], s5_log_step[0],
                  s5_b_re[0], s5_b_im[0], s5_c_re[0], s5_c_im[0], s5_d[0], s5_glu_w[0],
                  s5_glu_b[0], ffn_norm[0], wg, wu, wd, 0, row, hc)
    h = _ret_layer(h, batch, seq, ret_norm[0], ret_w_qkvg[0], ret_gn_w[0], ret_w_o[0],
                   ffn_norm[1], wg, wu, wd, 1, final_norm,
                   row, row_out, blk, hc)
    return h.reshape(batch, seq, dm)
```

```python
import functools

import jax
import jax.numpy as jnp
from jax import lax
from jax.experimental import pallas as pl
from jax.experimental.pallas import tpu as pltpu

F32 = jnp.float32
BF16 = jnp.bfloat16

NORM_EPS = 1e-6
CHUNK = 64
S5_GROUP = 16
S5_STATE = 64
S5_SUB = 16
S5_GB = 16
S5_TAB_GROUPS = 16
GLU_KSPLIT = 4
RET_QK_DIM = 256
RET_V_DIM = 512
ROPE_BASE = 10000.0

V7X_VMEM_BYTES = 64 * 1024 * 1024
V7X_LANES = 128
VMEM_LIMIT = V7X_VMEM_BYTES * 7 // 8


def _tiles(seq):
    row = min(512, seq)
    row_out = min(1024, seq)
    ret = min(256, seq)
    hc = 256
    assert seq % row == 0 and seq % row_out == 0 and row % ret == 0 and ret % CHUNK == 0
    return row, row_out, ret, hc


def _cparams(sem):
    return pltpu.CompilerParams(dimension_semantics=sem, vmem_limit_bytes=VMEM_LIMIT)


def _resident(shape):
    nd = len(shape)
    return pl.BlockSpec(shape, lambda *_: (0,) * nd, pipeline_mode=pl.Buffered(1))


def _rms(xf, g):
    ms = jnp.mean(jnp.square(xf), axis=-1, keepdims=True)
    return xf * lax.rsqrt(ms + NORM_EPS) * g


def _dot(a, b):
    return jnp.dot(a, b, preferred_element_type=F32)


def _bdot(a, b):
    return lax.dot_general(a, b, (((2,), (1,)), ((0,), (0,))), preferred_element_type=F32)


def _s5_pre_kernel(x_hbm, g_ref, nn_ref, are_ref, aim_ref, zt_ref, er_ref, ei_ref, xs, sem):
    groups = zt_ref.shape[1]
    nseg = zt_ref.shape[3]
    i = pl.program_id(0)
    steps = pl.num_programs(0)

    def gathers(step, slot):
        return [pltpu.make_async_copy(x_hbm.at[:, step * S5_SUB + t, :], xs.at[slot, t],
                                      sem.at[slot]) for t in range(S5_SUB)]

    @pl.when(i == 0)
    def _():
        for c in gathers(0, 0):
            c.start()

    @pl.when(i + 1 < steps)
    def _():
        for c in gathers(i + 1, (i + 1) % 2):
            c.start()

    slot = i % 2
    for c in gathers(i, slot):
        c.wait()
    g = g_ref[...]
    for t in range(S5_SUB):
        ut = _rms(xs[slot, t], g)
        zt_ref[0, :, t * S5_GROUP:(t + 1) * S5_GROUP, :] = (
            ut.T.reshape(groups, S5_GROUP, nseg).astype(zt_ref.dtype))
    s = _bdot(nn_ref[...], zt_ref[0])
    sr = s[:, :S5_STATE, :].reshape(groups * S5_STATE, nseg)
    si = s[:, S5_STATE:, :].reshape(groups * S5_STATE, nseg)

    @pl.when(i == 0)
    def _():
        er_ref[...] = sr
        ei_ref[...] = si

    @pl.when(i > 0)
    def _():
        er, ei = er_ref[...], ei_ref[...]
        ar, ai = are_ref[...], aim_ref[...]
        er_ref[...] = ar * er - ai * ei + sr
        ei_ref[...] = ar * ei + ai * er + si


def _s5_pre_call(x3, g, nn, a_re, a_im):
    nseg, sl, dm = x3.shape
    groups = dm // S5_GROUP
    steps = sl // S5_SUB
    width = S5_SUB * S5_GROUP
    rows = groups * S5_STATE
    return pl.pallas_call(
        _s5_pre_kernel,
        grid=(steps,),
        in_specs=[pl.BlockSpec(memory_space=pl.ANY), _resident((1, dm)),
                  _resident(nn.shape), _resident((rows, nseg)), _resident((rows, nseg))],
        out_specs=[pl.BlockSpec((1, groups, width, nseg), lambda i: (i, 0, 0, 0)),
                   pl.BlockSpec((rows, nseg), lambda i: (0, 0)),
                   pl.BlockSpec((rows, nseg), lambda i: (0, 0))],
        out_shape=[jax.ShapeDtypeStruct((steps, groups, width, nseg), BF16),
                   jax.ShapeDtypeStruct((rows, nseg), F32),
                   jax.ShapeDtypeStruct((rows, nseg), F32)],
        scratch_shapes=[pltpu.VMEM((2, S5_SUB, nseg, dm), F32), pltpu.SemaphoreType.DMA((2,))],
        compiler_params=_cparams(("arbitrary",)),
        name="s5_pre",
    )(x3, g.reshape(1, dm), nn, a_re, a_im)


def _s5_stitch(er_ref, ei_ref, are_ref, aim_ref, xr_ref, xi_ref, nsb, steps):
    er, ei = er_ref[...], ei_ref[...]
    pr, pi = are_ref[...], aim_ref[...]
    for _ in range(steps.bit_length() - 1):
        pr, pi = pr * pr - pi * pi, 2.0 * pr * pi
    lane = lax.broadcasted_iota(jnp.int32, er.shape, 1) % nsb
    d = 1
    while d < nsb:
        sr, si = pltpu.roll(er, d, axis=1), pltpu.roll(ei, d, axis=1)
        ok = lane >= d
        er, ei = (er + jnp.where(ok, pr * sr - pi * si, 0.0),
                  ei + jnp.where(ok, pr * si + pi * sr, 0.0))
        pr, pi = pr * pr - pi * pi, 2.0 * pr * pi
        d = 2 * d
    ok = lane >= 1
    xr_ref[...] = jnp.where(ok, pltpu.roll(er, 1, axis=1), 0.0)
    xi_ref[...] = jnp.where(ok, pltpu.roll(ei, 1, axis=1), 0.0)


def _s5_core_kernel(zt_ref, tt_ref, nn_ref, mm_ref, are_ref, aim_ref, er_ref, ei_ref, y_hbm,
                    xr_scr, xi_scr, ybuf, sem, *, nsb, steps):
    pair, gb, _, nseg = zt_ref.shape
    cols = gb * S5_GROUP
    j, k = pl.program_id(0), pl.program_id(1)
    n = j * pl.num_programs(1) + k
    last = pl.num_programs(0) * pl.num_programs(1) - 1
    slot = n % 2

    def scatters(sl):
        return [pltpu.make_async_copy(
            ybuf.at[sl, s, t],
            y_hbm.at[:, (k * pair + s) * S5_SUB + t, pl.ds(j * cols, cols)],
            sem.at[sl]) for s in range(pair) for t in range(S5_SUB)]

    @pl.when(k == 0)
    def _():
        _s5_stitch(er_ref, ei_ref, are_ref, aim_ref, xr_scr, xi_scr, nsb, steps)

    @pl.when(n >= 2)
    def _():
        for c in scatters(slot):
            c.wait()

    zt = jnp.concatenate([zt_ref[s] for s in range(pair)], axis=-1)
    s_all = _bdot(nn_ref[...], zt)
    ar, ai = are_ref[...], aim_ref[...]
    xr, xi = xr_scr[...], xi_scr[...]
    starts = []
    for s in range(pair):
        starts.append(jnp.concatenate([xr.reshape(gb, S5_STATE, nseg),
                                       xi.reshape(gb, S5_STATE, nseg)], axis=1))
        inc = s_all[:, :, s * nseg:(s + 1) * nseg]
        sr = inc[:, :S5_STATE, :].reshape(gb * S5_STATE, nseg)
        si = inc[:, S5_STATE:, :].reshape(gb * S5_STATE, nseg)
        xr, xi = ar * xr - ai * xi + sr, ar * xi + ai * xr + si
    xr_scr[...] = xr
    xi_scr[...] = xi
    xprev = jnp.concatenate(starts, axis=-1).astype(zt.dtype)
    y = _bdot(tt_ref[...], zt) + _bdot(mm_ref[...], xprev)
    for t in range(S5_SUB):
        blk = y[:, t * S5_GROUP:(t + 1) * S5_GROUP, :].reshape(cols, pair * nseg)
        bt = blk.T
        for s in range(pair):
            ybuf[slot, s, t] = bt[s * nseg:(s + 1) * nseg]
    for c in scatters(slot):
        c.start()

    @pl.when(n == last)
    def _():
        for c in scatters(slot):
            c.wait()

    @pl.when(jnp.logical_and(n == last, n >= 1))
    def _():
        for c in scatters(1 - slot):
            c.wait()


def _s5_core_call(zt, tt, nn, mm, a_re, a_im, er, ei, nsb, dm):
    steps, groups, width, nseg = zt.shape
    assert steps & (steps - 1) == 0
    gb = min(S5_GB, groups)
    pair = max(p for p in (1, 2, 4) if steps % p == 0)
    rows = gb * S5_STATE
    per_gb = lambda shape: pl.BlockSpec(shape, lambda j, k: (j,) + (0,) * (len(shape) - 1))
    return pl.pallas_call(
        functools.partial(_s5_core_kernel, nsb=nsb, steps=steps),
        grid=(groups // gb, steps // pair),
        in_specs=[pl.BlockSpec((pair, gb, width, nseg), lambda j, k: (k, j, 0, 0)),
                  per_gb((gb, width, width)), per_gb((gb, 2 * S5_STATE, width)),
                  per_gb((gb, width, 2 * S5_STATE)),
                  per_gb((rows, nseg)), per_gb((rows, nseg)), per_gb((rows, nseg)),
                  per_gb((rows, nseg))],
        out_specs=pl.BlockSpec(memory_space=pl.ANY),
        out_shape=jax.ShapeDtypeStruct((nseg, steps * S5_SUB, dm), F32),
        scratch_shapes=[pltpu.VMEM((rows, nseg), F32), pltpu.VMEM((rows, nseg), F32),
                        pltpu.VMEM((2, pair, S5_SUB, nseg, gb * S5_GROUP), F32),
                        pltpu.SemaphoreType.DMA((2,))],
        compiler_params=_cparams(("arbitrary", "arbitrary")),
        name="s5_core",
    )(zt, tt, nn, mm, a_re, a_im, er, ei)


def _ffn_rows(x, g_ref, wg_ref, wu_ref, wd_ref, acc_ref, hc):
    u = _rms(x, g_ref[...]).astype(BF16)
    hidden = wg_ref.shape[1]
    for j in range(hidden // hc):
        cols = slice(j * hc, (j + 1) * hc)
        gt = _dot(u, wg_ref[:, cols])
        up = _dot(u, wu_ref[:, cols])
        h = (jax.nn.silu(gt) * up).astype(BF16)
        part = _dot(h, wd_ref[cols, :])
        if j == 0:
            acc_ref[...] = part
        else:
            acc_ref[...] += part
    return x + acc_ref[...]


def _layer_resident(shape, layer):
    nd = len(shape)
    return pl.BlockSpec((None,) + shape, lambda *_: (layer,) + (0,) * nd,
                        pipeline_mode=pl.Buffered(1))


def _ffn_specs(row, dm, hidden, layer, hc):
    assert hidden % hc == 0
    tile = pl.BlockSpec((row, dm), lambda i: (i, 0))
    weights = [_resident((1, dm)), _layer_resident((dm, hidden), layer),
               _layer_resident((dm, hidden), layer), _layer_resident((hidden, dm), layer)]
    return tile, weights


def _s5_ffn_kernel(x_ref, y_ref, g_ref, d_ref, w_ref, b_ref, fg_ref, wg_ref, wu_ref, wd_ref,
                   o_ref, acc_ref, *, hc):
    x = x_ref[...]
    r = lax.rsqrt(jnp.mean(jnp.square(x), axis=-1, keepdims=True) + NORM_EPS)
    kb = x.shape[1] // GLU_KSPLIT
    ys, pre = [], None
    for k in range(GLU_KSPLIT):
        cs = slice(k * kb, (k + 1) * kb)
        yk = jax.nn.gelu(y_ref[:, cs] + d_ref[:, cs] * (x[:, cs] * r * g_ref[:, cs]))
        part = _dot(yk.astype(BF16), w_ref[cs, :])
        pre = part if pre is None else pre + part
        ys.append(yk)
    y = jnp.concatenate(ys, axis=1)
    gate = jax.nn.sigmoid(pre + b_ref[...])
    o_ref[...] = _ffn_rows(x + y * gate, fg_ref, wg_ref, wu_ref, wd_ref, acc_ref, hc)


def _s5_ffn_call(x, y, g, d, w, b, fg, wg, wu, wd, layer, row, hc):
    t, dm = x.shape
    hidden = wg.shape[-1]
    tile, weights = _ffn_specs(row, dm, hidden, layer, hc)
    return pl.pallas_call(
        functools.partial(_s5_ffn_kernel, hc=hc),
        grid=(t // row,),
        in_specs=[tile, tile, _resident((1, dm)), _resident((1, dm)), _resident((dm, dm)),
                  _resident((1, dm))] + weights,
        out_specs=tile,
        out_shape=jax.ShapeDtypeStruct((t, dm), F32),
        scratch_shapes=[pltpu.VMEM((row, dm), F32)],
        compiler_params=_cparams(("parallel",)),
        name="s5_ffn",
    )(x, y, g.reshape(1, dm), d.reshape(1, dm), w, b.reshape(1, dm), fg.reshape(1, dm), wg, wu, wd)


def _s5_tab_kernel(p_ref, q_ref, c1_ref, c2_ref, b1_ref, b2_ref, bs_ref, tt_ref, nn_ref, mm_ref):
    sub = S5_SUB
    width = sub * S5_GROUP
    lane = lax.broadcasted_iota(jnp.int32, (S5_GROUP, width), 1)
    sgn = jnp.where(lax.broadcasted_iota(jnp.int32, (1, 2 * S5_STATE), 1) < S5_STATE, 1.0, -1.0)
    for gi in range(p_ref.shape[0]):
        p, q = p_ref[gi], q_ref[gi]
        c1, c2, b1, b2 = c1_ref[gi], c2_ref[gi], b1_ref[gi], b2_ref[gi]
        cl = [c1 * p[k:k + 1] + c2 * q[k:k + 1] for k in range(sub + 1)]
        mm_ref[gi] = (jnp.concatenate(cl[1:], axis=0) * sgn).astype(mm_ref.dtype)
        r = jnp.dot(bs_ref[gi], jnp.concatenate(cl[:sub], axis=0).T,
                    precision=lax.Precision.HIGHEST, preferred_element_type=F32)
        rows = [r] + [jnp.where(lane >= S5_GROUP * s, pltpu.roll(r, S5_GROUP * s, axis=1), 0.0)
                      for s in range(1, sub)]
        tt_ref[gi] = jnp.concatenate(rows, axis=0).T.astype(tt_ref.dtype)
        nt = [b1 * p[sub - 1 - s:sub - s] + b2 * q[sub - 1 - s:sub - s] for s in range(sub)]
        nn_ref[gi] = jnp.concatenate(nt, axis=0).T.astype(nn_ref.dtype)


def _s5_tab_call(p, q, c1, c2, b1, b2, bs):
    groups = p.shape[0]
    gt = S5_TAB_GROUPS if groups % S5_TAB_GROUPS == 0 else 1
    width = S5_SUB * S5_GROUP
    st2 = 2 * S5_STATE
    blk = lambda a: pl.BlockSpec((gt,) + a.shape[1:], lambda i: (i, 0, 0))
    out = lambda r, c: pl.BlockSpec((gt, r, c), lambda i: (i, 0, 0))
    return pl.pallas_call(
        _s5_tab_kernel,
        grid=(groups // gt,),
        in_specs=[blk(a) for a in (p, q, c1, c2, b1, b2, bs)],
        out_specs=[out(width, width), out(st2, width), out(width, st2)],
        out_shape=[jax.ShapeDtypeStruct((groups, width, width), BF16),
                   jax.ShapeDtypeStruct((groups, st2, width), BF16),
                   jax.ShapeDtypeStruct((groups, width, st2), BF16)],
        compiler_params=_cparams(("parallel",)),
        name="s5_tab",
    )(p, q, c1, c2, b1, b2, bs)


def _s5_tables(lam_re, lam_im, log_step, b_re, b_im, c_re, c_im, nseg):
    sub = S5_SUB
    step = jnp.exp(log_step)[:, None]
    dre, dim = lam_re * step, lam_im * step

    def power(k):
        kk = jnp.asarray(k, F32)[None, :, None]
        mag = jnp.exp(kk * dre[:, None, :])
        return mag * jnp.cos(kk * dim[:, None, :]), mag * jnp.sin(kk * dim[:, None, :])

    pw_re, pw_im = power(jnp.arange(sub + 1))
    den = lam_re * lam_re + lam_im * lam_im
    nr, ni = pw_re[:, 1] - 1.0, pw_im[:, 1]
    cf_re = ((nr * lam_re + ni * lam_im) / den)[:, None, :]
    cf_im = ((ni * lam_re - nr * lam_im) / den)[:, None, :]
    bt_re, bt_im = b_re.transpose(0, 2, 1), b_im.transpose(0, 2, 1)
    bb_re = cf_re * bt_re - cf_im * bt_im
    bb_im = cf_re * bt_im + cf_im * bt_re
    cat = lambda a, b: jnp.concatenate([a, b], axis=-1)
    tt, nn, mm = _s5_tab_call(cat(pw_re, pw_im), cat(pw_im, pw_re), cat(c_re, c_re),
                              cat(-c_im, c_im), cat(bb_re, bb_re), cat(-bb_im, bb_im),
                              cat(bb_re, -bb_im))
    a_re = jnp.broadcast_to(pw_re[:, sub].reshape(-1, 1), (pw_re[:, sub].size, nseg))
    a_im = jnp.broadcast_to(pw_im[:, sub].reshape(-1, 1), (pw_im[:, sub].size, nseg))
    return tt, nn, mm, a_re, a_im


def _s5_layer(x, batch, seq, norm_w, lam_re, lam_im, log_step, b_re, b_im, c_re, c_im, d, glu_w,
              glu_b, ffn_g, ffn_wg, ffn_wu, ffn_wd, ffn_layer, row, hc):
    t, dm = x.shape
    nseg = V7X_LANES
    seg_len = t // nseg
    assert t % nseg == 0 and seg_len % S5_SUB == 0 and seq % seg_len == 0
    nsb = seq // seg_len
    tt, nn, mm, a_re, a_im = _s5_tables(lam_re, lam_im, log_step, b_re, b_im, c_re, c_im, nseg)
    zt, er, ei = _s5_pre_call(x.reshape(nseg, seg_len, dm), norm_w, nn, a_re, a_im)
    y = _s5_core_call(zt, tt, nn, mm, a_re, a_im, er, ei, nsb, dm).reshape(t, dm)
    return _s5_ffn_call(x, y, norm_w, d.reshape(-1), glu_w.astype(BF16), glu_b, ffn_g,
                        ffn_wg, ffn_wu, ffn_wd, ffn_layer, row, hc)


def _ret_kernel(x_ref, g_ref, w_ref, cr_ref, sr_ref, cb_ref, sb_ref, gn_ref, dm_ref, qd_ref, kd_ref, cd_ref,
                y_ref, state_ref, *, heads, blk):
    @pl.when(pl.program_id(1) == 0)
    def _():
        state_ref[...] = jnp.zeros_like(state_ref)

    u = _rms(x_ref[...], g_ref[...]).astype(BF16)
    cb = cb_ref[pl.ds(pl.program_id(1), 1), :]
    sb = sb_ref[pl.ds(pl.program_id(1), 1), :]
    cos = cr_ref[...] * cb - sr_ref[...] * sb
    sin = sr_ref[...] * cb + cr_ref[...] * sb
    half = RET_QK_DIM // 2
    qk_w = heads * RET_QK_DIM
    v_w = heads * RET_V_DIM
    k_scale = RET_QK_DIM ** -0.5
    row = x_ref.shape[0]

    def rotary(tq):
        t1, t2 = tq[:, :half], tq[:, half:]
        return jnp.concatenate([t1 * cos - t2 * sin, t1 * sin + t2 * cos], axis=-1)

    for h in range(heads):
        cq = h * RET_QK_DIM
        cv = 2 * qk_w + h * RET_V_DIM
        qf = rotary(_dot(u, w_ref[:, cq:cq + RET_QK_DIM]))
        q = qf.astype(BF16)
        k = rotary(_dot(u, w_ref[:, qk_w + cq:qk_w + cq + RET_QK_DIM]) * k_scale)
        v = _dot(u, w_ref[:, cv:cv + RET_V_DIM]).astype(BF16)
        gt = _dot(u, w_ref[:, v_w + cv:v_w + cv + RET_V_DIM])
        gn = gn_ref[:, h * RET_V_DIM:(h + 1) * RET_V_DIM]
        for r0 in range(0, row, blk):
            qs, ks, vs = q[r0:r0 + blk], k[r0:r0 + blk], v[r0:r0 + blk]
            s = lax.dot_general(qs, ks.astype(BF16), (((1,), (1,)), ((), ())),
                                preferred_element_type=F32)
            st = state_ref[h]
            qd = (qf[r0:r0 + blk] * qd_ref[h]).astype(BF16)
            o = _dot((s * dm_ref[h]).astype(BF16), vs) + _dot(qd, st.astype(BF16))
            kd = (ks * kd_ref[h]).astype(BF16)
            state_ref[h] = st * cd_ref[h] + lax.dot_general(
                kd, vs, (((0,), (0,)), ((), ())), preferred_element_type=F32)
            mean = jnp.mean(o, axis=-1, keepdims=True)
            cen = o - mean
            var = jnp.mean(jnp.square(cen), axis=-1, keepdims=True)
            on = cen * lax.rsqrt(var + NORM_EPS) * gn
            y_ref[r0:r0 + blk, h * RET_V_DIM:(h + 1) * RET_V_DIM] = (
                jax.nn.silu(gt[r0:r0 + blk]) * on).astype(y_ref.dtype)


def _ret_call(x, g, w, rot, gn_w, dmask, qdec, kdec, cdec, batch, seq, heads, row, blk):
    t, dm = x.shape
    v_w = heads * RET_V_DIM
    nb = seq // row
    kern = functools.partial(_ret_kernel, heads=heads, blk=blk)
    rows = lambda width: pl.BlockSpec((row, width), lambda b, i: (b * nb + i, 0))
    tabs = [_resident(a.shape) for a in rot]
    return pl.pallas_call(
        kern,
        grid=(batch, nb),
        in_specs=[rows(dm), _resident((1, dm)), _resident(w.shape)] + tabs + [
                  _resident((1, v_w)), _resident(dmask.shape), _resident(qdec.shape),
                  _resident(kdec.shape), _resident(cdec.shape)],
        out_specs=rows(v_w),
        out_shape=jax.ShapeDtypeStruct((t, v_w), BF16),
        scratch_shapes=[pltpu.VMEM((heads, RET_QK_DIM, RET_V_DIM), F32)],
        compiler_params=_cparams(("parallel", "arbitrary")),
        name="ret_mix",
    )(x, g.reshape(1, dm), w, *rot, gn_w.reshape(1, v_w), dmask, qdec, kdec, cdec)


def _ret_decays(heads, blk):
    log_gamma = jnp.log1p(-jnp.exp2(-5.0 - jnp.arange(heads, dtype=F32)))
    pos = jnp.arange(blk, dtype=F32)
    diff = pos[:, None] - pos[None, :]
    cn = (jnp.arange(blk) // CHUNK)[:, None]
    cm = (jnp.arange(blk) // CHUNK)[None, :]
    expo = jnp.where(cn == cm, jnp.abs(diff), diff)
    dmask = jnp.where((cm <= cn)[None], jnp.exp(log_gamma[:, None, None] * expo[None]), 0.0)
    qdec = jnp.exp((pos[None, :] + 1.0) * log_gamma[:, None])[..., None]
    kdec = jnp.exp((blk - 1.0 - pos)[None, :] * log_gamma[:, None])[..., None]
    cdec = jnp.exp(blk * log_gamma)[:, None, None]
    return dmask, qdec, kdec, cdec


def _rotary_tables(seq, row):
    inv_freq = 1.0 / (ROPE_BASE ** jnp.linspace(0.0, 1.0, RET_QK_DIM // 2, dtype=F32))
    ang_r = jnp.arange(row, dtype=F32)[:, None] * inv_freq[None, :]
    ang_b = (jnp.arange(seq // row, dtype=F32) * row)[:, None] * inv_freq[None, :]
    return jnp.cos(ang_r), jnp.sin(ang_r), jnp.cos(ang_b), jnp.sin(ang_b)


def _ret_ffn_kernel(x_ref, y_ref, w_ref, fg_ref, wg_ref, wu_ref, wd_ref, fin_ref, o_ref, acc_ref,
                    *, hc):
    x1 = x_ref[...] + _dot(y_ref[...], w_ref[...])
    out = _ffn_rows(x1, fg_ref, wg_ref, wu_ref, wd_ref, acc_ref, hc)
    o_ref[...] = _rms(out, fin_ref[...])


def _ret_ffn_call(x, y, w, fg, wg, wu, wd, layer, fin, row, hc):
    t, dm = x.shape
    hidden = wg.shape[-1]
    kdim = y.shape[1]
    tile, weights = _ffn_specs(row, dm, hidden, layer, hc)
    return pl.pallas_call(
        functools.partial(_ret_ffn_kernel, hc=hc),
        grid=(t // row,),
        in_specs=[tile, pl.BlockSpec((row, kdim), lambda i: (i, 0)), _resident((kdim, dm))]
        + weights + [_resident((1, dm))],
        out_specs=tile,
        out_shape=jax.ShapeDtypeStruct((t, dm), F32),
        scratch_shapes=[pltpu.VMEM((row, dm), F32)],
        compiler_params=_cparams(("parallel",)),
        name="ret_ffn",
    )(x, y, w, fg.reshape(1, dm), wg, wu, wd, fin.reshape(1, dm))


def _ret_layer(x, batch, seq, norm_w, w_qkvg, gn_w, w_o, ffn_g, ffn_wg, ffn_wu, ffn_wd, ffn_layer,
               fin, row, row_out, blk, hc):
    dm = x.shape[1]
    heads = dm // RET_QK_DIM
    rot = _rotary_tables(seq, row)
    dmask, qdec, kdec, cdec = _ret_decays(heads, blk)
    y = _ret_call(x, norm_w, w_qkvg.astype(BF16), rot, gn_w, dmask, qdec, kdec, cdec,
                  batch, seq, heads, row, blk)
    return _ret_ffn_call(x, y, w_o.astype(BF16), ffn_g, ffn_wg, ffn_wu, ffn_wd, ffn_layer, fin,
                         row_out, hc)


def kernel(x, s5_norm, s5_lambda_re, s5_lambda_im, s5_log_step, s5_b_re, s5_b_im, s5_c_re, s5_c_im,
           s5_d, s5_glu_w, s5_glu_b, ret_norm, ret_w_qkvg, ret_gn_w, ret_w_o, ffn_norm, ffn_w_gate,
           ffn_w_up, ffn_w_down, final_norm):
    batch, seq, dm = x.shape
    row, row_out, blk, hc = _tiles(seq)
    assert ffn_norm.shape[0] == 2 and s5_norm.shape[0] == 1 and ret_norm.shape[0] == 1
    h = x.reshape(batch * seq, dm)
    wg, wu, wd = ffn_w_gate.astype(BF16), ffn_w_up.astype(BF16), ffn_w_down.astype(BF16)
    h = _s5_layer(h, batch, seq, s5_norm[0], s5_lambda_re[0], s5_lambda_im[0], s5_log_step[0],
                  s5_b_re[0], s5_b_im[0], s5_c_re[0], s5_c_im[0], s5_d[0], s5_glu_w[0],
                  s5_glu_b[0], ffn_norm[0], wg, wu, wd, 0, row, hc)
    h = _ret_layer(h, batch, seq, ret_norm[0], ret_w_qkvg[0], ret_gn_w[0], ret_w_o[0],
                   ffn_norm[1], wg, wu, wd, 1, final_norm,
                   row, row_out, blk, hc)
    return h.reshape(batch, seq, dm)
```

```python
import functools

import jax
import jax.numpy as jnp
from jax import lax
from jax.experimental import pallas as pl
from jax.experimental.pallas import tpu as pltpu

F32 = jnp.float32
BF16 = jnp.bfloat16

NORM_EPS = 1e-6
CHUNK = 64
S5_GROUP = 16
S5_STATE = 64
S5_SUB = 16
S5_GB = 16
S5_TAB_GROUPS = 16
GLU_KSPLIT = 4
RET_QK_DIM = 256
RET_V_DIM = 512
ROPE_BASE = 10000.0

V7X_VMEM_BYTES = 64 * 1024 * 1024
V7X_LANES = 128
VMEM_LIMIT = V7X_VMEM_BYTES * 7 // 8


def _tiles(seq):
    row = min(512, seq)
    row_out = min(1024, seq)
    ret = min(256, seq)
    hc = 256
    assert seq % row == 0 and seq % row_out == 0 and row % ret == 0 and ret % CHUNK == 0
    return row, row_out, ret, hc


def _cparams(sem):
    return pltpu.CompilerParams(dimension_semantics=sem, vmem_limit_bytes=VMEM_LIMIT)


def _resident(shape):
    nd = len(shape)
    return pl.BlockSpec(shape, lambda *_: (0,) * nd, pipeline_mode=pl.Buffered(1))


def _rms(xf, g):
    ms = jnp.mean(jnp.square(xf), axis=-1, keepdims=True)
    return xf * lax.rsqrt(ms + NORM_EPS) * g


def _dot(a, b):
    return jnp.dot(a, b, preferred_element_type=F32)


def _bdot(a, b):
    return lax.dot_general(a, b, (((2,), (1,)), ((0,), (0,))), preferred_element_type=F32)


def _s5_pre_kernel(x_hbm, g_ref, nn_ref, are_ref, aim_ref, zt_ref, er_ref, ei_ref, xs, sem):
    groups = zt_ref.shape[1]
    nseg = zt_ref.shape[3]
    i = pl.program_id(0)
    steps = pl.num_programs(0)

    def gathers(step, slot):
        return [pltpu.make_async_copy(x_hbm.at[:, step * S5_SUB + t, :], xs.at[slot, t],
                                      sem.at[slot]) for t in range(S5_SUB)]

    @pl.when(i == 0)
    def _():
        for n, c in enumerate(gathers(0, 0)):
            c.start(priority=n % 2)

    @pl.when(i + 1 < steps)
    def _():
        for n, c in enumerate(gathers(i + 1, (i + 1) % 2)):
            c.start(priority=n % 2)

    slot = i % 2
    for c in gathers(i, slot):
        c.wait()
    g = g_ref[...]
    for t in range(S5_SUB):
        ut = _rms(xs[slot, t], g)
        zt_ref[0, :, t * S5_GROUP:(t + 1) * S5_GROUP, :] = (
            ut.T.reshape(groups, S5_GROUP, nseg).astype(zt_ref.dtype))
    s = _bdot(nn_ref[...], zt_ref[0])
    sr = s[:, :S5_STATE, :].reshape(groups * S5_STATE, nseg)
    si = s[:, S5_STATE:, :].reshape(groups * S5_STATE, nseg)

    @pl.when(i == 0)
    def _():
        er_ref[...] = sr
        ei_ref[...] = si

    @pl.when(i > 0)
    def _():
        er, ei = er_ref[...], ei_ref[...]
        ar, ai = are_ref[...], aim_ref[...]
        er_ref[...] = ar * er - ai * ei + sr
        ei_ref[...] = ar * ei + ai * er + si


def _s5_pre_call(x3, g, nn, a_re, a_im):
    nseg, sl, dm = x3.shape
    groups = dm // S5_GROUP
    steps = sl // S5_SUB
    width = S5_SUB * S5_GROUP
    rows = groups * S5_STATE
    return pl.pallas_call(
        _s5_pre_kernel,
        grid=(steps,),
        in_specs=[pl.BlockSpec(memory_space=pl.ANY), _resident((1, dm)),
                  _resident(nn.shape), _resident((rows, nseg)), _resident((rows, nseg))],
        out_specs=[pl.BlockSpec((1, groups, width, nseg), lambda i: (i, 0, 0, 0)),
                   pl.BlockSpec((rows, nseg), lambda i: (0, 0)),
                   pl.BlockSpec((rows, nseg), lambda i: (0, 0))],
        out_shape=[jax.ShapeDtypeStruct((steps, groups, width, nseg), BF16),
                   jax.ShapeDtypeStruct((rows, nseg), F32),
                   jax.ShapeDtypeStruct((rows, nseg), F32)],
        scratch_shapes=[pltpu.VMEM((2, S5_SUB, nseg, dm), F32), pltpu.SemaphoreType.DMA((2,))],
        compiler_params=_cparams(("arbitrary",)),
        name="s5_pre",
    )(x3, g.reshape(1, dm), nn, a_re, a_im)


def _s5_stitch(er_ref, ei_ref, are_ref, aim_ref, xr_ref, xi_ref, nsb, steps):
    er, ei = er_ref[...], ei_ref[...]
    pr, pi = are_ref[...], aim_ref[...]
    for _ in range(steps.bit_length() - 1):
        pr, pi = pr * pr - pi * pi, 2.0 * pr * pi
    lane = lax.broadcasted_iota(jnp.int32, er.shape, 1) % nsb
    d = 1
    while d < nsb:
        sr, si = pltpu.roll(er, d, axis=1), pltpu.roll(ei, d, axis=1)
        ok = lane >= d
        er, ei = (er + jnp.where(ok, pr * sr - pi * si, 0.0),
                  ei + jnp.where(ok, pr * si + pi * sr, 0.0))
        pr, pi = pr * pr - pi * pi, 2.0 * pr * pi
        d = 2 * d
    ok = lane >= 1
    xr_ref[...] = jnp.where(ok, pltpu.roll(er, 1, axis=1), 0.0)
    xi_ref[...] = jnp.where(ok, pltpu.roll(ei, 1, axis=1), 0.0)


def _s5_core_kernel(zt_ref, tt_ref, nn_ref, mm_ref, are_ref, aim_ref, er_ref, ei_ref, y_hbm,
                    xr_scr, xi_scr, ybuf, sem, *, nsb, steps):
    pair, gb, _, nseg = zt_ref.shape
    cols = gb * S5_GROUP
    j, k = pl.program_id(0), pl.program_id(1)
    n = j * pl.num_programs(1) + k
    last = pl.num_programs(0) * pl.num_programs(1) - 1
    slot = n % 2

    def scatters(sl):
        return [pltpu.make_async_copy(
            ybuf.at[sl, s, t],
            y_hbm.at[:, (k * pair + s) * S5_SUB + t, pl.ds(j * cols, cols)],
            sem.at[sl]) for s in range(pair) for t in range(S5_SUB)]

    @pl.when(k == 0)
    def _():
        _s5_stitch(er_ref, ei_ref, are_ref, aim_ref, xr_scr, xi_scr, nsb, steps)

    @pl.when(n >= 2)
    def _():
        for c in scatters(slot):
            c.wait()

    zt = jnp.concatenate([zt_ref[s] for s in range(pair)], axis=-1)
    s_all = _bdot(nn_ref[...], zt)
    ar, ai = are_ref[...], aim_ref[...]
    xr, xi = xr_scr[...], xi_scr[...]
    starts = []
    for s in range(pair):
        starts.append(jnp.concatenate([xr.reshape(gb, S5_STATE, nseg),
                                       xi.reshape(gb, S5_STATE, nseg)], axis=1))
        inc = s_all[:, :, s * nseg:(s + 1) * nseg]
        sr = inc[:, :S5_STATE, :].reshape(gb * S5_STATE, nseg)
        si = inc[:, S5_STATE:, :].reshape(gb * S5_STATE, nseg)
        xr, xi = ar * xr - ai * xi + sr, ar * xi + ai * xr + si
    xr_scr[...] = xr
    xi_scr[...] = xi
    xprev = jnp.concatenate(starts, axis=-1).astype(zt.dtype)
    y = _bdot(tt_ref[...], zt) + _bdot(mm_ref[...], xprev)
    for t in range(S5_SUB):
        blk = y[:, t * S5_GROUP:(t + 1) * S5_GROUP, :].reshape(cols, pair * nseg)
        bt = blk.T
        for s in range(pair):
            ybuf[slot, s, t] = bt[s * nseg:(s + 1) * nseg]
    for idx, c in enumerate(scatters(slot)):
        c.start(priority=idx % 2)

    @pl.when(n == last)
    def _():
        for c in scatters(slot):
            c.wait()

    @pl.when(jnp.logical_and(n == last, n >= 1))
    def _():
        for c in scatters(1 - slot):
            c.wait()


def _s5_core_call(zt, tt, nn, mm, a_re, a_im, er, ei, nsb, dm):
    steps, groups, width, nseg = zt.shape
    assert steps & (steps - 1) == 0
    gb = min(S5_GB, groups)
    pair = max(p for p in (1, 2, 4) if steps % p == 0)
    rows = gb * S5_STATE
    per_gb = lambda shape: pl.BlockSpec(shape, lambda j, k: (j,) + (0,) * (len(shape) - 1))
    return pl.pallas_call(
        functools.partial(_s5_core_kernel, nsb=nsb, steps=steps),
        grid=(groups // gb, steps // pair),
        in_specs=[pl.BlockSpec((pair, gb, width, nseg), lambda j, k: (k, j, 0, 0)),
                  per_gb((gb, width, width)), per_gb((gb, 2 * S5_STATE, width)),
                  per_gb((gb, width, 2 * S5_STATE)),
                  per_gb((rows, nseg)), per_gb((rows, nseg)), per_gb((rows, nseg)),
                  per_gb((rows, nseg))],
        out_specs=pl.BlockSpec(memory_space=pl.ANY),
        out_shape=jax.ShapeDtypeStruct((nseg, steps * S5_SUB, dm), F32),
        scratch_shapes=[pltpu.VMEM((rows, nseg), F32), pltpu.VMEM((rows, nseg), F32),
                        pltpu.VMEM((2, pair, S5_SUB, nseg, gb * S5_GROUP), F32),
                        pltpu.SemaphoreType.DMA((2,))],
        compiler_params=_cparams(("arbitrary", "arbitrary")),
        name="s5_core",
    )(zt, tt, nn, mm, a_re, a_im, er, ei)


def _ffn_rows(x, g_ref, wg_ref, wu_ref, wd_ref, acc_ref, hc):
    u = _rms(x, g_ref[...]).astype(BF16)
    hidden = wg_ref.shape[1]
    for j in range(hidden // hc):
        cols = slice(j * hc, (j + 1) * hc)
        gt = _dot(u, wg_ref[:, cols])
        up = _dot(u, wu_ref[:, cols])
        h = (jax.nn.silu(gt) * up).astype(BF16)
        part = _dot(h, wd_ref[cols, :])
        if j == 0:
            acc_ref[...] = part
        else:
            acc_ref[...] += part
    return x + acc_ref[...]


def _layer_resident(shape, layer):
    nd = len(shape)
    return pl.BlockSpec((None,) + shape, lambda *_: (layer,) + (0,) * nd,
                        pipeline_mode=pl.Buffered(1))


def _ffn_specs(row, dm, hidden, layer, hc):
    assert hidden % hc == 0
    tile = pl.BlockSpec((row, dm), lambda i: (i, 0))
    weights = [_resident((1, dm)), _layer_resident((dm, hidden), layer),
               _layer_resident((dm, hidden), layer), _layer_resident((hidden, dm), layer)]
    return tile, weights


def _s5_ffn_kernel(x_ref, y_ref, g_ref, d_ref, w_ref, b_ref, fg_ref, wg_ref, wu_ref, wd_ref,
                   o_ref, acc_ref, *, hc):
    x = x_ref[...]
    r = lax.rsqrt(jnp.mean(jnp.square(x), axis=-1, keepdims=True) + NORM_EPS)
    kb = x.shape[1] // GLU_KSPLIT
    ys, pre = [], None
    for k in range(GLU_KSPLIT):
        cs = slice(k * kb, (k + 1) * kb)
        yk = jax.nn.gelu(y_ref[:, cs] + d_ref[:, cs] * (x[:, cs] * r * g_ref[:, cs]))
        part = _dot(yk.astype(BF16), w_ref[cs, :])
        pre = part if pre is None else pre + part
        ys.append(yk)
    y = jnp.concatenate(ys, axis=1)
    gate = jax.nn.sigmoid(pre + b_ref[...])
    o_ref[...] = _ffn_rows(x + y * gate, fg_ref, wg_ref, wu_ref, wd_ref, acc_ref, hc)


def _s5_ffn_call(x, y, g, d, w, b, fg, wg, wu, wd, layer, row, hc):
    t, dm = x.shape
    hidden = wg.shape[-1]
    tile, weights = _ffn_specs(row, dm, hidden, layer, hc)
    return pl.pallas_call(
        functools.partial(_s5_ffn_kernel, hc=hc),
        grid=(t // row,),
        in_specs=[tile, tile, _resident((1, dm)), _resident((1, dm)), _resident((dm, dm)),
                  _resident((1, dm))] + weights,
        out_specs=tile,
        out_shape=jax.ShapeDtypeStruct((t, dm), F32),
        scratch_shapes=[pltpu.VMEM((row, dm), F32)],
        compiler_params=_cparams(("parallel",)),
        name="s5_ffn",
    )(x, y, g.reshape(1, dm), d.reshape(1, dm), w, b.reshape(1, dm), fg.reshape(1, dm), wg, wu, wd)


def _s5_tab_kernel(p_ref, q_ref, c1_ref, c2_ref, b1_ref, b2_ref, bs_ref, tt_ref, nn_ref, mm_ref):
    sub = S5_SUB
    width = sub * S5_GROUP
    lane = lax.broadcasted_iota(jnp.int32, (S5_GROUP, width), 1)
    sgn = jnp.where(lax.broadcasted_iota(jnp.int32, (1, 2 * S5_STATE), 1) < S5_STATE, 1.0, -1.0)
    for gi in range(p_ref.shape[0]):
        p, q = p_ref[gi], q_ref[gi]
        c1, c2, b1, b2 = c1_ref[gi], c2_ref[gi], b1_ref[gi], b2_ref[gi]
        cl = [c1 * p[k:k + 1] + c2 * q[k:k + 1] for k in range(sub + 1)]
        mm_ref[gi] = (jnp.concatenate(cl[1:], axis=0) * sgn).astype(mm_ref.dtype)
        r = jnp.dot(bs_ref[gi], jnp.concatenate(cl[:sub], axis=0).T,
                    precision=lax.Precision.HIGHEST, preferred_element_type=F32)
        rows = [r] + [jnp.where(lane >= S5_GROUP * s, pltpu.roll(r, S5_GROUP * s, axis=1), 0.0)
                      for s in range(1, sub)]
        tt_ref[gi] = jnp.concatenate(rows, axis=0).T.astype(tt_ref.dtype)
        nt = [b1 * p[sub - 1 - s:sub - s] + b2 * q[sub - 1 - s:sub - s] for s in range(sub)]
        nn_ref[gi] = jnp.concatenate(nt, axis=0).T.astype(nn_ref.dtype)


def _s5_tab_call(p, q, c1, c2, b1, b2, bs):
    groups = p.shape[0]
    gt = S5_TAB_GROUPS if groups % S5_TAB_GROUPS == 0 else 1
    width = S5_SUB * S5_GROUP
    st2 = 2 * S5_STATE
    blk = lambda a: pl.BlockSpec((gt,) + a.shape[1:], lambda i: (i, 0, 0))
    out = lambda r, c: pl.BlockSpec((gt, r, c), lambda i: (i, 0, 0))
    return pl.pallas_call(
        _s5_tab_kernel,
        grid=(groups // gt,),
        in_specs=[blk(a) for a in (p, q, c1, c2, b1, b2, bs)],
        out_specs=[out(width, width), out(st2, width), out(width, st2)],
        out_shape=[jax.ShapeDtypeStruct((groups, width, width), BF16),
                   jax.ShapeDtypeStruct((groups, st2, width), BF16),
                   jax.ShapeDtypeStruct((groups, width, st2), BF16)],
        compiler_params=_cparams(("parallel",)),
        name="s5_tab",
    )(p, q, c1, c2, b1, b2, bs)


def _s5_tables(lam_re, lam_im, log_step, b_re, b_im, c_re, c_im, nseg):
    sub = S5_SUB
    step = jnp.exp(log_step)[:, None]
    dre, dim = lam_re * step, lam_im * step

    def power(k):
        kk = jnp.asarray(k, F32)[None, :, None]
        mag = jnp.exp(kk * dre[:, None, :])
        return mag * jnp.cos(kk * dim[:, None, :]), mag * jnp.sin(kk * dim[:, None, :])

    pw_re, pw_im = power(jnp.arange(sub + 1))
    den = lam_re * lam_re + lam_im * lam_im
    nr, ni = pw_re[:, 1] - 1.0, pw_im[:, 1]
    cf_re = ((nr * lam_re + ni * lam_im) / den)[:, None, :]
    cf_im = ((ni * lam_re - nr * lam_im) / den)[:, None, :]
    bt_re, bt_im = b_re.transpose(0, 2, 1), b_im.transpose(0, 2, 1)
    bb_re = cf_re * bt_re - cf_im * bt_im
    bb_im = cf_re * bt_im + cf_im * bt_re
    cat = lambda a, b: jnp.concatenate([a, b], axis=-1)
    tt, nn, mm = _s5_tab_call(cat(pw_re, pw_im), cat(pw_im, pw_re), cat(c_re, c_re),
                              cat(-c_im, c_im), cat(bb_re, bb_re), cat(-bb_im, bb_im),
                              cat(bb_re, -bb_im))
    a_re = jnp.broadcast_to(pw_re[:, sub].reshape(-1, 1), (pw_re[:, sub].size, nseg))
    a_im = jnp.broadcast_to(pw_im[:, sub].reshape(-1, 1), (pw_im[:, sub].size, nseg))
    return tt, nn, mm, a_re, a_im


def _s5_layer(x, batch, seq, norm_w, lam_re, lam_im, log_step, b_re, b_im, c_re, c_im, d, glu_w,
              glu_b, ffn_g, ffn_wg, ffn_wu, ffn_wd, ffn_layer, row, hc):
    t, dm = x.shape
    nseg = V7X_LANES
    seg_len = t // nseg
    assert t % nseg == 0 and seg_len % S5_SUB == 0 and seq % seg_len == 0
    nsb = seq // seg_len
    tt, nn, mm, a_re, a_im = _s5_tables(lam_re, lam_im, log_step, b_re, b_im, c_re, c_im, nseg)
    zt, er, ei = _s5_pre_call(x.reshape(nseg, seg_len, dm), norm_w, nn, a_re, a_im)
    y = _s5_core_call(zt, tt, nn, mm, a_re, a_im, er, ei, nsb, dm).reshape(t, dm)
    return _s5_ffn_call(x, y, norm_w, d.reshape(-1), glu_w.astype(BF16), glu_b, ffn_g,
                        ffn_wg, ffn_wu, ffn_wd, ffn_layer, row, hc)


def _ret_kernel(x_ref, g_ref, w_ref, cr_ref, sr_ref, cb_ref, sb_ref, gn_ref, dm_ref, qd_ref, kd_ref, cd_ref,
                y_ref, state_ref, *, heads, blk):
    @pl.when(pl.program_id(1) == 0)
    def _():
        state_ref[...] = jnp.zeros_like(state_ref)

    u = _rms(x_ref[...], g_ref[...]).astype(BF16)
    cb = cb_ref[pl.ds(pl.program_id(1), 1), :]
    sb = sb_ref[pl.ds(pl.program_id(1), 1), :]
    cos = cr_ref[...] * cb - sr_ref[...] * sb
    sin = sr_ref[...] * cb + cr_ref[...] * sb
    half = RET_QK_DIM // 2
    qk_w = heads * RET_QK_DIM
    v_w = heads * RET_V_DIM
    k_scale = RET_QK_DIM ** -0.5
    row = x_ref.shape[0]

    def rotary(tq):
        t1, t2 = tq[:, :half], tq[:, half:]
        return jnp.concatenate([t1 * cos - t2 * sin, t1 * sin + t2 * cos], axis=-1)

    for h in range(heads):
        cq = h * RET_QK_DIM
        cv = 2 * qk_w + h * RET_V_DIM
        qf = rotary(_dot(u, w_ref[:, cq:cq + RET_QK_DIM]))
        q = qf.astype(BF16)
        k = rotary(_dot(u, w_ref[:, qk_w + cq:qk_w + cq + RET_QK_DIM]) * k_scale)
        v = _dot(u, w_ref[:, cv:cv + RET_V_DIM]).astype(BF16)
        gt = _dot(u, w_ref[:, v_w + cv:v_w + cv + RET_V_DIM])
        gn = gn_ref[:, h * RET_V_DIM:(h + 1) * RET_V_DIM]
        for r0 in range(0, row, blk):
            qs, ks, vs = q[r0:r0 + blk], k[r0:r0 + blk], v[r0:r0 + blk]
            s = lax.dot_general(qs, ks.astype(BF16), (((1,), (1,)), ((), ())),
                                preferred_element_type=F32)
            st = state_ref[h]
            qd = (qf[r0:r0 + blk] * qd_ref[h]).astype(BF16)
            o = _dot((s * dm_ref[h]).astype(BF16), vs) + _dot(qd, st.astype(BF16))
            kd = (ks * kd_ref[h]).astype(BF16)
            state_ref[h] = st * cd_ref[h] + lax.dot_general(
                kd, vs, (((0,), (0,)), ((), ())), preferred_element_type=F32)
            mean = jnp.mean(o, axis=-1, keepdims=True)
            cen = o - mean
            var = jnp.mean(jnp.square(cen), axis=-1, keepdims=True)
            on = cen * lax.rsqrt(var + NORM_EPS) * gn
            y_ref[r0:r0 + blk, h * RET_V_DIM:(h + 1) * RET_V_DIM] = (
                jax.nn.silu(gt[r0:r0 + blk]) * on).astype(y_ref.dtype)


def _ret_call(x, g, w, rot, gn_w, dmask, qdec, kdec, cdec, batch, seq, heads, row, blk):
    t, dm = x.shape
    v_w = heads * RET_V_DIM
    nb = seq // row
    kern = functools.partial(_ret_kernel, heads=heads, blk=blk)
    rows = lambda width: pl.BlockSpec((row, width), lambda b, i: (b * nb + i, 0))
    tabs = [_resident(a.shape) for a in rot]
    return pl.pallas_call(
        kern,
        grid=(batch, nb),
        in_specs=[rows(dm), _resident((1, dm)), _resident(w.shape)] + tabs + [
                  _resident((1, v_w)), _resident(dmask.shape), _resident(qdec.shape),
                  _resident(kdec.shape), _resident(cdec.shape)],
        out_specs=rows(v_w),
        out_shape=jax.ShapeDtypeStruct((t, v_w), BF16),
        scratch_shapes=[pltpu.VMEM((heads, RET_QK_DIM, RET_V_DIM), F32)],
        compiler_params=_cparams(("parallel", "arbitrary")),
        name="ret_mix",
    )(x, g.reshape(1, dm), w, *rot, gn_w.reshape(1, v_w), dmask, qdec, kdec, cdec)


def _ret_decays(heads, blk):
    log_gamma = jnp.log1p(-jnp.exp2(-5.0 - jnp.arange(heads, dtype=F32)))
    pos = jnp.arange(blk, dtype=F32)
    diff = pos[:, None] - pos[None, :]
    cn = (jnp.arange(blk) // CHUNK)[:, None]
    cm = (jnp.arange(blk) // CHUNK)[None, :]
    expo = jnp.where(cn == cm, jnp.abs(diff), diff)
    dmask = jnp.where((cm <= cn)[None], jnp.exp(log_gamma[:, None, None] * expo[None]), 0.0)
    qdec = jnp.exp((pos[None, :] + 1.0) * log_gamma[:, None])[..., None]
    kdec = jnp.exp((blk - 1.0 - pos)[None, :] * log_gamma[:, None])[..., None]
    cdec = jnp.exp(blk * log_gamma)[:, None, None]
    return dmask, qdec, kdec, cdec


def _rotary_tables(seq, row):
    inv_freq = 1.0 / (ROPE_BASE ** jnp.linspace(0.0, 1.0, RET_QK_DIM // 2, dtype=F32))
    ang_r = jnp.arange(row, dtype=F32)[:, None] * inv_freq[None, :]
    ang_b = (jnp.arange(seq // row, dtype=F32) * row)[:, None] * inv_freq[None, :]
    return jnp.cos(ang_r), jnp.sin(ang_r), jnp.cos(ang_b), jnp.sin(ang_b)


def _ret_ffn_kernel(x_ref, y_ref, w_ref, fg_ref, wg_ref, wu_ref, wd_ref, fin_ref, o_ref, acc_ref,
                    *, hc):
    x1 = x_ref[...] + _dot(y_ref[...], w_ref[...])
    out = _ffn_rows(x1, fg_ref, wg_ref, wu_ref, wd_ref, acc_ref, hc)
    o_ref[...] = _rms(out, fin_ref[...])


def _ret_ffn_call(x, y, w, fg, wg, wu, wd, layer, fin, row, hc):
    t, dm = x.shape
    hidden = wg.shape[-1]
    kdim = y.shape[1]
    tile, weights = _ffn_specs(row, dm, hidden, layer, hc)
    return pl.pallas_call(
        functools.partial(_ret_ffn_kernel, hc=hc),
        grid=(t // row,),
        in_specs=[tile, pl.BlockSpec((row, kdim), lambda i: (i, 0)), _resident((kdim, dm))]
        + weights + [_resident((1, dm))],
        out_specs=tile,
        out_shape=jax.ShapeDtypeStruct((t, dm), F32),
        scratch_shapes=[pltpu.VMEM((row, dm), F32)],
        compiler_params=_cparams(("parallel",)),
        name="ret_ffn",
    )(x, y, w, fg.reshape(1, dm), wg, wu, wd, fin.reshape(1, dm))


def _ret_layer(x, batch, seq, norm_w, w_qkvg, gn_w, w_o, ffn_g, ffn_wg, ffn_wu, ffn_wd, ffn_layer,
               fin, row, row_out, blk, hc):
    dm = x.shape[1]
    heads = dm // RET_QK_DIM
    rot = _rotary_tables(seq, row)
    dmask, qdec, kdec, cdec = _ret_decays(heads, blk)
    y = _ret_call(x, norm_w, w_qkvg.astype(BF16), rot, gn_w, dmask, qdec, kdec, cdec,
                  batch, seq, heads, row, blk)
    return _ret_ffn_call(x, y, w_o.astype(BF16), ffn_g, ffn_wg, ffn_wu, ffn_wd, ffn_layer, fin,
                         row_out, hc)


def kernel(x, s5_norm, s5_lambda_re, s5_lambda_im, s5_log_step, s5_b_re, s5_b_im, s5_c_re, s5_c_im,
           s5_d, s5_glu_w, s5_glu_b, ret_norm, ret_w_qkvg, ret_gn_w, ret_w_o, ffn_norm, ffn_w_gate,
           ffn_w_up, ffn_w_down, final_norm):
    batch, seq, dm = x.shape
    row, row_out, blk, hc = _tiles(seq)
    assert ffn_norm.shape[0] == 2 and s5_norm.shape[0] == 1 and ret_norm.shape[0] == 1
    h = x.reshape(batch * seq, dm)
    wg, wu, wd = ffn_w_gate.astype(BF16), ffn_w_up.astype(BF16), ffn_w_down.astype(BF16)
    h = _s5_layer(h, batch, seq, s5_norm[0], s5_lambda_re[0], s5_lambda_im[0], s5_log_step[0],
                  s5_b_re[0], s5_b_im[0], s5_c_re[0], s5_c_im[0], s5_d[0], s5_glu_w[0],
                  s5_glu_b[0], ffn_norm[0], wg, wu, wd, 0, row, hc)
    h = _ret_layer(h, batch, seq, ret_norm[0], ret_w_qkvg[0], ret_gn_w[0], ret_w_o[0],
                   ffn_norm[1], wg, wu, wd, 1, final_norm,
                   row, row_out, blk, hc)
    return h.reshape(batch, seq, dm)
```
